```python
import math
import jax, jax.numpy as jnp
from jax import lax
import numpy as np

D_MODEL = 1024
BATCH = 8
SEQ = 2048
DEPTH = 1

PLE_DIM = 256
EPS = 1e-6
ROPE_THETA = 10000.0
MOBA_HEADS = 8
MOBA_HEAD_DIM = 64
MOBA_BLOCK = 256
MOBA_TOPK = 3
MOBA_Q_CHUNK = 16
MOBA_WIDTH = MOBA_HEADS * MOBA_HEAD_DIM
MLA_HEADS = 8
MLA_Q_LORA = 256
MLA_KV_LORA = 128
MLA_NOPE_DIM = 64
MLA_ROPE_DIM = 32
MLA_V_DIM = 64
MLA_QK_DIM = MLA_NOPE_DIM + MLA_ROPE_DIM
MLA_WIDTH = MLA_HEADS * MLA_V_DIM
MLA_Q_BLOCK = 128
N_EXPERTS = 32
TOP_K = 4
D_EXPERT = 1024
SWIGLU_LIMIT = 7.0
SWIGLU_ALPHA = 1.702
EXPERT_ROW_BLOCK = 128
IN_SPLITS = (MOBA_WIDTH, MOBA_WIDTH, MOBA_WIDTH, MLA_Q_LORA, MLA_KV_LORA, MLA_ROPE_DIM, D_MODEL, D_MODEL)
D_IN = 3 * MOBA_WIDTH + MLA_Q_LORA + MLA_KV_LORA + MLA_ROPE_DIM + 2 * D_MODEL

kernel_name = "hybrid_moba_mla_moe_ple_block"


def rmsnorm(x, g):
    xf = x.astype(jnp.float32)
    y = xf * lax.rsqrt(jnp.mean(xf * xf, axis=-1, keepdims=True) + EPS)
    return (y * g.astype(jnp.float32)).astype(x.dtype)


def apply_rope(x):
    s, dim = x.shape[-2], x.shape[-1]
    half = dim // 2
    inv_freq = ROPE_THETA ** (-(jnp.arange(half, dtype=jnp.float32) / half))
    ang = jnp.arange(s, dtype=jnp.float32)[:, None] * inv_freq[None, :]
    cos, sin = jnp.cos(ang), jnp.sin(ang)
    xf = x.astype(jnp.float32)
    x1, x2 = xf[..., :half], xf[..., half:]
    return jnp.concatenate([x1 * cos - x2 * sin, x2 * cos + x1 * sin], axis=-1).astype(x.dtype)


def moba_attention(q, k, v):
    b, h, s, hd = q.shape
    nb = -(-s // MOBA_BLOCK)
    s_pad = nb * MOBA_BLOCK
    padcfg = ((0, 0), (0, 0), (0, s_pad - s), (0, 0))
    q, k, v = jnp.pad(q, padcfg), jnp.pad(k, padcfg), jnp.pad(v, padcfg)
    scale = hd ** -0.5
    q_blk = q.reshape(b, h, nb, MOBA_BLOCK, hd)
    k_blk = k.reshape(b, h, nb, MOBA_BLOCK, hd)
    v_blk = v.reshape(b, h, nb, MOBA_BLOCK, hd)
    k_mean = jnp.mean(k_blk.astype(jnp.float32), axis=3)
    gate = jnp.einsum('bhsd,bhnd->bhsn', q.astype(jnp.float32), k_mean)
    cur_blk = jnp.arange(s_pad) // MOBA_BLOCK
    is_past = jnp.arange(nb)[None, :] < cur_blk[:, None]
    gate = jnp.where(is_past, gate, -jnp.inf)
    n_sel = min(MOBA_TOPK, nb)
    _, sel = lax.top_k(gate, n_sel)
    sel_valid = sel < cur_blk[:, None]
    s_own = jnp.einsum('bhnqd,bhnkd->bhnqk', q_blk, k_blk) * scale
    causal = jnp.tril(jnp.ones((MOBA_BLOCK, MOBA_BLOCK), dtype=bool))
    s_own = jnp.where(causal, s_own, -jnp.inf).reshape(b, h, s_pad, MOBA_BLOCK)
    gather_blocks = jax.vmap(jax.vmap(lambda blocks, ix: blocks[ix]))

    def chunk(ci):
        start = ci * MOBA_Q_CHUNK
        qc = lax.dynamic_slice_in_dim(q, start, MOBA_Q_CHUNK, axis=2)
        ix = lax.dynamic_slice_in_dim(sel, start, MOBA_Q_CHUNK, axis=2)
        ok = lax.dynamic_slice_in_dim(sel_valid, start, MOBA_Q_CHUNK, axis=2)
        so = lax.dynamic_slice_in_dim(s_own, start, MOBA_Q_CHUNK, axis=2)
        v_own = lax.dynamic_index_in_dim(v_blk, start // MOBA_BLOCK, axis=2, keepdims=False)
        k_sel = gather_blocks(k_blk, ix)
        v_sel = gather_blocks(v_blk, ix)
        ss = jnp.einsum('bhqd,bhqjkd->bhqjk', qc, k_sel) * scale
        ss = jnp.where(ok[..., None], ss, -jnp.inf)
        logits = jnp.concatenate([so, ss.reshape(b, h, MOBA_Q_CHUNK, n_sel * MOBA_BLOCK)], axis=-1)
        probs = jax.nn.softmax(logits.astype(jnp.float32), axis=-1).astype(v.dtype)
        p_own = probs[..., :MOBA_BLOCK]
        p_sel = probs[..., MOBA_BLOCK:].reshape(b, h, MOBA_Q_CHUNK, n_sel, MOBA_BLOCK)
        return (jnp.einsum('bhqk,bhkd->bhqd', p_own, v_own)
                + jnp.einsum('bhqjk,bhqjkd->bhqd', p_sel, v_sel))

    out = lax.map(chunk, jnp.arange(s_pad // MOBA_Q_CHUNK))
    out = jnp.moveaxis(out, 0, 2).reshape(b, h, s_pad, hd)
    return out[:, :, :s]


def causal_attention(q, k, v):
    b, h, s, dq = q.shape
    scale = dq ** -0.5
    kpos = jnp.arange(s)

    def blk(ci):
        start = ci * MLA_Q_BLOCK
        qc = lax.dynamic_slice_in_dim(q, start, MLA_Q_BLOCK, axis=2)
        sc = jnp.einsum('bhqd,bhkd->bhqk', qc, k).astype(jnp.float32) * scale
        qpos = start + jnp.arange(MLA_Q_BLOCK)
        sc = jnp.where(kpos[None, :] <= qpos[:, None], sc, -jnp.inf)
        pr = jax.nn.softmax(sc, axis=-1).astype(v.dtype)
        return jnp.einsum('bhqk,bhkd->bhqd', pr, v)

    out = lax.map(blk, jnp.arange(s // MLA_Q_BLOCK))
    return jnp.moveaxis(out, 0, 2).reshape(b, h, s, v.shape[-1])


def routed_experts(h, w_router, b_router, w_gate_up, b_gate_up, w_down, b_down):
    b, s, d = h.shape
    n = b * s
    hf = h.reshape(n, d)
    logits = (hf @ w_router).astype(jnp.float32) + b_router.astype(jnp.float32)
    top_vals, top_idx = lax.top_k(logits, TOP_K)
    top_w = jax.nn.softmax(top_vals, axis=-1)
    nk = n * TOP_K
    m = EXPERT_ROW_BLOCK
    flat_e = top_idx.reshape(nk)
    flat_t = jnp.arange(nk, dtype=jnp.int32) // TOP_K
    flat_w = top_w.reshape(nk)
    order = jnp.argsort(flat_e)
    se = flat_e[order]
    counts = jnp.bincount(flat_e, length=N_EXPERTS)
    starts = jnp.cumsum(counts) - counts
    pcounts = ((counts + m - 1) // m) * m
    pends = jnp.cumsum(pcounts)
    pstarts = pends - pcounts
    dest = pstarts[se] + (jnp.arange(nk) - starts[se])
    n_rows = ((nk + m - 1) // m) * m + N_EXPERTS * m
    n_blocks = n_rows // m
    row_tok = jnp.full((n_rows,), n, dtype=jnp.int32).at[dest].set(flat_t[order])
    row_w = jnp.zeros((n_rows,), jnp.float32).at[dest].set(flat_w[order])
    blk_e = jnp.clip(jnp.searchsorted(pends, jnp.arange(n_blocks) * m, side='right'), 0, N_EXPERTS - 1)
    h_pad = jnp.concatenate([hf, jnp.zeros((1, d), hf.dtype)], axis=0)
    xr = h_pad[row_tok].reshape(n_blocks, m, d)

    def expert_block(args):
        xb, e = args
        gu = xb @ w_gate_up[e] + b_gate_up[e]
        g, u = gu[:, :D_EXPERT], gu[:, D_EXPERT:]
        g = jnp.minimum(g, SWIGLU_LIMIT)
        u = jnp.clip(u, -SWIGLU_LIMIT, SWIGLU_LIMIT)
        glu = g * jax.nn.sigmoid(SWIGLU_ALPHA * g)
        return ((u + 1.0) * glu) @ w_down[e] + b_down[e]

    yr = lax.map(expert_block, (xr, blk_e)).reshape(n_rows, d)
    y = jax.ops.segment_sum(yr * row_w[:, None].astype(yr.dtype), row_tok, num_segments=n + 1)[:n]
    return y.reshape(b, s, d).astype(h.dtype)


def hybrid_layer(x, p_i, g_mix, w_in, moba_q_norm, moba_k_norm, mla_q_lat_norm, w_uq,
                 mla_kv_lat_norm, w_ukv, mla_q_norm, mla_k_norm, w_branch_a, w_branch_b, w_out,
                 g_ffn, w_router, b_router, w_gate_up, b_gate_up, w_down, b_down,
                 g_ple, w_ple_gate, w_ple_proj):
    b, s, d = x.shape
    hn = rmsnorm(x, g_mix)
    proj = hn @ w_in
    parts, off = [], 0
    for wdt in IN_SPLITS:
        parts.append(proj[..., off:off + wdt])
        off += wdt
    q_a, k_a, v_a, c_q, c_kv, k_pe, gate_a, gate_b = parts

    heads_a = lambda t: t.reshape(b, s, MOBA_HEADS, MOBA_HEAD_DIM).transpose(0, 2, 1, 3)
    q_a = apply_rope(rmsnorm(heads_a(q_a), moba_q_norm))
    k_a = apply_rope(rmsnorm(heads_a(k_a), moba_k_norm))
    y_a = moba_attention(q_a, k_a, heads_a(v_a))
    y_a = y_a.transpose(0, 2, 1, 3).reshape(b, s, MOBA_WIDTH)

    q_b = (rmsnorm(c_q, mla_q_lat_norm) @ w_uq).reshape(b, s, MLA_HEADS, MLA_QK_DIM).transpose(0, 2, 1, 3)
    kv = (rmsnorm(c_kv, mla_kv_lat_norm) @ w_ukv).reshape(b, s, MLA_HEADS, MLA_NOPE_DIM + MLA_V_DIM).transpose(0, 2, 1, 3)
    k_nope, v_b = kv[..., :MLA_NOPE_DIM], kv[..., MLA_NOPE_DIM:]
    k_pe_h = jnp.broadcast_to(k_pe[:, None], (b, MLA_HEADS, s, MLA_ROPE_DIM))
    k_b = jnp.concatenate([k_nope, k_pe_h], axis=-1)
    q_b = rmsnorm(q_b, mla_q_norm)
    k_b = rmsnorm(k_b, mla_k_norm)
    q_b = jnp.concatenate([q_b[..., :MLA_NOPE_DIM], apply_rope(q_b[..., MLA_NOPE_DIM:])], axis=-1)
    k_b = jnp.concatenate([k_b[..., :MLA_NOPE_DIM], apply_rope(k_b[..., MLA_NOPE_DIM:])], axis=-1)
    y_b = causal_attention(q_b, k_b, v_b)
    y_b = y_b.transpose(0, 2, 1, 3).reshape(b, s, MLA_WIDTH)

    merged = jax.nn.sigmoid(gate_a) * (y_a @ w_branch_a) + jax.nn.sigmoid(gate_b) * (y_b @ w_branch_b)
    x = x + merged @ w_out

    x = x + routed_experts(rmsnorm(x, g_ffn), w_router, b_router, w_gate_up, b_gate_up, w_down, b_down)

    hp = rmsnorm(x, g_ple)
    x = x + jax.nn.sigmoid(hp @ w_ple_gate) * (p_i @ w_ple_proj)
    return x


def setup_inputs(seed: int = 0) -> dict:
    key = jax.random.key(seed)
    ks = jax.random.split(key, 32)
    f32 = jnp.float32

    def nrm(k, shape, scale):
        return jax.random.normal(k, shape, f32) * scale

    def gain(k, dim):
        return 1.0 + 0.02 * jax.random.normal(k, (DEPTH, dim), f32)

    L = DEPTH
    return {
        "x": nrm(ks[0], (BATCH, SEQ, D_MODEL), 1.0),
        "p": nrm(ks[1], (DEPTH, BATCH, SEQ, PLE_DIM), 1.0),
        "g_mix": gain(ks[2], D_MODEL),
        "w_in": nrm(ks[3], (L, D_MODEL, D_IN), D_MODEL ** -0.5),
        "moba_q_norm": gain(ks[4], MOBA_HEAD_DIM),
        "moba_k_norm": gain(ks[5], MOBA_HEAD_DIM),
        "mla_q_lat_norm": gain(ks[6], MLA_Q_LORA),
        "w_uq": nrm(ks[7], (L, MLA_Q_LORA, MLA_HEADS * MLA_QK_DIM), MLA_Q_LORA ** -0.5),
        "mla_kv_lat_norm": gain(ks[8], MLA_KV_LORA),
        "w_ukv": nrm(ks[9], (L, MLA_KV_LORA, MLA_HEADS * (MLA_NOPE_DIM + MLA_V_DIM)), MLA_KV_LORA ** -0.5),
        "mla_q_norm": gain(ks[10], MLA_QK_DIM),
        "mla_k_norm": gain(ks[11], MLA_QK_DIM),
        "w_branch_a": nrm(ks[12], (L, MOBA_WIDTH, D_MODEL), MOBA_WIDTH ** -0.5),
        "w_branch_b": nrm(ks[13], (L, MLA_WIDTH, D_MODEL), MLA_WIDTH ** -0.5),
        "w_out": nrm(ks[14], (L, D_MODEL, D_MODEL), D_MODEL ** -0.5),
        "g_ffn": gain(ks[15], D_MODEL),
        "w_router": nrm(ks[16], (L, D_MODEL, N_EXPERTS), D_MODEL ** -0.5),
        "b_router": nrm(ks[17], (L, N_EXPERTS), 0.01),
        "w_gate_up": nrm(ks[18], (L, N_EXPERTS, D_MODEL, 2 * D_EXPERT), D_MODEL ** -0.5),
        "b_gate_up": nrm(ks[19], (L, N_EXPERTS, 2 * D_EXPERT), 0.01),
        "w_down": nrm(ks[20], (L, N_EXPERTS, D_EXPERT, D_MODEL), D_EXPERT ** -0.5),
        "b_down": nrm(ks[21], (L, N_EXPERTS, D_MODEL), 0.01),
        "g_ple": gain(ks[22], D_MODEL),
        "w_ple_gate": nrm(ks[23], (L, D_MODEL, D_MODEL), D_MODEL ** -0.5),
        "w_ple_proj": nrm(ks[24], (L, PLE_DIM, D_MODEL), PLE_DIM ** -0.5),
    }


def reference(x, p, g_mix, w_in, moba_q_norm, moba_k_norm, mla_q_lat_norm, w_uq,
              mla_kv_lat_norm, w_ukv, mla_q_norm, mla_k_norm, w_branch_a, w_branch_b, w_out,
              g_ffn, w_router, b_router, w_gate_up, b_gate_up, w_down, b_down,
              g_ple, w_ple_gate, w_ple_proj):
    for i in range(DEPTH):
        x = hybrid_layer(x, p[i], g_mix[i], w_in[i], moba_q_norm[i], moba_k_norm[i],
                         mla_q_lat_norm[i], w_uq[i], mla_kv_lat_norm[i], w_ukv[i],
                         mla_q_norm[i], mla_k_norm[i], w_branch_a[i], w_branch_b[i], w_out[i],
                         g_ffn[i], w_router[i], b_router[i], w_gate_up[i], b_gate_up[i],
                         w_down[i], b_down[i], g_ple[i], w_ple_gate[i], w_ple_proj[i])
    return x
```

```python
import functools
import math

import jax
import jax.numpy as jnp
from jax import lax
from jax.experimental import pallas as pl
from jax.experimental.pallas import tpu as pltpu

F32 = jnp.float32
BF16 = jnp.bfloat16

D_MODEL = 1024
PLE_DIM = 256
EPS = 1e-6
ROPE_THETA = 10000.0
MOBA_HEADS = 8
MOBA_HEAD_DIM = 64
MOBA_BLOCK = 256
MOBA_TOPK = 3
MOBA_WIDTH = MOBA_HEADS * MOBA_HEAD_DIM
MLA_HEADS = 8
MLA_Q_LORA = 256
MLA_KV_LORA = 128
MLA_NOPE_DIM = 64
MLA_ROPE_DIM = 32
MLA_V_DIM = 64
MLA_QK_DIM = MLA_NOPE_DIM + MLA_ROPE_DIM
MLA_WIDTH = MLA_HEADS * MLA_V_DIM
N_EXPERTS = 32
TOP_K = 4
D_EXPERT = 1024
SWIGLU_LIMIT = 7.0
SWIGLU_ALPHA = 1.702

LANES = 128
SUBLANES = 8
ROW_SLABS = D_MODEL // LANES
VMEM_LIMIT = 56 * 1024 * 1024

TOKEN_TILE = 256
ATTN_TILE = 256
EXPERT_ROWS = 256
DISPATCH_TILE = 512
FINAL_TILE = 128

NEG = -1e30
MASK_BIAS = -1e9

C_QA, C_KA, C_VA = 0, 512, 1024
C_CQ, C_CKV, C_KPE = 1536, 1792, 1920
C_GA, C_GB = 2048, 3072
D_IN_PACKED = 4096


def _rms(x, gain):
    return x * lax.rsqrt(jnp.mean(x * x, axis=-1, keepdims=True) + EPS) * gain


def _rope(t, cos, s1, s2, half):
    return t * cos + pltpu.roll(t, LANES - half, 1) * s1 + pltpu.roll(t, half, 1) * s2


def _inproj_kernel(x_ref, gmix_ref, win_ref, gqa_ref, gka_ref, cosa_ref, s1a_ref, s2a_ref,
                   gql_ref, wuq_ref, gkvl_ref, wuk_ref, wuv_ref, gqb_ref, gkb_ref,
                   cosb_ref, s1b_ref, s2b_ref,
                   qa_ref, ka_ref, va_ref, kmean_ref, qb_ref, kb_ref, vb_ref, ga_ref, gb_ref):
    hn = _rms(x_ref[...], gmix_ref[...]).astype(BF16)

    def proj(c0, width):
        return jnp.dot(hn, win_ref[:, c0:c0 + width], preferred_element_type=F32)

    lane = lax.broadcasted_iota(jnp.int32, (TOKEN_TILE, LANES), 1)
    first = lane < MOBA_HEAD_DIM
    cosa, s1a, s2a = cosa_ref[...], s1a_ref[...], s2a_ref[...]

    def moba_norm_rope(t, gain):
        sq = t * t
        ss0 = jnp.sum(jnp.where(first, sq, 0.0), axis=-1, keepdims=True)
        ss1 = jnp.sum(jnp.where(first, 0.0, sq), axis=-1, keepdims=True)
        ms = jnp.where(first, ss0, ss1) * (1.0 / MOBA_HEAD_DIM)
        t = t * lax.rsqrt(ms + EPS) * gain
        return _rope(t, cosa, s1a, s2a, MOBA_HEAD_DIM // 2)

    qa = proj(C_QA, MOBA_WIDTH)
    ka = proj(C_KA, MOBA_WIDTH)
    for c in range(MOBA_WIDTH // LANES):
        sl = slice(c * LANES, (c + 1) * LANES)
        qa_ref[:, sl] = moba_norm_rope(qa[:, sl], gqa_ref[...]).astype(BF16)
        kc = moba_norm_rope(ka[:, sl], gka_ref[...])
        ka_ref[:, sl] = kc.astype(BF16)
        kmean_ref[0, :, sl] = jnp.mean(kc, axis=0, keepdims=True)
    va_ref[...] = proj(C_VA, MOBA_WIDTH).astype(BF16)

    cosb, s1b, s2b = cosb_ref[...], s1b_ref[...], s2b_ref[...]

    def mla_norm_rope(t, gain):
        ms = jnp.sum(t * t, axis=-1, keepdims=True) * (1.0 / MLA_QK_DIM)
        t = t * lax.rsqrt(ms + EPS) * gain
        return _rope(t, cosb, s1b, s2b, MLA_ROPE_DIM // 2)

    cq = _rms(proj(C_CQ, MLA_Q_LORA), gql_ref[...]).astype(BF16)
    qb = jnp.dot(cq, wuq_ref[...], preferred_element_type=F32)
    ckv = _rms(proj(C_CKV, MLA_KV_LORA), gkvl_ref[...]).astype(BF16)
    kn = jnp.dot(ckv, wuk_ref[...], preferred_element_type=F32)
    kpe = proj(C_KPE, LANES)
    for h in range(MLA_HEADS):
        sl = slice(h * LANES, (h + 1) * LANES)
        qb_ref[:, sl] = mla_norm_rope(qb[:, sl], gqb_ref[...]).astype(BF16)
        kb_ref[:, sl] = mla_norm_rope(kn[:, sl] + kpe, gkb_ref[...]).astype(BF16)
    vb_ref[...] = jnp.dot(ckv, wuv_ref[...], preferred_element_type=F32).astype(BF16)

    ga_ref[...] = jax.nn.sigmoid(proj(C_GA, D_MODEL)).astype(BF16)
    gb_ref[...] = jax.nn.sigmoid(proj(C_GB, D_MODEL)).astype(BF16)


_NT = (((1,), (1,)), ((), ()))


def _attn_kernel(*refs, moba):
    if moba:
        q_ref, k_ref, v_ref, kmean_ref, o_ref, m_scr, l_scr, acc_scr = refs
    else:
        q_ref, k_ref, v_ref, o_ref, m_scr, l_scr, acc_scr = refs
    t = ATTN_TILE
    qi = pl.program_id(2)
    lane = lax.broadcasted_iota(jnp.int32, (t, LANES), 1)
    row = lax.broadcasted_iota(jnp.int32, (t, t), 0)
    col = lax.broadcasted_iota(jnp.int32, (t, t), 1)
    diag_start = pl.multiple_of(qi * t, t)

    def online_update(s, v_blk):
        m_prev = m_scr[...]
        m_new = jnp.maximum(m_prev, jnp.max(s, axis=-1, keepdims=True))
        alpha = jnp.exp(m_prev - m_new)
        p = jnp.exp(s - m_new)
        l_scr[...] = alpha * l_scr[...] + jnp.sum(p, axis=-1, keepdims=True)
        acc_scr[...] = alpha * acc_scr[...] + jnp.dot(p.astype(BF16), v_blk,
                                                      preferred_element_type=F32)
        m_scr[...] = m_new

    outs = []
    for hh in range(2):
        if moba:
            head_lanes = (lane < MOBA_HEAD_DIM) if hh == 0 else (lane >= MOBA_HEAD_DIM)
            q = jnp.where(head_lanes, q_ref[0], jnp.zeros((), BF16))
            kcols = slice(0, LANES)
            gate = lax.dot_general(q, kmean_ref[0], _NT, preferred_element_type=F32)
            g = jnp.where(lane < qi, gate, -jnp.inf)
            keep = jnp.zeros((t, LANES), F32)
            for _ in range(MOBA_TOPK):
                gmax = jnp.max(g, axis=-1, keepdims=True)
                pick = jnp.min(jnp.where(g == gmax, lane, LANES), axis=-1, keepdims=True)
                hit = lane == jnp.where(gmax > -jnp.inf, pick, LANES)
                keep = jnp.where(hit, 1.0, keep)
                g = jnp.where(hit, -jnp.inf, g)
            bias = jnp.where(keep > 0.0, 0.0, MASK_BIAS).astype(BF16)
            q_full = jnp.concatenate([q, bias], axis=1)
        else:
            kcols = slice(hh * LANES, (hh + 1) * LANES)
            q = q_ref[0, :, kcols]
            q_full = q

        m_scr[...] = jnp.full((t, 1), NEG, F32)
        l_scr[...] = jnp.zeros((t, 1), F32)
        acc_scr[...] = jnp.zeros((t, LANES), F32)

        def past_block(j, carry, q_full=q_full, kcols=kcols):
            start = pl.multiple_of(j * t, t)
            k_blk = k_ref[0, pl.ds(start, t), kcols]
            if moba:
                one_hot = jnp.where(lane == j, 1.0, 0.0).astype(BF16)
                k_blk = jnp.concatenate([k_blk, one_hot], axis=1)
            s = lax.dot_general(q_full, k_blk, _NT, preferred_element_type=F32)
            online_update(s, v_ref[0, pl.ds(start, t), :])
            return carry

        lax.fori_loop(0, qi, past_block, 0)

        k_blk = k_ref[0, pl.ds(diag_start, t), kcols]
        s = lax.dot_general(q, k_blk, _NT, preferred_element_type=F32)
        online_update(jnp.where(col <= row, s, NEG), v_ref[0, pl.ds(diag_start, t), :])
        outs.append(acc_scr[...] / l_scr[...])

    o_ref[0] = jnp.where(lane < MOBA_HEAD_DIM, outs[0], outs[1]).astype(BF16)


def _attention(q, k, v, kmean, *, moba):
    b, s, _ = q.shape
    groups = v.shape[-1] // LANES
    qk_cols = LANES if moba else 2 * LANES
    in_specs = [
        pl.BlockSpec((1, ATTN_TILE, qk_cols), lambda bi, gi, qi: (bi, qi, gi)),
        pl.BlockSpec((1, s, qk_cols), lambda bi, gi, qi: (bi, 0, gi)),
        pl.BlockSpec((1, s, LANES), lambda bi, gi, qi: (bi, 0, gi)),
    ]
    args = [q, k, v]
    if moba:
        in_specs.append(pl.BlockSpec((1, LANES, LANES), lambda bi, gi, qi: (bi, 0, gi)))
        args.append(kmean)
    return pl.pallas_call(
        functools.partial(_attn_kernel, moba=moba),
        grid=(b, groups, s // ATTN_TILE),
        in_specs=in_specs,
        out_specs=pl.BlockSpec((1, ATTN_TILE, LANES), lambda bi, gi, qi: (bi, qi, gi)),
        out_shape=jax.ShapeDtypeStruct((b, s, groups * LANES), BF16),
        scratch_shapes=[pltpu.VMEM((ATTN_TILE, 1), F32), pltpu.VMEM((ATTN_TILE, 1), F32),
                        pltpu.VMEM((ATTN_TILE, LANES), F32)],
        compiler_params=pltpu.CompilerParams(
            dimension_semantics=("arbitrary", "arbitrary", "arbitrary"),
            vmem_limit_bytes=VMEM_LIMIT),
        name="moba_attention" if moba else "mla_attention",
    )(*args)


def _merge_kernel(x_ref, ya_ref, yb_ref, ga_ref, gb_ref, wa_ref, wb_ref, wo_ref, gffn_ref,
                  wr_ref, br_ref, x1_ref, hslab_ref, route_ref, cnt_ref, run_scr):
    t = TOKEN_TILE

    @pl.when(pl.program_id(0) == 0)
    def _():
        run_scr[...] = jnp.zeros_like(run_scr)

    merged = (ga_ref[...].astype(F32) * jnp.dot(ya_ref[...], wa_ref[...], preferred_element_type=F32)
              + gb_ref[...].astype(F32) * jnp.dot(yb_ref[...], wb_ref[...], preferred_element_type=F32))
    x1 = x_ref[...] + jnp.dot(merged.astype(BF16), wo_ref[...], preferred_element_type=F32)
    x1_ref[...] = x1
    h = _rms(x1, gffn_ref[...])
    for c in range(ROW_SLABS):
        hslab_ref[pl.ds(c, t, stride=ROW_SLABS), :] = h[:, c * LANES:(c + 1) * LANES]

    logits = jnp.dot(h.astype(BF16), wr_ref[...], preferred_element_type=F32) + br_ref[...]
    lane = lax.broadcasted_iota(jnp.int32, (t, LANES), 1)
    lg = logits
    hits, picks = [], []
    top = None
    for r in range(TOP_K):
        gmax = jnp.max(lg, axis=-1, keepdims=True)
        pick = jnp.min(jnp.where(lg == gmax, lane, LANES), axis=-1, keepdims=True)
        hit = lane == pick
        if r == 0:
            top = gmax
        hits.append(hit)
        picks.append(pick)
        lg = jnp.where(hit, -jnp.inf, lg)
    sel = jnp.where(lg == -jnp.inf, 1.0, 0.0)
    wgt = sel * jnp.exp(logits - top)
    wgt = wgt / jnp.sum(wgt, axis=-1, keepdims=True)

    r_i = lax.broadcasted_iota(jnp.int32, (t, t), 0)
    c_i = lax.broadcasted_iota(jnp.int32, (t, t), 1)
    lower = jnp.where(c_i < r_i, 1.0, 0.0).astype(BF16)
    run = run_scr[0:1, :]
    rank = jnp.dot(lower, sel.astype(BF16), preferred_element_type=F32) + run
    run_new = run + jnp.sum(sel, axis=0, keepdims=True)
    run_scr[...] = jnp.broadcast_to(run_new, run_scr.shape)
    cnt_ref[...] = jnp.broadcast_to(run_new, cnt_ref.shape)

    route = jnp.zeros((t, LANES), F32)
    for r in range(TOP_K):
        pos = jnp.sum(jnp.where(hits[r], rank, 0.0), axis=-1, keepdims=True)
        w_r = jnp.sum(jnp.where(hits[r], wgt, 0.0), axis=-1, keepdims=True)
        route = jnp.where(lane == r, picks[r].astype(F32), route)
        route = jnp.where(lane == TOP_K + r, pos, route)
        route = jnp.where(lane == 2 * TOP_K + r, w_r, route)
    route_ref[...] = route


def _dispatch_kernel(pstart_ref, route_ref, h_hbm, xr_init_hbm, xr_hbm, sem):
    del xr_init_hbm
    base = pl.program_id(0) * DISPATCH_TILE

    def row_copy(tok, dest):
        return pltpu.make_async_copy(h_hbm.at[tok], xr_hbm.at[dest], sem)

    def issue(ti, carry):
        for k in range(TOP_K):
            e = route_ref[ti * 2 * TOP_K + k]
            pos = route_ref[ti * 2 * TOP_K + TOP_K + k]
            row_copy(base + ti, pstart_ref[e] + pos).start()
        return carry

    lax.fori_loop(0, DISPATCH_TILE, issue, 0)

    def drain(ti, carry):
        for k in range(TOP_K):
            row_copy(0, 0).wait()
        return carry

    lax.fori_loop(0, DISPATCH_TILE, drain, 0)


def _expert_kernel(blk_e_ref, nact_ref, xr_ref, wgu_ref, bgu_ref, wdn_ref, bdn_ref, yr_ref):
    del blk_e_ref
    r = EXPERT_ROWS
    active = pl.program_id(0) < nact_ref[0]

    @pl.when(active)
    def _():
        x = jnp.concatenate(
            [xr_ref[pl.ds(c, r, stride=ROW_SLABS), :] for c in range(ROW_SLABS)], axis=1)
        gu = jnp.dot(x.astype(BF16), wgu_ref[0], preferred_element_type=F32) + bgu_ref[0]
        g = jnp.minimum(gu[:, :D_EXPERT], SWIGLU_LIMIT)
        u = jnp.clip(gu[:, D_EXPERT:], -SWIGLU_LIMIT, SWIGLU_LIMIT)
        act = (u + 1.0) * (g * jax.nn.sigmoid(SWIGLU_ALPHA * g))
        y = jnp.dot(act.astype(BF16), wdn_ref[0], preferred_element_type=F32) + bdn_ref[0]
        for c in range(ROW_SLABS):
            yr_ref[pl.ds(c, r, stride=ROW_SLABS), :] = y[:, c * LANES:(c + 1) * LANES]

    @pl.when(jnp.logical_not(active))
    def _():
        yr_ref[...] = jnp.zeros_like(yr_ref)


def _final_kernel(pstart_ref, route_i_ref, x1_ref, route_ref, p_ref, gple_ref, wpg_ref, wpp_ref,
                  yr_hbm, o_ref, gbuf, sem):
    t = FINAL_TILE

    def row_copy(dest, k, ti):
        return pltpu.make_async_copy(
            yr_hbm.at[dest], gbuf.at[k, pl.ds(pl.multiple_of(ti * ROW_SLABS, ROW_SLABS), ROW_SLABS)], sem)

    def issue(ti, carry):
        for k in range(TOP_K):
            e = route_i_ref[ti * 2 * TOP_K + k]
            pos = route_i_ref[ti * 2 * TOP_K + TOP_K + k]
            row_copy(pstart_ref[e] + pos, k, ti).start()
        return carry

    lax.fori_loop(0, t, issue, 0)

    def drain(ti, carry):
        for k in range(TOP_K):
            row_copy(0, k, ti).wait()
        return carry

    lax.fori_loop(0, t, drain, 0)

    route = route_ref[...]
    w = [route[:, 2 * TOP_K + k:2 * TOP_K + k + 1] for k in range(TOP_K)]
    chunks = []
    for c in range(ROW_SLABS):
        acc = w[0] * gbuf[0, pl.ds(c, t, stride=ROW_SLABS), :]
        for k in range(1, TOP_K):
            acc = acc + w[k] * gbuf[k, pl.ds(c, t, stride=ROW_SLABS), :]
        chunks.append(acc)
    x2 = x1_ref[...] + jnp.concatenate(chunks, axis=1)
    hp = _rms(x2, gple_ref[...]).astype(BF16)
    gate = jax.nn.sigmoid(jnp.dot(hp, wpg_ref[...], preferred_element_type=F32))
    emb = jnp.dot(p_ref[...].astype(BF16), wpp_ref[...], preferred_element_type=F32)
    o_ref[...] = x2 + gate * emb


def _rope_tables(s, half, lane0, period):
    inv_freq = ROPE_THETA ** (-(jnp.arange(half, dtype=F32) / half))
    ang = jnp.arange(s, dtype=F32)[:, None] * inv_freq[None, :]
    cos, sin = jnp.cos(ang), jnp.sin(ang)
    ones, zeros = jnp.ones((s, 1), F32), jnp.zeros((s, 1), F32)

    def group(lo_fill, a, b):
        pad_lo = jnp.broadcast_to(lo_fill, (s, lane0))
        pad_hi = jnp.broadcast_to(lo_fill, (s, period - lane0 - 2 * half))
        g = jnp.concatenate([pad_lo, a, b, pad_hi], axis=1)
        return jnp.tile(g, (1, LANES // period))

    zero_half = jnp.zeros_like(sin)
    return group(ones, cos, cos), group(zeros, -sin, zero_half), group(zeros, zero_half, sin)


def _pad_heads(w, heads, width):
    k = w.shape[0]
    w = w.reshape(k, heads, width)
    return jnp.pad(w, ((0, 0), (0, 0), (0, LANES - width))).reshape(k, heads * LANES)


def _row(v):
    return v.reshape(1, -1).astype(F32)


def _layer(x, p_i, g_mix, w_in, moba_q_norm, moba_k_norm, mla_q_lat_norm, w_uq, mla_kv_lat_norm,
           w_ukv, mla_q_norm, mla_k_norm, w_branch_a, w_branch_b, w_out, g_ffn, w_router, b_router,
           w_gate_up, b_gate_up, w_down, b_down, g_ple, w_ple_gate, w_ple_proj):
    b, s, d = x.shape
    n = b * s
    assert d == D_MODEL and s % ATTN_TILE == 0 and n % DISPATCH_TILE == 0
    assert s // MOBA_BLOCK <= LANES
    n_tiles = n // TOKEN_TILE
    tiles_per_seq = s // TOKEN_TILE
    xf = x.reshape(n, d)

    off = [0]
    for wdt in (MOBA_WIDTH, MOBA_WIDTH, MOBA_WIDTH, MLA_Q_LORA, MLA_KV_LORA, MLA_ROPE_DIM, D_MODEL, D_MODEL):
        off.append(off[-1] + wdt)
    seg = [w_in[:, off[i]:off[i + 1]] for i in range(8)]
    kpe_cols = jnp.pad(seg[5], ((0, 0), (MLA_NOPE_DIM, LANES - MLA_QK_DIM)))
    w_in_p = jnp.concatenate(seg[:5] + [kpe_cols] + seg[6:], axis=1).astype(BF16)
    assert w_in_p.shape[1] == D_IN_PACKED
    w_uq_p = _pad_heads(w_uq, MLA_HEADS, MLA_QK_DIM).astype(BF16)
    w_ukv_h = w_ukv.reshape(MLA_KV_LORA, MLA_HEADS, MLA_NOPE_DIM + MLA_V_DIM)
    w_uk_p = _pad_heads(w_ukv_h[:, :, :MLA_NOPE_DIM].reshape(MLA_KV_LORA, -1), MLA_HEADS,
                        MLA_NOPE_DIM).astype(BF16)
    w_uv = w_ukv_h[:, :, MLA_NOPE_DIM:].reshape(MLA_KV_LORA, MLA_WIDTH).astype(BF16)
    gqa = _row(jnp.tile(moba_q_norm, 2)) * (MOBA_HEAD_DIM ** -0.5)
    gka = _row(jnp.tile(moba_k_norm, 2))
    gqb = _row(jnp.pad(mla_q_norm, (0, LANES - MLA_QK_DIM))) * (MLA_QK_DIM ** -0.5)
    gkb = _row(jnp.pad(mla_k_norm, (0, LANES - MLA_QK_DIM)))
    cosa, s1a, s2a = _rope_tables(s, MOBA_HEAD_DIM // 2, 0, MOBA_HEAD_DIM)
    cosb, s1b, s2b = _rope_tables(s, MLA_ROPE_DIM // 2, MLA_NOPE_DIM, LANES)

    tok = lambda width: pl.BlockSpec((TOKEN_TILE, width), lambda i: (i, 0))
    whole = lambda arr: pl.BlockSpec(arr.shape, lambda i: (0,) * arr.ndim)
    seq_tab = pl.BlockSpec((TOKEN_TILE, LANES), lambda i: (i % tiles_per_seq, 0))
    params = pltpu.CompilerParams(dimension_semantics=("arbitrary",), vmem_limit_bytes=VMEM_LIMIT)

    consts1 = [_row(g_mix), w_in_p, gqa, gka]
    consts2 = [_row(mla_q_lat_norm), w_uq_p, _row(mla_kv_lat_norm), w_uk_p, w_uv, gqb, gkb]
    qa, ka, va, kmean, qb, kb, vb, ga, gb = pl.pallas_call(
        _inproj_kernel,
        grid=(n_tiles,),
        in_specs=([tok(d)] + [whole(a) for a in consts1] + [seq_tab] * 3
                  + [whole(a) for a in consts2] + [seq_tab] * 3),
        out_specs=[tok(MOBA_WIDTH), tok(MOBA_WIDTH), tok(MOBA_WIDTH),
                   pl.BlockSpec((1, 1, MOBA_WIDTH), lambda i: (i, 0, 0)),
                   tok(MLA_HEADS * LANES), tok(MLA_HEADS * LANES), tok(MLA_WIDTH),
                   tok(d), tok(d)],
        out_shape=[jax.ShapeDtypeStruct((n, MOBA_WIDTH), BF16)] * 3
        + [jax.ShapeDtypeStruct((n_tiles, 1, MOBA_WIDTH), F32)]
        + [jax.ShapeDtypeStruct((n, MLA_HEADS * LANES), BF16)] * 2
        + [jax.ShapeDtypeStruct((n, MLA_WIDTH), BF16)]
        + [jax.ShapeDtypeStruct((n, d), BF16)] * 2,
        compiler_params=params,
        name="in_projection",
    )(xf, *consts1, cosa, s1a, s2a, *consts2, cosb, s1b, s2b)

    kmean = kmean.reshape(b, tiles_per_seq, MOBA_WIDTH)
    kmean = jnp.pad(kmean, ((0, 0), (0, LANES - tiles_per_seq), (0, 0))).astype(BF16)

    r3 = lambda a: a.reshape(b, s, a.shape[-1])
    ya = _attention(r3(qa), r3(ka), r3(va), kmean, moba=True).reshape(n, MOBA_WIDTH)
    yb = _attention(r3(qb), r3(kb), r3(vb), None, moba=False).reshape(n, MLA_WIDTH)

    wr_p = jnp.pad(w_router, ((0, 0), (0, LANES - N_EXPERTS))).astype(BF16)
    br_p = jnp.pad(b_router.astype(F32), (0, LANES - N_EXPERTS), constant_values=NEG).reshape(1, LANES)
    consts3 = [w_branch_a.astype(BF16), w_branch_b.astype(BF16), w_out.astype(BF16), _row(g_ffn),
               wr_p, br_p]
    x1, hslab, route, cnt = pl.pallas_call(
        _merge_kernel,
        grid=(n_tiles,),
        in_specs=[tok(d), tok(MOBA_WIDTH), tok(MLA_WIDTH), tok(d), tok(d)]
        + [whole(a) for a in consts3],
        out_specs=[tok(d), pl.BlockSpec((TOKEN_TILE * ROW_SLABS, LANES), lambda i: (i, 0)),
                   tok(LANES), pl.BlockSpec((SUBLANES, LANES), lambda i: (0, 0))],
        out_shape=[jax.ShapeDtypeStruct((n, d), F32),
                   jax.ShapeDtypeStruct((n * ROW_SLABS, LANES), F32),
                   jax.ShapeDtypeStruct((n, LANES), F32),
                   jax.ShapeDtypeStruct((SUBLANES, LANES), F32)],
        scratch_shapes=[pltpu.VMEM((SUBLANES, LANES), F32)],
        compiler_params=params,
        name="merge_router",
    )(xf, ya, yb, ga, gb, *consts3)

    rb = EXPERT_ROWS
    n_blocks = (n * TOP_K) // rb + N_EXPERTS
    n_rows = n_blocks * rb
    counts = cnt[0, :N_EXPERTS].astype(jnp.int32)
    pcounts = ((counts + rb - 1) // rb) * rb
    pends = jnp.cumsum(pcounts)
    pstarts = (pends - pcounts).astype(jnp.int32)
    nact = (pends[-1] // rb).astype(jnp.int32).reshape(1)
    blk = jnp.minimum(jnp.arange(n_blocks, dtype=jnp.int32), nact[0] - 1)
    blk_e = jnp.sum((pends[None, :] <= (blk * rb)[:, None]).astype(jnp.int32), axis=1)
    blk_e = jnp.minimum(blk_e, N_EXPERTS - 1)
    route_i = route[:, :2 * TOP_K].astype(jnp.int32).reshape(n * 2 * TOP_K)

    xr = pl.pallas_call(
        _dispatch_kernel,
        grid_spec=pltpu.PrefetchScalarGridSpec(
            num_scalar_prefetch=1,
            grid=(n // DISPATCH_TILE,),
            in_specs=[pl.BlockSpec((DISPATCH_TILE * 2 * TOP_K,), lambda i, ps: (i,),
                                   memory_space=pltpu.SMEM),
                      pl.BlockSpec(memory_space=pl.ANY),
                      pl.BlockSpec(memory_space=pl.ANY)],
            out_specs=pl.BlockSpec(memory_space=pl.ANY),
            scratch_shapes=[pltpu.SemaphoreType.DMA(())]),
        out_shape=jax.ShapeDtypeStruct((n_rows, ROW_SLABS, LANES), F32),
        input_output_aliases={3: 0},
        compiler_params=pltpu.CompilerParams(dimension_semantics=("arbitrary",),
                                             has_side_effects=True),
        name="dispatch_rows",
    )(pstarts, route_i, hslab.reshape(n, ROW_SLABS, LANES),
      jnp.zeros((n_rows, ROW_SLABS, LANES), F32))

    act_blk = lambda i, be, na: jnp.minimum(i, na[0] - 1)
    yr = pl.pallas_call(
        _expert_kernel,
        grid_spec=pltpu.PrefetchScalarGridSpec(
            num_scalar_prefetch=2,
            grid=(n_blocks,),
            in_specs=[
                pl.BlockSpec((rb * ROW_SLABS, LANES), lambda i, be, na: (act_blk(i, be, na), 0)),
                pl.BlockSpec((1, d, 2 * D_EXPERT), lambda i, be, na: (be[i], 0, 0)),
                pl.BlockSpec((1, 1, 2 * D_EXPERT), lambda i, be, na: (be[i], 0, 0)),
                pl.BlockSpec((1, D_EXPERT, d), lambda i, be, na: (be[i], 0, 0)),
                pl.BlockSpec((1, 1, d), lambda i, be, na: (be[i], 0, 0)),
            ],
            out_specs=pl.BlockSpec((rb * ROW_SLABS, LANES), lambda i, be, na: (i, 0))),
        out_shape=jax.ShapeDtypeStruct((n_rows * ROW_SLABS, LANES), F32),
        compiler_params=params,
        name="experts",
    )(blk_e, nact, xr.reshape(n_rows * ROW_SLABS, LANES), w_gate_up.astype(BF16),
      b_gate_up.reshape(N_EXPERTS, 1, -1).astype(F32), w_down.astype(BF16),
      b_down.reshape(N_EXPERTS, 1, -1).astype(F32))

    ftok = lambda width: pl.BlockSpec((FINAL_TILE, width), lambda i, ps: (i, 0))
    fwhole = lambda arr: pl.BlockSpec(arr.shape, lambda i, ps: (0,) * arr.ndim)
    consts4 = [_row(g_ple), w_ple_gate.astype(BF16), w_ple_proj.astype(BF16)]
    out = pl.pallas_call(
        _final_kernel,
        grid_spec=pltpu.PrefetchScalarGridSpec(
            num_scalar_prefetch=1,
            grid=(n // FINAL_TILE,),
            in_specs=[pl.BlockSpec((FINAL_TILE * 2 * TOP_K,), lambda i, ps: (i,),
                                   memory_space=pltpu.SMEM),
                      ftok(d), ftok(LANES), ftok(PLE_DIM)]
            + [fwhole(a) for a in consts4] + [pl.BlockSpec(memory_space=pl.ANY)],
            out_specs=ftok(d),
            scratch_shapes=[pltpu.VMEM((TOP_K, FINAL_TILE * ROW_SLABS, LANES), F32),
                            pltpu.SemaphoreType.DMA(())]),
        out_shape=jax.ShapeDtypeStruct((n, d), F32),
        compiler_params=params,
        name="combine_ple",
    )(pstarts, route_i, x1, route, p_i.reshape(n, PLE_DIM), *consts4,
      yr.reshape(n_rows, ROW_SLABS, LANES))
    return out.reshape(b, s, d)


def kernel(x, p, g_mix, w_in, moba_q_norm, moba_k_norm, mla_q_lat_norm, w_uq, mla_kv_lat_norm, w_ukv, mla_q_norm, mla_k_norm, w_branch_a, w_branch_b, w_out, g_ffn, w_router, b_router, w_gate_up, b_gate_up, w_down, b_down, g_ple, w_ple_gate, w_ple_proj):
    for i in range(p.shape[0]):
        x = _layer(x, p[i], g_mix[i], w_in[i], moba_q_norm[i], moba_k_norm[i], mla_q_lat_norm[i],
                   w_uq[i], mla_kv_lat_norm[i], w_ukv[i], mla_q_norm[i], mla_k_norm[i],
                   w_branch_a[i], w_branch_b[i], w_out[i], g_ffn[i], w_router[i], b_router[i],
                   w_gate_up[i], b_gate_up[i], w_down[i], b_down[i], g_ple[i], w_ple_gate[i],
                   w_ple_proj[i])
    return x
```

```python
import functools
import math

import jax
import jax.numpy as jnp
from jax import lax
from jax.experimental import pallas as pl
from jax.experimental.pallas import tpu as pltpu

F32 = jnp.float32
BF16 = jnp.bfloat16

D_MODEL = 1024
PLE_DIM = 256
EPS = 1e-6
ROPE_THETA = 10000.0
MOBA_HEADS = 8
MOBA_HEAD_DIM = 64
MOBA_BLOCK = 256
MOBA_TOPK = 3
MOBA_WIDTH = MOBA_HEADS * MOBA_HEAD_DIM
MLA_HEADS = 8
MLA_Q_LORA = 256
MLA_KV_LORA = 128
MLA_NOPE_DIM = 64
MLA_ROPE_DIM = 32
MLA_V_DIM = 64
MLA_QK_DIM = MLA_NOPE_DIM + MLA_ROPE_DIM
MLA_WIDTH = MLA_HEADS * MLA_V_DIM
N_EXPERTS = 32
TOP_K = 4
D_EXPERT = 1024
SWIGLU_LIMIT = 7.0
SWIGLU_ALPHA = 1.702

LANES = 128
SUBLANES = 8
ROW_SLABS = D_MODEL // LANES
VMEM_LIMIT = 56 * 1024 * 1024

TOKEN_TILE = 256
ATTN_TILE = 256
ATTN_GROUPS = 4
EXPERT_ROWS = 256
DISPATCH_TILE = 512
FINAL_TILE = 128

NEG = -1e30
MASK_BIAS = -1e9

C_QA, C_KA, C_VA = 0, 512, 1024
C_CQ, C_CKV, C_KPE = 1536, 1792, 1920
C_GA, C_GB = 2048, 3072
D_IN_PACKED = 4096


def _rms(x, gain):
    return x * lax.rsqrt(jnp.mean(x * x, axis=-1, keepdims=True) + EPS) * gain


def _rope(t, cos, s1, s2, half):
    return t * cos + pltpu.roll(t, LANES - half, 1) * s1 + pltpu.roll(t, half, 1) * s2


def _inproj_kernel(x_ref, gmix_ref, win_ref, gqa_ref, gka_ref, cosa_ref, s1a_ref, s2a_ref,
                   gql_ref, wuq_ref, gkvl_ref, wuk_ref, wuv_ref, gqb_ref, gkb_ref,
                   cosb_ref, s1b_ref, s2b_ref,
                   qa_ref, ka_ref, vat_ref, kmean_ref, qb_ref, kb_ref, vbt_ref, ga_ref, gb_ref):
    hn = _rms(x_ref[...], gmix_ref[...]).astype(BF16)

    def proj(c0, width):
        return jnp.dot(hn, win_ref[:, c0:c0 + width], preferred_element_type=F32)

    lane = lax.broadcasted_iota(jnp.int32, (TOKEN_TILE, LANES), 1)
    first = lane < MOBA_HEAD_DIM
    cosa, s1a, s2a = cosa_ref[...], s1a_ref[...], s2a_ref[...]

    def moba_norm_rope(t, gain):
        sq = t * t
        ss0 = jnp.sum(jnp.where(first, sq, 0.0), axis=-1, keepdims=True)
        ss1 = jnp.sum(jnp.where(first, 0.0, sq), axis=-1, keepdims=True)
        ms = jnp.where(first, ss0, ss1) * (1.0 / MOBA_HEAD_DIM)
        t = t * lax.rsqrt(ms + EPS) * gain
        return _rope(t, cosa, s1a, s2a, MOBA_HEAD_DIM // 2)

    qa = proj(C_QA, MOBA_WIDTH)
    ka = proj(C_KA, MOBA_WIDTH)
    for c in range(MOBA_WIDTH // LANES):
        sl = slice(c * LANES, (c + 1) * LANES)
        qa_ref[:, sl] = moba_norm_rope(qa[:, sl], gqa_ref[...]).astype(BF16)
        kc = moba_norm_rope(ka[:, sl], gka_ref[...])
        ka_ref[:, sl] = kc.astype(BF16)
        kmean_ref[0, :, sl] = jnp.mean(kc, axis=0, keepdims=True)
    vat_ref[0] = proj(C_VA, MOBA_WIDTH).T.astype(BF16)

    cosb, s1b, s2b = cosb_ref[...], s1b_ref[...], s2b_ref[...]

    def mla_norm_rope(t, gain):
        ms = jnp.sum(t * t, axis=-1, keepdims=True) * (1.0 / MLA_QK_DIM)
        t = t * lax.rsqrt(ms + EPS) * gain
        return _rope(t, cosb, s1b, s2b, MLA_ROPE_DIM // 2)

    cq = _rms(proj(C_CQ, MLA_Q_LORA), gql_ref[...]).astype(BF16)
    qb = jnp.dot(cq, wuq_ref[...], preferred_element_type=F32)
    ckv = _rms(proj(C_CKV, MLA_KV_LORA), gkvl_ref[...]).astype(BF16)
    kn = jnp.dot(ckv, wuk_ref[...], preferred_element_type=F32)
    kpe = proj(C_KPE, LANES)
    for h in range(MLA_HEADS):
        sl = slice(h * LANES, (h + 1) * LANES)
        qb_ref[:, sl] = mla_norm_rope(qb[:, sl], gqb_ref[...]).astype(BF16)
        kb_ref[:, sl] = mla_norm_rope(kn[:, sl] + kpe, gkb_ref[...]).astype(BF16)
    vbt_ref[0] = jnp.dot(ckv, wuv_ref[...], preferred_element_type=F32).T.astype(BF16)

    ga_ref[...] = jax.nn.sigmoid(proj(C_GA, D_MODEL)).astype(BF16)
    gb_ref[...] = jax.nn.sigmoid(proj(C_GB, D_MODEL)).astype(BF16)


_NT = (((1,), (1,)), ((), ()))


GATE_ROWS = 16


def _attn_kernel(*refs, moba):
    if moba:
        q_ref, k_ref, vt_ref, kmean_ref, o_ref = refs
    else:
        q_ref, k_ref, vt_ref, o_ref = refs
    t = ATTN_TILE
    hd = MOBA_HEAD_DIM
    n_heads = 2 * ATTN_GROUPS
    qi = pl.program_id(2)
    key_i = lax.broadcasted_iota(jnp.int32, (t, t), 0)
    qry_i = lax.broadcasted_iota(jnp.int32, (t, t), 1)

    blk = lax.broadcasted_iota(jnp.int32, (GATE_ROWS, t), 0)
    heads, biases = [], []
    for hh in range(n_heads):
        if moba:
            lane = lax.broadcasted_iota(jnp.int32, (t, LANES), 1)
            head_lanes = (lane < hd) if hh % 2 == 0 else (lane >= hd)
            kcols = slice((hh // 2) * LANES, (hh // 2 + 1) * LANES)
            q = jnp.where(head_lanes, q_ref[0, :, kcols], jnp.zeros((), BF16))
            gate = lax.dot_general(kmean_ref[0, :, kcols], q, _NT, preferred_element_type=F32)
            g = jnp.where(blk < qi, gate, -jnp.inf)
            keep = jnp.zeros((GATE_ROWS, t), F32)
            for _ in range(MOBA_TOPK):
                gmax = jnp.max(g, axis=0, keepdims=True)
                pick = jnp.min(jnp.where(g == gmax, blk, GATE_ROWS), axis=0, keepdims=True)
                hit = blk == jnp.where(gmax > -jnp.inf, pick, GATE_ROWS)
                keep = jnp.where(hit, 1.0, keep)
                g = jnp.where(hit, -jnp.inf, g)
            biases.append(jnp.where(keep > 0.0, 0.0, MASK_BIAS))
        else:
            kcols = slice(hh * LANES, (hh + 1) * LANES)
            q = q_ref[0, :, kcols]
        heads.append((q, kcols))

    def update(s, vt_blk, state):
        m_prev, l_prev, acc = state
        m_new = jnp.maximum(m_prev, jnp.max(s, axis=0, keepdims=True))
        alpha = jnp.exp(m_prev - m_new)
        p = jnp.exp(s - m_new)
        l_new = alpha * l_prev + jnp.sum(p, axis=0, keepdims=True)
        acc = alpha * acc + jnp.dot(vt_blk, p.astype(BF16), preferred_element_type=F32)
        return m_new, l_new, acc

    def past_block(j, states):
        start = pl.multiple_of(j * t, t)
        scores = [lax.dot_general(k_ref[0, pl.ds(start, t), kcols], q, _NT,
                                  preferred_element_type=F32) for q, kcols in heads]
        out = []
        for hh, s in enumerate(scores):
            if moba:
                s = jnp.sum(jnp.where(blk == j, biases[hh], 0.0), axis=0, keepdims=True) + s
            out.append(update(s, vt_ref[0, j, hh * hd:(hh + 1) * hd, :], states[hh]))
        return tuple(out)

    init = (jnp.full((1, t), NEG, F32), jnp.zeros((1, t), F32), jnp.zeros((hd, t), F32))
    states = lax.fori_loop(0, qi, past_block, (init,) * n_heads)

    diag_start = pl.multiple_of(qi * t, t)
    scores = [lax.dot_general(k_ref[0, pl.ds(diag_start, t), kcols], q, _NT,
                              preferred_element_type=F32) for q, kcols in heads]
    outs = []
    for hh, s in enumerate(scores):
        s = jnp.where(key_i <= qry_i, s, NEG)
        _, l_fin, acc = update(s, vt_ref[0, qi, hh * hd:(hh + 1) * hd, :], states[hh])
        outs.append(acc / l_fin)
    o_ref[0] = jnp.concatenate(outs, axis=0).T.astype(BF16)


def _attention(q, k, vt, kmean, *, moba):
    b, s, _ = q.shape
    v_cols = ATTN_GROUPS * LANES
    steps = vt.shape[2] // v_cols
    nblk = s // ATTN_TILE
    qk_cols = v_cols if moba else 2 * v_cols
    in_specs = [
        pl.BlockSpec((1, ATTN_TILE, qk_cols), lambda bi, gi, qi: (bi, qi, gi)),
        pl.BlockSpec((1, s, qk_cols), lambda bi, gi, qi: (bi, 0, gi)),
        pl.BlockSpec((1, nblk, v_cols, ATTN_TILE), lambda bi, gi, qi: (bi, 0, gi, 0)),
    ]
    args = [q, k, vt]
    if moba:
        in_specs.append(pl.BlockSpec((1, GATE_ROWS, v_cols), lambda bi, gi, qi: (bi, 0, gi)))
        args.append(kmean)
    return pl.pallas_call(
        functools.partial(_attn_kernel, moba=moba),
        grid=(b, steps, nblk),
        in_specs=in_specs,
        out_specs=pl.BlockSpec((1, ATTN_TILE, v_cols), lambda bi, gi, qi: (bi, qi, gi)),
        out_shape=jax.ShapeDtypeStruct((b, s, steps * v_cols), BF16),
        compiler_params=pltpu.CompilerParams(
            dimension_semantics=("arbitrary", "arbitrary", "arbitrary"),
            vmem_limit_bytes=VMEM_LIMIT),
        name="moba_attention" if moba else "mla_attention",
    )(*args)


def _merge_kernel(x_ref, ya_ref, yb_ref, ga_ref, gb_ref, wa_ref, wb_ref, wo_ref, gffn_ref,
                  wr_ref, br_ref, x1_ref, hslab_ref, route_ref, cnt_ref, run_scr):
    t = TOKEN_TILE

    @pl.when(pl.program_id(0) == 0)
    def _():
        run_scr[...] = jnp.zeros_like(run_scr)

    merged = (ga_ref[...].astype(F32) * jnp.dot(ya_ref[...], wa_ref[...], preferred_element_type=F32)
              + gb_ref[...].astype(F32) * jnp.dot(yb_ref[...], wb_ref[...], preferred_element_type=F32))
    x1 = x_ref[...] + jnp.dot(merged.astype(BF16), wo_ref[...], preferred_element_type=F32)
    x1_ref[...] = x1
    h = _rms(x1, gffn_ref[...])
    for c in range(ROW_SLABS):
        hslab_ref[pl.ds(c, t, stride=ROW_SLABS), :] = h[:, c * LANES:(c + 1) * LANES]

    logits = jnp.dot(h.astype(BF16), wr_ref[...], preferred_element_type=F32) + br_ref[...]
    lane = lax.broadcasted_iota(jnp.int32, (t, LANES), 1)
    lg = logits
    hits, picks = [], []
    top = None
    for r in range(TOP_K):
        gmax = jnp.max(lg, axis=-1, keepdims=True)
        pick = jnp.min(jnp.where(lg == gmax, lane, LANES), axis=-1, keepdims=True)
        hit = lane == pick
        if r == 0:
            top = gmax
        hits.append(hit)
        picks.append(pick)
        lg = jnp.where(hit, -jnp.inf, lg)
    sel = jnp.where(lg == -jnp.inf, 1.0, 0.0)
    wgt = sel * jnp.exp(logits - top)
    wgt = wgt / jnp.sum(wgt, axis=-1, keepdims=True)

    r_i = lax.broadcasted_iota(jnp.int32, (t, t), 0)
    c_i = lax.broadcasted_iota(jnp.int32, (t, t), 1)
    lower = jnp.where(c_i < r_i, 1.0, 0.0).astype(BF16)
    run = run_scr[0:1, :]
    rank = jnp.dot(lower, sel.astype(BF16), preferred_element_type=F32) + run
    run_new = run + jnp.sum(sel, axis=0, keepdims=True)
    run_scr[...] = jnp.broadcast_to(run_new, run_scr.shape)
    cnt_ref[...] = jnp.broadcast_to(run_new, cnt_ref.shape)

    route = jnp.zeros((t, LANES), F32)
    for r in range(TOP_K):
        pos = jnp.sum(jnp.where(hits[r], rank, 0.0), axis=-1, keepdims=True)
        w_r = jnp.sum(jnp.where(hits[r], wgt, 0.0), axis=-1, keepdims=True)
        route = jnp.where(lane == r, picks[r].astype(F32), route)
        route = jnp.where(lane == TOP_K + r, pos, route)
        route = jnp.where(lane == 2 * TOP_K + r, w_r, route)
    route_ref[...] = route


def _dispatch_kernel(pstart_ref, count_ref, nact_ref, route_ref, h_ref, xr_hbm, zero_scr, sem, zsem):
    def row_copy(ti, dest):
        src = h_ref.at[pl.ds(pl.multiple_of(ti * ROW_SLABS, ROW_SLABS), ROW_SLABS)]
        return pltpu.make_async_copy(src, xr_hbm.at[dest], sem)

    def issue(ti, carry):
        for k in range(TOP_K):
            e = route_ref[ti * 2 * TOP_K + k]
            pos = route_ref[ti * 2 * TOP_K + TOP_K + k]
            row_copy(ti, pstart_ref[e] + pos).start()
        return carry

    lax.fori_loop(0, DISPATCH_TILE, issue, 0)

    @pl.when(pl.program_id(0) == 0)
    def _():
        zero_scr[...] = jnp.zeros_like(zero_scr)

        def pad_rows(e):
            first = pstart_ref[e] + count_ref[e]
            n_pad = (-count_ref[e]) & (EXPERT_ROWS - 1)
            return first, n_pad

        def zero_copy(dest):
            return pltpu.make_async_copy(zero_scr.at[0], xr_hbm.at[dest], zsem)

        def issue_e(e, carry):
            first, n_pad = pad_rows(e)
            lax.fori_loop(0, n_pad, lambda r, c: (zero_copy(first + r).start(), c)[1], 0)
            return carry

        def drain_e(e, carry):
            _, n_pad = pad_rows(e)
            lax.fori_loop(0, n_pad, lambda r, c: (zero_copy(0).wait(), c)[1], 0)
            return carry

        lax.fori_loop(0, N_EXPERTS, issue_e, 0)
        lax.fori_loop(0, N_EXPERTS, drain_e, 0)

        n_blocks = xr_hbm.shape[0] // EXPERT_ROWS

        def block_copy(blk):
            dst = xr_hbm.at[pl.ds(pl.multiple_of(blk * EXPERT_ROWS, EXPERT_ROWS), EXPERT_ROWS)]
            return pltpu.make_async_copy(zero_scr, dst, zsem)

        lax.fori_loop(nact_ref[0], n_blocks, lambda blk, c: (block_copy(blk).start(), c)[1], 0)
        lax.fori_loop(nact_ref[0], n_blocks, lambda blk, c: (block_copy(0).wait(), c)[1], 0)

    def drain(ti, carry):
        for k in range(TOP_K):
            row_copy(0, 0).wait()
        return carry

    lax.fori_loop(0, DISPATCH_TILE, drain, 0)


def _expert_kernel(blk_e_ref, nact_ref, xr_ref, wgu_ref, bgu_ref, wdn_ref, bdn_ref, yr_ref):
    del blk_e_ref
    r = EXPERT_ROWS
    active = pl.program_id(0) < nact_ref[0]

    @pl.when(active)
    def _():
        x = jnp.concatenate(
            [xr_ref[pl.ds(c, r, stride=ROW_SLABS), :] for c in range(ROW_SLABS)], axis=1)
        gu = jnp.dot(x.astype(BF16), wgu_ref[0], preferred_element_type=F32) + bgu_ref[0]
        g = jnp.minimum(gu[:, :D_EXPERT], SWIGLU_LIMIT)
        u = jnp.clip(gu[:, D_EXPERT:], -SWIGLU_LIMIT, SWIGLU_LIMIT)
        act = (u + 1.0) * (g * jax.nn.sigmoid(SWIGLU_ALPHA * g))
        y = jnp.dot(act.astype(BF16), wdn_ref[0], preferred_element_type=F32) + bdn_ref[0]
        for c in range(ROW_SLABS):
            yr_ref[pl.ds(c, r, stride=ROW_SLABS), :] = y[:, c * LANES:(c + 1) * LANES]

    @pl.when(jnp.logical_not(active))
    def _():
        yr_ref[...] = jnp.zeros_like(yr_ref)


def _final_kernel(pstart_ref, route_i_ref, x1_ref, route_ref, p_ref, gple_ref, wpg_ref, wpp_ref,
                  yr_hbm, o_ref, gbuf, sem):
    t = FINAL_TILE

    def row_copy(dest, k, ti):
        return pltpu.make_async_copy(
            yr_hbm.at[dest], gbuf.at[k, pl.ds(pl.multiple_of(ti * ROW_SLABS, ROW_SLABS), ROW_SLABS)], sem)

    def issue(ti, carry):
        for k in range(TOP_K):
            e = route_i_ref[ti * 2 * TOP_K + k]
            pos = route_i_ref[ti * 2 * TOP_K + TOP_K + k]
            row_copy(pstart_ref[e] + pos, k, ti).start()
        return carry

    lax.fori_loop(0, t, issue, 0)

    def drain(ti, carry):
        for k in range(TOP_K):
            row_copy(0, k, ti).wait()
        return carry

    lax.fori_loop(0, t, drain, 0)

    route = route_ref[...]
    w = [route[:, 2 * TOP_K + k:2 * TOP_K + k + 1] for k in range(TOP_K)]
    chunks = []
    for c in range(ROW_SLABS):
        acc = w[0] * gbuf[0, pl.ds(c, t, stride=ROW_SLABS), :]
        for k in range(1, TOP_K):
            acc = acc + w[k] * gbuf[k, pl.ds(c, t, stride=ROW_SLABS), :]
        chunks.append(acc)
    x2 = x1_ref[...] + jnp.concatenate(chunks, axis=1)
    hp = _rms(x2, gple_ref[...]).astype(BF16)
    gate = jax.nn.sigmoid(jnp.dot(hp, wpg_ref[...], preferred_element_type=F32))
    emb = jnp.dot(p_ref[...].astype(BF16), wpp_ref[...], preferred_element_type=F32)
    o_ref[...] = x2 + gate * emb


def _rope_tables(s, half, lane0, period):
    inv_freq = ROPE_THETA ** (-(jnp.arange(half, dtype=F32) / half))
    ang = jnp.arange(s, dtype=F32)[:, None] * inv_freq[None, :]
    cos, sin = jnp.cos(ang), jnp.sin(ang)
    ones, zeros = jnp.ones((s, 1), F32), jnp.zeros((s, 1), F32)

    def group(lo_fill, a, b):
        pad_lo = jnp.broadcast_to(lo_fill, (s, lane0))
        pad_hi = jnp.broadcast_to(lo_fill, (s, period - lane0 - 2 * half))
        g = jnp.concatenate([pad_lo, a, b, pad_hi], axis=1)
        return jnp.tile(g, (1, LANES // period))

    zero_half = jnp.zeros_like(sin)
    return group(ones, cos, cos), group(zeros, -sin, zero_half), group(zeros, zero_half, sin)


def _pad_heads(w, heads, width):
    k = w.shape[0]
    w = w.reshape(k, heads, width)
    return jnp.pad(w, ((0, 0), (0, 0), (0, LANES - width))).reshape(k, heads * LANES)


def _row(v):
    return v.reshape(1, -1).astype(F32)


def _layer(x, p_i, g_mix, w_in, moba_q_norm, moba_k_norm, mla_q_lat_norm, w_uq, mla_kv_lat_norm,
           w_ukv, mla_q_norm, mla_k_norm, w_branch_a, w_branch_b, w_out, g_ffn, w_router, b_router,
           w_gate_up, b_gate_up, w_down, b_down, g_ple, w_ple_gate, w_ple_proj):
    b, s, d = x.shape
    n = b * s
    assert d == D_MODEL and s % ATTN_TILE == 0 and n % DISPATCH_TILE == 0
    assert s // MOBA_BLOCK <= GATE_ROWS and TOKEN_TILE == ATTN_TILE == MOBA_BLOCK
    n_tiles = n // TOKEN_TILE
    tiles_per_seq = s // TOKEN_TILE
    xf = x.reshape(n, d)

    off = [0]
    for wdt in (MOBA_WIDTH, MOBA_WIDTH, MOBA_WIDTH, MLA_Q_LORA, MLA_KV_LORA, MLA_ROPE_DIM, D_MODEL, D_MODEL):
        off.append(off[-1] + wdt)
    seg = [w_in[:, off[i]:off[i + 1]] for i in range(8)]
    kpe_cols = jnp.pad(seg[5], ((0, 0), (MLA_NOPE_DIM, LANES - MLA_QK_DIM)))
    w_in_p = jnp.concatenate(seg[:5] + [kpe_cols] + seg[6:], axis=1).astype(BF16)
    assert w_in_p.shape[1] == D_IN_PACKED
    w_uq_p = _pad_heads(w_uq, MLA_HEADS, MLA_QK_DIM).astype(BF16)
    w_ukv_h = w_ukv.reshape(MLA_KV_LORA, MLA_HEADS, MLA_NOPE_DIM + MLA_V_DIM)
    w_uk_p = _pad_heads(w_ukv_h[:, :, :MLA_NOPE_DIM].reshape(MLA_KV_LORA, -1), MLA_HEADS,
                        MLA_NOPE_DIM).astype(BF16)
    w_uv = w_ukv_h[:, :, MLA_NOPE_DIM:].reshape(MLA_KV_LORA, MLA_WIDTH).astype(BF16)
    gqa = _row(jnp.tile(moba_q_norm, 2)) * (MOBA_HEAD_DIM ** -0.5)
    gka = _row(jnp.tile(moba_k_norm, 2))
    gqb = _row(jnp.pad(mla_q_norm, (0, LANES - MLA_QK_DIM))) * (MLA_QK_DIM ** -0.5)
    gkb = _row(jnp.pad(mla_k_norm, (0, LANES - MLA_QK_DIM)))
    cosa, s1a, s2a = _rope_tables(s, MOBA_HEAD_DIM // 2, 0, MOBA_HEAD_DIM)
    cosb, s1b, s2b = _rope_tables(s, MLA_ROPE_DIM // 2, MLA_NOPE_DIM, LANES)

    tok = lambda width: pl.BlockSpec((TOKEN_TILE, width), lambda i: (i, 0))
    whole = lambda arr: pl.BlockSpec(arr.shape, lambda i: (0,) * arr.ndim)
    seq_tab = pl.BlockSpec((TOKEN_TILE, LANES), lambda i: (i % tiles_per_seq, 0))
    vt_spec = pl.BlockSpec((1, MOBA_WIDTH, TOKEN_TILE), lambda i: (i, 0, 0))
    params = pltpu.CompilerParams(dimension_semantics=("arbitrary",), vmem_limit_bytes=VMEM_LIMIT)

    consts1 = [_row(g_mix), w_in_p, gqa, gka]
    consts2 = [_row(mla_q_lat_norm), w_uq_p, _row(mla_kv_lat_norm), w_uk_p, w_uv, gqb, gkb]
    qa, ka, va, kmean, qb, kb, vb, ga, gb = pl.pallas_call(
        _inproj_kernel,
        grid=(n_tiles,),
        in_specs=([tok(d)] + [whole(a) for a in consts1] + [seq_tab] * 3
                  + [whole(a) for a in consts2] + [seq_tab] * 3),
        out_specs=[tok(MOBA_WIDTH), tok(MOBA_WIDTH), vt_spec,
                   pl.BlockSpec((1, 1, MOBA_WIDTH), lambda i: (i, 0, 0)),
                   tok(MLA_HEADS * LANES), tok(MLA_HEADS * LANES), vt_spec,
                   tok(d), tok(d)],
        out_shape=[jax.ShapeDtypeStruct((n, MOBA_WIDTH), BF16)] * 2
        + [jax.ShapeDtypeStruct((n_tiles, MOBA_WIDTH, TOKEN_TILE), BF16)]
        + [jax.ShapeDtypeStruct((n_tiles, 1, MOBA_WIDTH), F32)]
        + [jax.ShapeDtypeStruct((n, MLA_HEADS * LANES), BF16)] * 2
        + [jax.ShapeDtypeStruct((n_tiles, MLA_WIDTH, TOKEN_TILE), BF16)]
        + [jax.ShapeDtypeStruct((n, d), BF16)] * 2,
        compiler_params=params,
        name="in_projection",
    )(xf, *consts1, cosa, s1a, s2a, *consts2, cosb, s1b, s2b)

    kmean = kmean.reshape(b, tiles_per_seq, MOBA_WIDTH)
    kmean = jnp.pad(kmean, ((0, 0), (0, GATE_ROWS - tiles_per_seq), (0, 0))).astype(BF16)

    r3 = lambda a: a.reshape(b, s, a.shape[-1])
    r4 = lambda a: a.reshape(b, tiles_per_seq, a.shape[1], TOKEN_TILE)
    ya = _attention(r3(qa), r3(ka), r4(va), kmean, moba=True).reshape(n, MOBA_WIDTH)
    yb = _attention(r3(qb), r3(kb), r4(vb), None, moba=False).reshape(n, MLA_WIDTH)

    wr_p = jnp.pad(w_router, ((0, 0), (0, LANES - N_EXPERTS))).astype(BF16)
    br_p = jnp.pad(b_router.astype(F32), (0, LANES - N_EXPERTS), constant_values=NEG).reshape(1, LANES)
    consts3 = [w_branch_a.astype(BF16), w_branch_b.astype(BF16), w_out.astype(BF16), _row(g_ffn),
               wr_p, br_p]
    x1, hslab, route, cnt = pl.pallas_call(
        _merge_kernel,
        grid=(n_tiles,),
        in_specs=[tok(d), tok(MOBA_WIDTH), tok(MLA_WIDTH), tok(d), tok(d)]
        + [whole(a) for a in consts3],
        out_specs=[tok(d), pl.BlockSpec((TOKEN_TILE * ROW_SLABS, LANES), lambda i: (i, 0)),
                   tok(LANES), pl.BlockSpec((SUBLANES, LANES), lambda i: (0, 0))],
        out_shape=[jax.ShapeDtypeStruct((n, d), F32),
                   jax.ShapeDtypeStruct((n * ROW_SLABS, LANES), F32),
                   jax.ShapeDtypeStruct((n, LANES), F32),
                   jax.ShapeDtypeStruct((SUBLANES, LANES), F32)],
        scratch_shapes=[pltpu.VMEM((SUBLANES, LANES), F32)],
        compiler_params=params,
        name="merge_router",
    )(xf, ya, yb, ga, gb, *consts3)

    rb = EXPERT_ROWS
    n_blocks = (n * TOP_K) // rb + N_EXPERTS
    n_rows = n_blocks * rb
    counts = cnt[0, :N_EXPERTS].astype(jnp.int32)
    pcounts = ((counts + rb - 1) // rb) * rb
    pends = jnp.cumsum(pcounts)
    pstarts = (pends - pcounts).astype(jnp.int32)
    nact = (pends[-1] // rb).astype(jnp.int32).reshape(1)
    blk = jnp.minimum(jnp.arange(n_blocks, dtype=jnp.int32), nact[0] - 1)
    blk_e = jnp.sum((pends[None, :] <= (blk * rb)[:, None]).astype(jnp.int32), axis=1)
    blk_e = jnp.minimum(blk_e, N_EXPERTS - 1)
    route_i = route[:, :2 * TOP_K].astype(jnp.int32).reshape(n * 2 * TOP_K)

    xr = pl.pallas_call(
        _dispatch_kernel,
        grid_spec=pltpu.PrefetchScalarGridSpec(
            num_scalar_prefetch=3,
            grid=(n // DISPATCH_TILE,),
            in_specs=[pl.BlockSpec((DISPATCH_TILE * 2 * TOP_K,), lambda i, ps, ct, na: (i,),
                                   memory_space=pltpu.SMEM),
                      pl.BlockSpec((DISPATCH_TILE * ROW_SLABS, LANES),
                                   lambda i, ps, ct, na: (i, 0))],
            out_specs=pl.BlockSpec(memory_space=pl.ANY),
            scratch_shapes=[pltpu.VMEM((EXPERT_ROWS, ROW_SLABS, LANES), F32),
                            pltpu.SemaphoreType.DMA(()), pltpu.SemaphoreType.DMA(())]),
        out_shape=jax.ShapeDtypeStruct((n_rows, ROW_SLABS, LANES), F32),
        compiler_params=params,
        name="dispatch_rows",
    )(pstarts, counts, nact, route_i, hslab)

    act_blk = lambda i, be, na: jnp.minimum(i, na[0] - 1)
    yr = pl.pallas_call(
        _expert_kernel,
        grid_spec=pltpu.PrefetchScalarGridSpec(
            num_scalar_prefetch=2,
            grid=(n_blocks,),
            in_specs=[
                pl.BlockSpec((rb * ROW_SLABS, LANES), lambda i, be, na: (act_blk(i, be, na), 0)),
                pl.BlockSpec((1, d, 2 * D_EXPERT), lambda i, be, na: (be[i], 0, 0)),
                pl.BlockSpec((1, 1, 2 * D_EXPERT), lambda i, be, na: (be[i], 0, 0)),
                pl.BlockSpec((1, D_EXPERT, d), lambda i, be, na: (be[i], 0, 0)),
                pl.BlockSpec((1, 1, d), lambda i, be, na: (be[i], 0, 0)),
            ],
            out_specs=pl.BlockSpec((rb * ROW_SLABS, LANES), lambda i, be, na: (i, 0))),
        out_shape=jax.ShapeDtypeStruct((n_rows * ROW_SLABS, LANES), F32),
        compiler_params=params,
        name="experts",
    )(blk_e, nact, xr.reshape(n_rows * ROW_SLABS, LANES), w_gate_up.astype(BF16),
      b_gate_up.reshape(N_EXPERTS, 1, -1).astype(F32), w_down.astype(BF16),
      b_down.reshape(N_EXPERTS, 1, -1).astype(F32))

    ftok = lambda width: pl.BlockSpec((FINAL_TILE, width), lambda i, ps: (i, 0))
    fwhole = lambda arr: pl.BlockSpec(arr.shape, lambda i, ps: (0,) * arr.ndim)
    consts4 = [_row(g_ple), w_ple_gate.astype(BF16), w_ple_proj.astype(BF16)]
    out = pl.pallas_call(
        _final_kernel,
        grid_spec=pltpu.PrefetchScalarGridSpec(
            num_scalar_prefetch=1,
            grid=(n // FINAL_TILE,),
            in_specs=[pl.BlockSpec((FINAL_TILE * 2 * TOP_K,), lambda i, ps: (i,),
                                   memory_space=pltpu.SMEM),
                      ftok(d), ftok(LANES), ftok(PLE_DIM)]
            + [fwhole(a) for a in consts4] + [pl.BlockSpec(memory_space=pl.ANY)],
            out_specs=ftok(d),
            scratch_shapes=[pltpu.VMEM((TOP_K, FINAL_TILE * ROW_SLABS, LANES), F32),
                            pltpu.SemaphoreType.DMA(())]),
        out_shape=jax.ShapeDtypeStruct((n, d), F32),
        compiler_params=params,
        name="combine_ple",
    )(pstarts, route_i, x1, route, p_i.reshape(n, PLE_DIM), *consts4,
      yr.reshape(n_rows, ROW_SLABS, LANES))
    return out.reshape(b, s, d)


def kernel(x, p, g_mix, w_in, moba_q_norm, moba_k_norm, mla_q_lat_norm, w_uq, mla_kv_lat_norm, w_ukv, mla_q_norm, mla_k_norm, w_branch_a, w_branch_b, w_out, g_ffn, w_router, b_router, w_gate_up, b_gate_up, w_down, b_down, g_ple, w_ple_gate, w_ple_proj):
    for i in range(p.shape[0]):
        x = _layer(x, p[i], g_mix[i], w_in[i], moba_q_norm[i], moba_k_norm[i], mla_q_lat_norm[i],
                   w_uq[i], mla_kv_lat_norm[i], w_ukv[i], mla_q_norm[i], mla_k_norm[i],
                   w_branch_a[i], w_branch_b[i], w_out[i], g_ffn[i], w_router[i], b_router[i],
                   w_gate_up[i], b_gate_up[i], w_down[i], b_down[i], g_ple[i], w_ple_gate[i],
                   w_ple_proj[i])
    return x
```

```python
import functools
import math

import jax
import jax.numpy as jnp
from jax import lax
from jax.experimental import pallas as pl
from jax.experimental.pallas import tpu as pltpu

F32 = jnp.float32
BF16 = jnp.bfloat16

D_MODEL = 1024
PLE_DIM = 256
EPS = 1e-6
ROPE_THETA = 10000.0
MOBA_HEADS = 8
MOBA_HEAD_DIM = 64
MOBA_BLOCK = 256
MOBA_TOPK = 3
MOBA_WIDTH = MOBA_HEADS * MOBA_HEAD_DIM
MLA_HEADS = 8
MLA_Q_LORA = 256
MLA_KV_LORA = 128
MLA_NOPE_DIM = 64
MLA_ROPE_DIM = 32
MLA_V_DIM = 64
MLA_QK_DIM = MLA_NOPE_DIM + MLA_ROPE_DIM
MLA_WIDTH = MLA_HEADS * MLA_V_DIM
N_EXPERTS = 32
TOP_K = 4
D_EXPERT = 1024
SWIGLU_LIMIT = 7.0
SWIGLU_ALPHA = 1.702

LANES = 128
SUBLANES = 8
ROW_SLABS = D_MODEL // LANES
VMEM_LIMIT = 56 * 1024 * 1024

TOKEN_TILE = 256
ATTN_TILE = 256
ATTN_GROUPS = 4
EXPERT_ROWS = 256
DISPATCH_TILE = 512
FINAL_TILE = 128

NEG = -1e30
MASK_BIAS = -1e9

C_QA, C_KA, C_VA = 0, 512, 1024
C_CQ, C_CKV, C_KPE = 1536, 1792, 1920
C_GA, C_GB = 2048, 3072
D_IN_PACKED = 4096


def _rms(x, gain):
    return x * lax.rsqrt(jnp.mean(x * x, axis=-1, keepdims=True) + EPS) * gain


def _rope(t, cos, sin):
    return t * cos + pltpu.roll(t, LANES // 2, 1) * sin


def _moba_even_head(lane):
    return (lane & (MOBA_HEAD_DIM // 2)) == 0


def _inproj_kernel(x_ref, gmix_ref, win_ref, gqa_ref, gka_ref, cosa_ref, sina_ref,
                   gql_ref, wuq_ref, gkvl_ref, wuk_ref, wuv_ref, gqb_ref, gkb_ref,
                   cosb_ref, sinb_ref,
                   qa_ref, ka_ref, vat_ref, kmean_ref, qb_ref, kb_ref, vbt_ref, ga_ref, gb_ref):
    hn = _rms(x_ref[...], gmix_ref[...]).astype(BF16)

    def proj(c0, width):
        return jnp.dot(hn, win_ref[:, c0:c0 + width], preferred_element_type=F32)

    first = _moba_even_head(lax.broadcasted_iota(jnp.int32, (TOKEN_TILE, LANES), 1))
    cosa, sina = cosa_ref[...], sina_ref[...]

    def moba_norm_rope(t, gain):
        sq = t * t
        ss0 = jnp.sum(jnp.where(first, sq, 0.0), axis=-1, keepdims=True)
        ss1 = jnp.sum(jnp.where(first, 0.0, sq), axis=-1, keepdims=True)
        ms = jnp.where(first, ss0, ss1) * (1.0 / MOBA_HEAD_DIM)
        t = t * lax.rsqrt(ms + EPS) * gain
        return _rope(t, cosa, sina)

    qa = proj(C_QA, MOBA_WIDTH)
    ka = proj(C_KA, MOBA_WIDTH)
    for c in range(MOBA_WIDTH // LANES):
        sl = slice(c * LANES, (c + 1) * LANES)
        qa_ref[:, sl] = moba_norm_rope(qa[:, sl], gqa_ref[...]).astype(BF16)
        kc = moba_norm_rope(ka[:, sl], gka_ref[...])
        ka_ref[:, sl] = kc.astype(BF16)
        kmean_ref[0, :, sl] = jnp.mean(kc, axis=0, keepdims=True)
    vat_ref[0] = proj(C_VA, MOBA_WIDTH).T.astype(BF16)

    cosb, sinb = cosb_ref[...], sinb_ref[...]

    def mla_norm_rope(t, gain):
        ms = jnp.sum(t * t, axis=-1, keepdims=True) * (1.0 / MLA_QK_DIM)
        t = t * lax.rsqrt(ms + EPS) * gain
        return _rope(t, cosb, sinb)

    cq = _rms(proj(C_CQ, MLA_Q_LORA), gql_ref[...]).astype(BF16)
    qb = jnp.dot(cq, wuq_ref[...], preferred_element_type=F32)
    ckv = _rms(proj(C_CKV, MLA_KV_LORA), gkvl_ref[...]).astype(BF16)
    kn = jnp.dot(ckv, wuk_ref[...], preferred_element_type=F32)
    kpe = proj(C_KPE, LANES)
    for h in range(MLA_HEADS):
        sl = slice(h * LANES, (h + 1) * LANES)
        qb_ref[:, sl] = mla_norm_rope(qb[:, sl], gqb_ref[...]).astype(BF16)
        kb_ref[:, sl] = mla_norm_rope(kn[:, sl] + kpe, gkb_ref[...]).astype(BF16)
    vbt_ref[0] = jnp.dot(ckv, wuv_ref[...], preferred_element_type=F32).T.astype(BF16)

    ga_ref[...] = jax.nn.sigmoid(proj(C_GA, D_MODEL)).astype(BF16)
    gb_ref[...] = jax.nn.sigmoid(proj(C_GB, D_MODEL)).astype(BF16)


_NT = (((1,), (1,)), ((), ()))


GATE_ROWS = 16


def _attn_kernel(*refs, moba):
    if moba:
        q_ref, k_ref, vt_ref, kmean_ref, o_ref = refs
    else:
        q_ref, k_ref, vt_ref, o_ref = refs
    t = ATTN_TILE
    hd = MOBA_HEAD_DIM
    n_heads = 2 * ATTN_GROUPS
    qi = pl.program_id(2)
    key_i = lax.broadcasted_iota(jnp.int32, (t, t), 0)
    qry_i = lax.broadcasted_iota(jnp.int32, (t, t), 1)

    blk = lax.broadcasted_iota(jnp.int32, (GATE_ROWS, t), 0)
    heads, biases = [], []
    for hh in range(n_heads):
        if moba:
            lane = lax.broadcasted_iota(jnp.int32, (t, LANES), 1)
            even = _moba_even_head(lane)
            head_lanes = even if hh % 2 == 0 else jnp.logical_not(even)
            kcols = slice((hh // 2) * LANES, (hh // 2 + 1) * LANES)
            q = jnp.where(head_lanes, q_ref[0, :, kcols], jnp.zeros((), BF16))
            gate = lax.dot_general(kmean_ref[0, :, kcols], q, _NT, preferred_element_type=F32)
            g = jnp.where(blk < qi, gate, -jnp.inf)
            keep = jnp.zeros((GATE_ROWS, t), F32)
            for _ in range(MOBA_TOPK):
                gmax = jnp.max(g, axis=0, keepdims=True)
                pick = jnp.min(jnp.where(g == gmax, blk, GATE_ROWS), axis=0, keepdims=True)
                hit = blk == jnp.where(gmax > -jnp.inf, pick, GATE_ROWS)
                keep = jnp.where(hit, 1.0, keep)
                g = jnp.where(hit, -jnp.inf, g)
            biases.append(jnp.where(keep > 0.0, 0.0, MASK_BIAS))
        else:
            kcols = slice(hh * LANES, (hh + 1) * LANES)
            q = q_ref[0, :, kcols]
        heads.append((q, kcols))

    def update(s, vt_blk, state):
        m_prev, l_prev, acc = state
        m_new = jnp.maximum(m_prev, jnp.max(s, axis=0, keepdims=True))
        alpha = jnp.exp(m_prev - m_new)
        p = jnp.exp(s - m_new)
        l_new = alpha * l_prev + jnp.sum(p, axis=0, keepdims=True)
        acc = alpha * acc + jnp.dot(vt_blk, p.astype(BF16), preferred_element_type=F32)
        return m_new, l_new, acc

    def past_block(j, states):
        start = pl.multiple_of(j * t, t)
        scores = [lax.dot_general(k_ref[0, pl.ds(start, t), kcols], q, _NT,
                                  preferred_element_type=F32) for q, kcols in heads]
        out = []
        for hh, s in enumerate(scores):
            if moba:
                s = jnp.sum(jnp.where(blk == j, biases[hh], 0.0), axis=0, keepdims=True) + s
            out.append(update(s, vt_ref[0, j, hh * hd:(hh + 1) * hd, :], states[hh]))
        return tuple(out)

    init = (jnp.full((1, t), NEG, F32), jnp.zeros((1, t), F32), jnp.zeros((hd, t), F32))
    states = lax.fori_loop(0, qi, past_block, (init,) * n_heads)

    diag_start = pl.multiple_of(qi * t, t)
    scores = [lax.dot_general(k_ref[0, pl.ds(diag_start, t), kcols], q, _NT,
                              preferred_element_type=F32) for q, kcols in heads]
    outs = []
    for hh, s in enumerate(scores):
        s = jnp.where(key_i <= qry_i, s, NEG)
        _, l_fin, acc = update(s, vt_ref[0, qi, hh * hd:(hh + 1) * hd, :], states[hh])
        outs.append(acc / l_fin)
    o_ref[0] = jnp.concatenate(outs, axis=0).T.astype(BF16)


def _attention(q, k, vt, kmean, *, moba):
    b, s, _ = q.shape
    v_cols = ATTN_GROUPS * LANES
    steps = vt.shape[2] // v_cols
    nblk = s // ATTN_TILE
    qk_cols = v_cols if moba else 2 * v_cols
    in_specs = [
        pl.BlockSpec((1, ATTN_TILE, qk_cols), lambda bi, gi, qi: (bi, qi, gi)),
        pl.BlockSpec((1, s, qk_cols), lambda bi, gi, qi: (bi, 0, gi)),
        pl.BlockSpec((1, nblk, v_cols, ATTN_TILE), lambda bi, gi, qi: (bi, 0, gi, 0)),
    ]
    args = [q, k, vt]
    if moba:
        in_specs.append(pl.BlockSpec((1, GATE_ROWS, v_cols), lambda bi, gi, qi: (bi, 0, gi)))
        args.append(kmean)
    return pl.pallas_call(
        functools.partial(_attn_kernel, moba=moba),
        grid=(b, steps, nblk),
        in_specs=in_specs,
        out_specs=pl.BlockSpec((1, ATTN_TILE, v_cols), lambda bi, gi, qi: (bi, qi, gi)),
        out_shape=jax.ShapeDtypeStruct((b, s, steps * v_cols), BF16),
        compiler_params=pltpu.CompilerParams(
            dimension_semantics=("arbitrary", "arbitrary", "arbitrary"),
            vmem_limit_bytes=VMEM_LIMIT),
        name="moba_attention" if moba else "mla_attention",
    )(*args)


def _merge_kernel(x_ref, ya_ref, yb_ref, ga_ref, gb_ref, wa_ref, wb_ref, wo_ref, gffn_ref,
                  wr_ref, br_ref, x1_ref, hslab_ref, route_ref, cnt_ref, run_scr):
    t = TOKEN_TILE

    @pl.when(pl.program_id(0) == 0)
    def _():
        run_scr[...] = jnp.zeros_like(run_scr)

    merged = (ga_ref[...].astype(F32) * jnp.dot(ya_ref[...], wa_ref[...], preferred_element_type=F32)
              + gb_ref[...].astype(F32) * jnp.dot(yb_ref[...], wb_ref[...], preferred_element_type=F32))
    x1 = x_ref[...] + jnp.dot(merged.astype(BF16), wo_ref[...], preferred_element_type=F32)
    x1_ref[...] = x1
    h = _rms(x1, gffn_ref[...])
    for c in range(ROW_SLABS):
        hslab_ref[pl.ds(c, t, stride=ROW_SLABS), :] = h[:, c * LANES:(c + 1) * LANES]

    logits = jnp.dot(h.astype(BF16), wr_ref[...], preferred_element_type=F32) + br_ref[...]
    lane = lax.broadcasted_iota(jnp.int32, (t, LANES), 1)
    lg = logits
    hits, picks = [], []
    top = None
    for r in range(TOP_K):
        gmax = jnp.max(lg, axis=-1, keepdims=True)
        pick = jnp.min(jnp.where(lg == gmax, lane, LANES), axis=-1, keepdims=True)
        hit = lane == pick
        if r == 0:
            top = gmax
        hits.append(hit)
        picks.append(pick)
        lg = jnp.where(hit, -jnp.inf, lg)
    sel = jnp.where(lg == -jnp.inf, 1.0, 0.0)
    wgt = sel * jnp.exp(logits - top)
    wgt = wgt / jnp.sum(wgt, axis=-1, keepdims=True)

    r_i = lax.broadcasted_iota(jnp.int32, (t, t), 0)
    c_i = lax.broadcasted_iota(jnp.int32, (t, t), 1)
    lower = jnp.where(c_i < r_i, 1.0, 0.0).astype(BF16)
    run = run_scr[0:1, :]
    rank = jnp.dot(lower, sel.astype(BF16), preferred_element_type=F32) + run
    run_new = run + jnp.sum(sel, axis=0, keepdims=True)
    run_scr[...] = jnp.broadcast_to(run_new, run_scr.shape)
    cnt_ref[...] = jnp.broadcast_to(run_new, cnt_ref.shape)

    route = jnp.zeros((t, LANES), F32)
    for r in range(TOP_K):
        pos = jnp.sum(jnp.where(hits[r], rank, 0.0), axis=-1, keepdims=True)
        w_r = jnp.sum(jnp.where(hits[r], wgt, 0.0), axis=-1, keepdims=True)
        route = jnp.where(lane == r, picks[r].astype(F32), route)
        route = jnp.where(lane == TOP_K + r, pos, route)
        route = jnp.where(lane == 2 * TOP_K + r, w_r, route)
    route_ref[...] = route


def _dispatch_kernel(pstart_ref, count_ref, nact_ref, route_ref, h_ref, xr_hbm, zero_scr, sem, zsem):
    def row_copy(ti, dest):
        src = h_ref.at[pl.ds(pl.multiple_of(ti * ROW_SLABS, ROW_SLABS), ROW_SLABS)]
        return pltpu.make_async_copy(src, xr_hbm.at[dest], sem)

    def issue(ti, carry):
        for k in range(TOP_K):
            e = route_ref[ti * 2 * TOP_K + k]
            pos = route_ref[ti * 2 * TOP_K + TOP_K + k]
            row_copy(ti, pstart_ref[e] + pos).start(priority=k % 2)
        return carry

    lax.fori_loop(0, DISPATCH_TILE, issue, 0)

    @pl.when(pl.program_id(0) == 0)
    def _():
        zero_scr[...] = jnp.zeros_like(zero_scr)

        def pad_rows(e):
            first = pstart_ref[e] + count_ref[e]
            n_pad = (-count_ref[e]) & (EXPERT_ROWS - 1)
            return first, n_pad

        def zero_copy(dest):
            return pltpu.make_async_copy(zero_scr.at[0], xr_hbm.at[dest], zsem)

        def issue_e(e, carry):
            first, n_pad = pad_rows(e)
            lax.fori_loop(0, n_pad, lambda r, c: (zero_copy(first + r).start(), c)[1], 0)
            return carry

        def drain_e(e, carry):
            _, n_pad = pad_rows(e)
            lax.fori_loop(0, n_pad, lambda r, c: (zero_copy(0).wait(), c)[1], 0)
            return carry

        lax.fori_loop(0, N_EXPERTS, issue_e, 0)
        lax.fori_loop(0, N_EXPERTS, drain_e, 0)

        n_blocks = xr_hbm.shape[0] // EXPERT_ROWS

        def block_copy(blk):
            dst = xr_hbm.at[pl.ds(pl.multiple_of(blk * EXPERT_ROWS, EXPERT_ROWS), EXPERT_ROWS)]
            return pltpu.make_async_copy(zero_scr, dst, zsem)

        lax.fori_loop(nact_ref[0], n_blocks, lambda blk, c: (block_copy(blk).start(), c)[1], 0)
        lax.fori_loop(nact_ref[0], n_blocks, lambda blk, c: (block_copy(0).wait(), c)[1], 0)

    def drain(ti, carry):
        for k in range(TOP_K):
            row_copy(0, 0).wait()
        return carry

    lax.fori_loop(0, DISPATCH_TILE, drain, 0)


def _expert_kernel(blk_e_ref, nact_ref, xr_ref, wgu_ref, bgu_ref, wdn_ref, bdn_ref, yr_ref,
                   wgu_bf, wdn_bf):
    r = EXPERT_ROWS
    i = pl.program_id(0)
    active = i < nact_ref[0]
    new_expert = jnp.logical_or(i == 0, blk_e_ref[i] != blk_e_ref[jnp.maximum(i - 1, 0)])

    @pl.when(jnp.logical_and(active, new_expert))
    def _():
        def cast_rows(c, carry):
            rows = pl.ds(pl.multiple_of(c * LANES, LANES), LANES)
            wgu_bf[rows, :] = wgu_ref[0, rows, :].astype(BF16)
            wdn_bf[rows, :] = wdn_ref[0, rows, :].astype(BF16)
            return carry

        lax.fori_loop(0, D_MODEL // LANES, cast_rows, 0)

    @pl.when(active)
    def _():
        x = jnp.concatenate(
            [xr_ref[pl.ds(c, r, stride=ROW_SLABS), :] for c in range(ROW_SLABS)], axis=1)
        gu = jnp.dot(x.astype(BF16), wgu_bf[...], preferred_element_type=F32) + bgu_ref[0]
        g = jnp.minimum(gu[:, :D_EXPERT], SWIGLU_LIMIT)
        u = jnp.clip(gu[:, D_EXPERT:], -SWIGLU_LIMIT, SWIGLU_LIMIT)
        act = (u + 1.0) * (g * jax.nn.sigmoid(SWIGLU_ALPHA * g))
        y = jnp.dot(act.astype(BF16), wdn_bf[...], preferred_element_type=F32) + bdn_ref[0]
        for c in range(ROW_SLABS):
            yr_ref[pl.ds(c, r, stride=ROW_SLABS), :] = y[:, c * LANES:(c + 1) * LANES]

    @pl.when(jnp.logical_not(active))
    def _():
        yr_ref[...] = jnp.zeros_like(yr_ref)


def _final_kernel(pstart_ref, route_cur_ref, route_nxt_ref, x1_ref, route_ref, p_ref, gple_ref,
                  wpg_ref, wpp_ref, yr_hbm, o_ref, gbuf, sems):
    t = FINAL_TILE
    i = pl.program_id(0)
    slot = lax.rem(i, 2)

    def row_copy(dest, slot_, k, ti):
        dst = gbuf.at[slot_ * TOP_K + k,
                      pl.ds(pl.multiple_of(ti * ROW_SLABS, ROW_SLABS), ROW_SLABS)]
        return pltpu.make_async_copy(yr_hbm.at[dest], dst, sems.at[slot_])

    def fetch_tile(route_i_ref, slot_):
        def issue(ti, carry):
            for k in range(TOP_K):
                e = route_i_ref[ti * 2 * TOP_K + k]
                pos = route_i_ref[ti * 2 * TOP_K + TOP_K + k]
                row_copy(pstart_ref[e] + pos, slot_, k, ti).start(priority=k % 2)
            return carry

        lax.fori_loop(0, t, issue, 0)

    @pl.when(i == 0)
    def _():
        fetch_tile(route_cur_ref, 0)

    @pl.when(i + 1 < pl.num_programs(0))
    def _():
        fetch_tile(route_nxt_ref, 1 - slot)

    def drain(ti, carry):
        for k in range(TOP_K):
            row_copy(0, slot, k, ti).wait()
        return carry

    lax.fori_loop(0, t, drain, 0)

    route = route_ref[...]
    w = [route[:, 2 * TOP_K + k:2 * TOP_K + k + 1] for k in range(TOP_K)]
    chunks = []
    for c in range(ROW_SLABS):
        acc = w[0] * gbuf[slot * TOP_K, pl.ds(c, t, stride=ROW_SLABS), :]
        for k in range(1, TOP_K):
            acc = acc + w[k] * gbuf[slot * TOP_K + k, pl.ds(c, t, stride=ROW_SLABS), :]
        chunks.append(acc)
    x2 = x1_ref[...] + jnp.concatenate(chunks, axis=1)
    hp = _rms(x2, gple_ref[...]).astype(BF16)
    gate = jax.nn.sigmoid(jnp.dot(hp, wpg_ref[...], preferred_element_type=F32))
    emb = jnp.dot(p_ref[...].astype(BF16), wpp_ref[...], preferred_element_type=F32)
    o_ref[...] = x2 + gate * emb


def _rope_tables(s, half, x1_starts, x2_starts):
    inv_freq = ROPE_THETA ** (-(jnp.arange(half, dtype=F32) / half))
    ang = jnp.arange(s, dtype=F32)[:, None] * inv_freq[None, :]
    cos, sin = jnp.cos(ang), jnp.sin(ang)
    cos_t, sin_t = jnp.ones((s, LANES), F32), jnp.zeros((s, LANES), F32)
    for st in x1_starts:
        cos_t = cos_t.at[:, st:st + half].set(cos)
        sin_t = sin_t.at[:, st:st + half].set(-sin)
    for st in x2_starts:
        cos_t = cos_t.at[:, st:st + half].set(cos)
        sin_t = sin_t.at[:, st:st + half].set(sin)
    return cos_t, sin_t


_MOBA_LANE_COLS = tuple(list(range(0, 32)) + list(range(64, 96)) + list(range(32, 64))
                        + list(range(96, 128)))
_MLA_LANE_DIMS = tuple(list(range(80, 96)) + list(range(0, 48)) + list(range(64, 80))
                       + list(range(48, 64)) + [MLA_QK_DIM] * 32)


def _moba_lanes(w):
    k, width = w.shape
    cols = jnp.asarray(_MOBA_LANE_COLS, jnp.int32)
    return w.reshape(k, width // LANES, LANES)[:, :, cols].reshape(k, width)


def _mla_lanes(w, heads):
    k = w.shape[0]
    w = jnp.pad(w.reshape(k, heads, MLA_QK_DIM), ((0, 0), (0, 0), (0, 1)))
    return w[:, :, jnp.asarray(_MLA_LANE_DIMS, jnp.int32)].reshape(k, heads * LANES)


def _row(v):
    return v.reshape(1, -1).astype(F32)


def _layer(x, p_i, g_mix, w_in, moba_q_norm, moba_k_norm, mla_q_lat_norm, w_uq, mla_kv_lat_norm,
           w_ukv, mla_q_norm, mla_k_norm, w_branch_a, w_branch_b, w_out, g_ffn, w_router, b_router,
           w_gate_up, b_gate_up, w_down, b_down, g_ple, w_ple_gate, w_ple_proj):
    b, s, d = x.shape
    n = b * s
    assert d == D_MODEL and s % ATTN_TILE == 0 and n % DISPATCH_TILE == 0
    assert s // MOBA_BLOCK <= GATE_ROWS and TOKEN_TILE == ATTN_TILE == MOBA_BLOCK
    assert D_EXPERT == D_MODEL and n % FINAL_TILE == 0
    n_tiles = n // TOKEN_TILE
    tiles_per_seq = s // TOKEN_TILE
    xf = x.reshape(n, d)

    off = [0]
    for wdt in (MOBA_WIDTH, MOBA_WIDTH, MOBA_WIDTH, MLA_Q_LORA, MLA_KV_LORA, MLA_ROPE_DIM, D_MODEL, D_MODEL):
        off.append(off[-1] + wdt)
    seg = [w_in[:, off[i]:off[i + 1]] for i in range(8)]
    kpe_cols = _mla_lanes(jnp.pad(seg[5], ((0, 0), (MLA_NOPE_DIM, 0))), 1)
    w_in_p = jnp.concatenate([_moba_lanes(seg[0]), _moba_lanes(seg[1])] + seg[2:5] + [kpe_cols]
                             + seg[6:], axis=1).astype(BF16)
    assert w_in_p.shape[1] == D_IN_PACKED
    w_uq_p = _mla_lanes(w_uq, MLA_HEADS).astype(BF16)
    w_ukv_h = w_ukv.reshape(MLA_KV_LORA, MLA_HEADS, MLA_NOPE_DIM + MLA_V_DIM)
    w_uk_p = _mla_lanes(jnp.pad(w_ukv_h[:, :, :MLA_NOPE_DIM], ((0, 0), (0, 0), (0, MLA_ROPE_DIM)))
                        .reshape(MLA_KV_LORA, -1), MLA_HEADS).astype(BF16)
    w_uv = w_ukv_h[:, :, MLA_NOPE_DIM:].reshape(MLA_KV_LORA, MLA_WIDTH).astype(BF16)
    gqa = _moba_lanes(_row(jnp.tile(moba_q_norm, 2))) * (MOBA_HEAD_DIM ** -0.5)
    gka = _moba_lanes(_row(jnp.tile(moba_k_norm, 2)))
    gqb = _mla_lanes(_row(mla_q_norm), 1) * (MLA_QK_DIM ** -0.5)
    gkb = _mla_lanes(_row(mla_k_norm), 1)
    half_a, half_b = MOBA_HEAD_DIM // 2, MLA_ROPE_DIM // 2
    cosa, sina = _rope_tables(s, half_a, (0, half_a), (LANES // 2, LANES // 2 + half_a))
    cosb, sinb = _rope_tables(s, half_b, (LANES // 2,), (0,))

    tok = lambda width: pl.BlockSpec((TOKEN_TILE, width), lambda i: (i, 0))
    whole = lambda arr: pl.BlockSpec(arr.shape, lambda i: (0,) * arr.ndim)
    seq_tab = pl.BlockSpec((TOKEN_TILE, LANES), lambda i: (i % tiles_per_seq, 0))
    vt_spec = pl.BlockSpec((1, MOBA_WIDTH, TOKEN_TILE), lambda i: (i, 0, 0))
    params = pltpu.CompilerParams(dimension_semantics=("arbitrary",), vmem_limit_bytes=VMEM_LIMIT)

    consts1 = [_row(g_mix), w_in_p, gqa, gka]
    consts2 = [_row(mla_q_lat_norm), w_uq_p, _row(mla_kv_lat_norm), w_uk_p, w_uv, gqb, gkb]
    qa, ka, va, kmean, qb, kb, vb, ga, gb = pl.pallas_call(
        _inproj_kernel,
        grid=(n_tiles,),
        in_specs=([tok(d)] + [whole(a) for a in consts1] + [seq_tab] * 2
                  + [whole(a) for a in consts2] + [seq_tab] * 2),
        out_specs=[tok(MOBA_WIDTH), tok(MOBA_WIDTH), vt_spec,
                   pl.BlockSpec((1, 1, MOBA_WIDTH), lambda i: (i, 0, 0)),
                   tok(MLA_HEADS * LANES), tok(MLA_HEADS * LANES), vt_spec,
                   tok(d), tok(d)],
        out_shape=[jax.ShapeDtypeStruct((n, MOBA_WIDTH), BF16)] * 2
        + [jax.ShapeDtypeStruct((n_tiles, MOBA_WIDTH, TOKEN_TILE), BF16)]
        + [jax.ShapeDtypeStruct((n_tiles, 1, MOBA_WIDTH), F32)]
        + [jax.ShapeDtypeStruct((n, MLA_HEADS * LANES), BF16)] * 2
        + [jax.ShapeDtypeStruct((n_tiles, MLA_WIDTH, TOKEN_TILE), BF16)]
        + [jax.ShapeDtypeStruct((n, d), BF16)] * 2,
        compiler_params=params,
        name="in_projection",
    )(xf, *consts1, cosa, sina, *consts2, cosb, sinb)

    kmean = kmean.reshape(b, tiles_per_seq, MOBA_WIDTH)
    kmean = jnp.pad(kmean, ((0, 0), (0, GATE_ROWS - tiles_per_seq), (0, 0))).astype(BF16)

    r3 = lambda a: a.reshape(b, s, a.shape[-1])
    r4 = lambda a: a.reshape(b, tiles_per_seq, a.shape[1], TOKEN_TILE)
    ya = _attention(r3(qa), r3(ka), r4(va), kmean, moba=True).reshape(n, MOBA_WIDTH)
    yb = _attention(r3(qb), r3(kb), r4(vb), None, moba=False).reshape(n, MLA_WIDTH)

    wr_p = jnp.pad(w_router, ((0, 0), (0, LANES - N_EXPERTS))).astype(BF16)
    br_p = jnp.pad(b_router.astype(F32), (0, LANES - N_EXPERTS), constant_values=NEG).reshape(1, LANES)
    consts3 = [w_branch_a.astype(BF16), w_branch_b.astype(BF16), w_out.astype(BF16), _row(g_ffn),
               wr_p, br_p]
    x1, hslab, route, cnt = pl.pallas_call(
        _merge_kernel,
        grid=(n_tiles,),
        in_specs=[tok(d), tok(MOBA_WIDTH), tok(MLA_WIDTH), tok(d), tok(d)]
        + [whole(a) for a in consts3],
        out_specs=[tok(d), pl.BlockSpec((TOKEN_TILE * ROW_SLABS, LANES), lambda i: (i, 0)),
                   tok(LANES), pl.BlockSpec((SUBLANES, LANES), lambda i: (0, 0))],
        out_shape=[jax.ShapeDtypeStruct((n, d), F32),
                   jax.ShapeDtypeStruct((n * ROW_SLABS, LANES), F32),
                   jax.ShapeDtypeStruct((n, LANES), F32),
                   jax.ShapeDtypeStruct((SUBLANES, LANES), F32)],
        scratch_shapes=[pltpu.VMEM((SUBLANES, LANES), F32)],
        compiler_params=params,
        name="merge_router",
    )(xf, ya, yb, ga, gb, *consts3)

    rb = EXPERT_ROWS
    n_blocks = (n * TOP_K) // rb + N_EXPERTS
    n_rows = n_blocks * rb
    counts = cnt[0, :N_EXPERTS].astype(jnp.int32)
    pcounts = ((counts + rb - 1) // rb) * rb
    pends = jnp.cumsum(pcounts)
    pstarts = (pends - pcounts).astype(jnp.int32)
    nact = (pends[-1] // rb).astype(jnp.int32).reshape(1)
    blk = jnp.minimum(jnp.arange(n_blocks, dtype=jnp.int32), nact[0] - 1)
    blk_e = jnp.sum((pends[None, :] <= (blk * rb)[:, None]).astype(jnp.int32), axis=1)
    blk_e = jnp.minimum(blk_e, N_EXPERTS - 1)
    route_i = route[:, :2 * TOP_K].astype(jnp.int32).reshape(n * 2 * TOP_K)

    xr = pl.pallas_call(
        _dispatch_kernel,
        grid_spec=pltpu.PrefetchScalarGridSpec(
            num_scalar_prefetch=3,
            grid=(n // DISPATCH_TILE,),
            in_specs=[pl.BlockSpec((DISPATCH_TILE * 2 * TOP_K,), lambda i, ps, ct, na: (i,),
                                   memory_space=pltpu.SMEM),
                      pl.BlockSpec((DISPATCH_TILE * ROW_SLABS, LANES),
                                   lambda i, ps, ct, na: (i, 0))],
            out_specs=pl.BlockSpec(memory_space=pl.ANY),
            scratch_shapes=[pltpu.VMEM((EXPERT_ROWS, ROW_SLABS, LANES), F32),
                            pltpu.SemaphoreType.DMA(()), pltpu.SemaphoreType.DMA(())]),
        out_shape=jax.ShapeDtypeStruct((n_rows, ROW_SLABS, LANES), F32),
        compiler_params=params,
        name="dispatch_rows",
    )(pstarts, counts, nact, route_i, hslab)

    act_blk = lambda i, be, na: jnp.minimum(i, na[0] - 1)
    yr = pl.pallas_call(
        _expert_kernel,
        grid_spec=pltpu.PrefetchScalarGridSpec(
            num_scalar_prefetch=2,
            grid=(n_blocks,),
            in_specs=[
                pl.BlockSpec((rb * ROW_SLABS, LANES), lambda i, be, na: (act_blk(i, be, na), 0)),
                pl.BlockSpec((1, d, 2 * D_EXPERT), lambda i, be, na: (be[i], 0, 0)),
                pl.BlockSpec((1, 1, 2 * D_EXPERT), lambda i, be, na: (be[i], 0, 0)),
                pl.BlockSpec((1, D_EXPERT, d), lambda i, be, na: (be[i], 0, 0)),
                pl.BlockSpec((1, 1, d), lambda i, be, na: (be[i], 0, 0)),
            ],
            out_specs=pl.BlockSpec((rb * ROW_SLABS, LANES), lambda i, be, na: (i, 0)),
            scratch_shapes=[pltpu.VMEM((d, 2 * D_EXPERT), BF16), pltpu.VMEM((D_EXPERT, d), BF16)]),
        out_shape=jax.ShapeDtypeStruct((n_rows * ROW_SLABS, LANES), F32),
        compiler_params=params,
        name="experts",
    )(blk_e, nact, xr.reshape(n_rows * ROW_SLABS, LANES), w_gate_up.astype(F32),
      b_gate_up.reshape(N_EXPERTS, 1, -1).astype(F32), w_down.astype(F32),
      b_down.reshape(N_EXPERTS, 1, -1).astype(F32))

    final_steps = n // FINAL_TILE
    ftok = lambda width: pl.BlockSpec((FINAL_TILE, width), lambda i, ps: (i, 0))
    fwhole = lambda arr: pl.BlockSpec(arr.shape, lambda i, ps: (0,) * arr.ndim)
    consts4 = [_row(g_ple), w_ple_gate.astype(BF16), w_ple_proj.astype(BF16)]
    out = pl.pallas_call(
        _final_kernel,
        grid_spec=pltpu.PrefetchScalarGridSpec(
            num_scalar_prefetch=1,
            grid=(n // FINAL_TILE,),
            in_specs=[pl.BlockSpec((FINAL_TILE * 2 * TOP_K,), lambda i, ps: (i,),
                                   memory_space=pltpu.SMEM),
                      pl.BlockSpec((FINAL_TILE * 2 * TOP_K,),
                                   lambda i, ps: (jnp.minimum(i + 1, final_steps - 1),),
                                   memory_space=pltpu.SMEM),
                      ftok(d), ftok(LANES), ftok(PLE_DIM)]
            + [fwhole(a) for a in consts4] + [pl.BlockSpec(memory_space=pl.ANY)],
            out_specs=ftok(d),
            scratch_shapes=[pltpu.VMEM((2 * TOP_K, FINAL_TILE * ROW_SLABS, LANES), F32),
                            pltpu.SemaphoreType.DMA((2,))]),
        out_shape=jax.ShapeDtypeStruct((n, d), F32),
        compiler_params=params,
        name="combine_ple",
    )(pstarts, route_i, route_i, x1, route, p_i.reshape(n, PLE_DIM), *consts4,
      yr.reshape(n_rows, ROW_SLABS, LANES))
    return out.reshape(b, s, d)


def kernel(x, p, g_mix, w_in, moba_q_norm, moba_k_norm, mla_q_lat_norm, w_uq, mla_kv_lat_norm, w_ukv, mla_q_norm, mla_k_norm, w_branch_a, w_branch_b, w_out, g_ffn, w_router, b_router, w_gate_up, b_gate_up, w_down, b_down, g_ple, w_ple_gate, w_ple_proj):
    for i in range(p.shape[0]):
        x = _layer(x, p[i], g_mix[i], w_in[i], moba_q_norm[i], moba_k_norm[i], mla_q_lat_norm[i],
                   w_uq[i], mla_kv_lat_norm[i], w_ukv[i], mla_q_norm[i], mla_k_norm[i],
                   w_branch_a[i], w_branch_b[i], w_out[i], g_ffn[i], w_router[i], b_router[i],
                   w_gate_up[i], b_gate_up[i], w_down[i], b_down[i], g_ple[i], w_ple_gate[i],
                   w_ple_proj[i])
    return x
```

```python
import functools

import numpy as np
import jax
import jax.numpy as jnp
from jax import lax
from jax.experimental import pallas as pl
from jax.experimental.pallas import tpu as pltpu

F32 = jnp.float32
BF16 = jnp.bfloat16

D_MODEL = 1024
PLE_DIM = 256
EPS = 1e-6
ROPE_THETA = 10000.0
MOBA_HEADS = 8
MOBA_HEAD_DIM = 64
MOBA_BLOCK = 256
MOBA_TOPK = 3
MOBA_WIDTH = MOBA_HEADS * MOBA_HEAD_DIM
MLA_HEADS = 8
MLA_Q_LORA = 256
MLA_KV_LORA = 128
MLA_NOPE_DIM = 64
MLA_ROPE_DIM = 32
MLA_V_DIM = 64
MLA_QK_DIM = MLA_NOPE_DIM + MLA_ROPE_DIM
MLA_WIDTH = MLA_HEADS * MLA_V_DIM
N_EXPERTS = 32
TOP_K = 4
D_EXPERT = 1024
SWIGLU_LIMIT = 7.0
SWIGLU_ALPHA = 1.702

LANES = 128
SUBLANES = 8
ROW_SLABS = D_MODEL // LANES
VMEM_LIMIT = 56 * 1024 * 1024

TOKEN_TILE = 256
ATTN_TILE = 256
ATTN_GROUPS = 4
EXPERT_ROWS = 256
RUN_CHUNK = 8
RUN_SLOTS = -(-(TOKEN_TILE * TOP_K + N_EXPERTS * (RUN_CHUNK - 1)) // 256) * 256

NEG = -1e30
MASK_BIAS = -1e9

C_QA, C_KA, C_VA = 0, 512, 1024
C_CQ, C_CKV, C_KPE = 1536, 1792, 1920
C_GA, C_GB = 2048, 3072
D_IN_PACKED = 4096


def _rms(x, gain):
    return x * lax.rsqrt(jnp.mean(x * x, axis=-1, keepdims=True) + EPS) * gain


def _rope(t, cos, sin):
    return t * cos + pltpu.roll(t, LANES // 2, 1) * sin


def _moba_even_head(lane):
    return (lane & (MOBA_HEAD_DIM // 2)) == 0


def _inproj_kernel(x_ref, gmix_ref, win_ref, gqa_ref, gka_ref, cosa_ref, sina_ref,
                   gql_ref, wuq_ref, gkvl_ref, wuk_ref, wuv_ref, gqb_ref, gkb_ref,
                   cosb_ref, sinb_ref,
                   qa_ref, ka_ref, vat_ref, kmean_ref, qb_ref, kb_ref, vbt_ref, ga_ref, gb_ref):
    hn = _rms(x_ref[...], gmix_ref[...]).astype(BF16)

    def proj(c0, width):
        return jnp.dot(hn, win_ref[:, c0:c0 + width], preferred_element_type=F32)

    first = _moba_even_head(lax.broadcasted_iota(jnp.int32, (TOKEN_TILE, LANES), 1))
    cosa, sina = cosa_ref[...], sina_ref[...]

    def moba_norm_rope(t, gain):
        sq = t * t
        ss0 = jnp.sum(jnp.where(first, sq, 0.0), axis=-1, keepdims=True)
        ss1 = jnp.sum(jnp.where(first, 0.0, sq), axis=-1, keepdims=True)
        ms = jnp.where(first, ss0, ss1) * (1.0 / MOBA_HEAD_DIM)
        t = t * lax.rsqrt(ms + EPS) * gain
        return _rope(t, cosa, sina)

    qa = proj(C_QA, MOBA_WIDTH)
    ka = proj(C_KA, MOBA_WIDTH)
    for c in range(MOBA_WIDTH // LANES):
        sl = slice(c * LANES, (c + 1) * LANES)
        qa_ref[:, sl] = moba_norm_rope(qa[:, sl], gqa_ref[...]).astype(BF16)
        kc = moba_norm_rope(ka[:, sl], gka_ref[...])
        ka_ref[:, sl] = kc.astype(BF16)
        kmean_ref[0, :, sl] = jnp.mean(kc, axis=0, keepdims=True)
    vat_ref[0] = proj(C_VA, MOBA_WIDTH).T.astype(BF16)

    cosb, sinb = cosb_ref[...], sinb_ref[...]

    def mla_norm_rope(t, gain):
        ms = jnp.sum(t * t, axis=-1, keepdims=True) * (1.0 / MLA_QK_DIM)
        t = t * lax.rsqrt(ms + EPS) * gain
        return _rope(t, cosb, sinb)

    cq = _rms(proj(C_CQ, MLA_Q_LORA), gql_ref[...]).astype(BF16)
    qb = jnp.dot(cq, wuq_ref[...], preferred_element_type=F32)
    ckv = _rms(proj(C_CKV, MLA_KV_LORA), gkvl_ref[...]).astype(BF16)
    kn = jnp.dot(ckv, wuk_ref[...], preferred_element_type=F32)
    kpe = proj(C_KPE, LANES)
    for h in range(MLA_HEADS):
        sl = slice(h * LANES, (h + 1) * LANES)
        qb_ref[:, sl] = mla_norm_rope(qb[:, sl], gqb_ref[...]).astype(BF16)
        kb_ref[:, sl] = mla_norm_rope(kn[:, sl] + kpe, gkb_ref[...]).astype(BF16)
    vbt_ref[0] = jnp.dot(ckv, wuv_ref[...], preferred_element_type=F32).T.astype(BF16)

    ga_ref[...] = jax.nn.sigmoid(proj(C_GA, D_MODEL)).astype(BF16)
    gb_ref[...] = jax.nn.sigmoid(proj(C_GB, D_MODEL)).astype(BF16)


_NT = (((1,), (1,)), ((), ()))


GATE_ROWS = 16


def _attn_kernel(*refs, moba):
    if moba:
        q_ref, k_ref, vt_ref, kmean_ref, o_ref = refs
    else:
        q_ref, k_ref, vt_ref, o_ref = refs
    t = ATTN_TILE
    hd = MOBA_HEAD_DIM
    n_heads = 2 * ATTN_GROUPS
    qi = pl.program_id(2)
    key_i = lax.broadcasted_iota(jnp.int32, (t, t), 0)
    qry_i = lax.broadcasted_iota(jnp.int32, (t, t), 1)

    blk = lax.broadcasted_iota(jnp.int32, (GATE_ROWS, t), 0)
    heads, biases = [], []
    for hh in range(n_heads):
        if moba:
            lane = lax.broadcasted_iota(jnp.int32, (t, LANES), 1)
            even = _moba_even_head(lane)
            head_lanes = even if hh % 2 == 0 else jnp.logical_not(even)
            kcols = slice((hh // 2) * LANES, (hh // 2 + 1) * LANES)
            q = jnp.where(head_lanes, q_ref[0, :, kcols], jnp.zeros((), BF16))
            gate = lax.dot_general(kmean_ref[0, :, kcols], q, _NT, preferred_element_type=F32)
            g = jnp.where(blk < qi, gate, -jnp.inf)
            keep = jnp.zeros((GATE_ROWS, t), F32)
            for _ in range(MOBA_TOPK):
                gmax = jnp.max(g, axis=0, keepdims=True)
                pick = jnp.min(jnp.where(g == gmax, blk, GATE_ROWS), axis=0, keepdims=True)
                hit = blk == jnp.where(gmax > -jnp.inf, pick, GATE_ROWS)
                keep = jnp.where(hit, 1.0, keep)
                g = jnp.where(hit, -jnp.inf, g)
            biases.append(jnp.where(keep > 0.0, 0.0, MASK_BIAS))
        else:
            kcols = slice(hh * LANES, (hh + 1) * LANES)
            q = q_ref[0, :, kcols]
        heads.append((q, kcols))

    def update(s, vt_blk, state):
        m_prev, l_prev, acc = state
        m_new = jnp.maximum(m_prev, jnp.max(s, axis=0, keepdims=True))
        alpha = jnp.exp(m_prev - m_new)
        p = jnp.exp(s - m_new)
        l_new = alpha * l_prev + jnp.sum(p, axis=0, keepdims=True)
        acc = alpha * acc + jnp.dot(vt_blk, p.astype(BF16), preferred_element_type=F32)
        return m_new, l_new, acc

    def past_block(j, states):
        start = pl.multiple_of(j * t, t)
        scores = [lax.dot_general(k_ref[0, pl.ds(start, t), kcols], q, _NT,
                                  preferred_element_type=F32) for q, kcols in heads]
        out = []
        for hh, s in enumerate(scores):
            if moba:
                s = jnp.sum(jnp.where(blk == j, biases[hh], 0.0), axis=0, keepdims=True) + s
            out.append(update(s, vt_ref[0, j, hh * hd:(hh + 1) * hd, :], states[hh]))
        return tuple(out)

    init = (jnp.full((1, t), NEG, F32), jnp.zeros((1, t), F32), jnp.zeros((hd, t), F32))
    states = lax.fori_loop(0, qi, past_block, (init,) * n_heads)

    diag_start = pl.multiple_of(qi * t, t)
    scores = [lax.dot_general(k_ref[0, pl.ds(diag_start, t), kcols], q, _NT,
                              preferred_element_type=F32) for q, kcols in heads]
    outs = []
    for hh, s in enumerate(scores):
        s = jnp.where(key_i <= qry_i, s, NEG)
        _, l_fin, acc = update(s, vt_ref[0, qi, hh * hd:(hh + 1) * hd, :], states[hh])
        outs.append(acc / l_fin)
    o_ref[0] = jnp.concatenate(outs, axis=0).T.astype(BF16)


def _attention(q, k, vt, kmean, *, moba):
    b, s, _ = q.shape
    v_cols = ATTN_GROUPS * LANES
    steps = vt.shape[2] // v_cols
    nblk = s // ATTN_TILE
    qk_cols = v_cols if moba else 2 * v_cols
    in_specs = [
        pl.BlockSpec((1, ATTN_TILE, qk_cols), lambda bi, gi, qi: (bi, qi, gi)),
        pl.BlockSpec((1, s, qk_cols), lambda bi, gi, qi: (bi, 0, gi)),
        pl.BlockSpec((1, nblk, v_cols, ATTN_TILE), lambda bi, gi, qi: (bi, 0, gi, 0)),
    ]
    args = [q, k, vt]
    if moba:
        in_specs.append(pl.BlockSpec((1, GATE_ROWS, v_cols), lambda bi, gi, qi: (bi, 0, gi)))
        args.append(kmean)
    return pl.pallas_call(
        functools.partial(_attn_kernel, moba=moba),
        grid=(b, steps, nblk),
        in_specs=in_specs,
        out_specs=pl.BlockSpec((1, ATTN_TILE, v_cols), lambda bi, gi, qi: (bi, qi, gi)),
        out_shape=jax.ShapeDtypeStruct((b, s, steps * v_cols), BF16),
        compiler_params=pltpu.CompilerParams(
            dimension_semantics=("arbitrary", "arbitrary", "arbitrary"),
            vmem_limit_bytes=VMEM_LIMIT),
        name="moba_attention" if moba else "mla_attention",
    )(*args)


def _merge_kernel(x_ref, ya_ref, yb_ref, ga_ref, gb_ref, wa_ref, wb_ref, wo_ref, gffn_ref,
                  wr_ref, br_ref, x1_ref, h_ref, route_ref, meta_ref, cnt_ref, run_scr):
    t = TOKEN_TILE

    @pl.when(pl.program_id(0) == 0)
    def _():
        run_scr[...] = jnp.zeros_like(run_scr)

    merged = (ga_ref[...].astype(F32) * jnp.dot(ya_ref[...], wa_ref[...], preferred_element_type=F32)
              + gb_ref[...].astype(F32) * jnp.dot(yb_ref[...], wb_ref[...], preferred_element_type=F32))
    x1 = x_ref[...] + jnp.dot(merged.astype(BF16), wo_ref[...], preferred_element_type=F32)
    x1_ref[...] = x1
    h = _rms(x1, gffn_ref[...]).astype(BF16)
    h_ref[...] = h

    logits = jnp.dot(h, wr_ref[...], preferred_element_type=F32) + br_ref[...]
    lane = lax.broadcasted_iota(jnp.int32, (t, LANES), 1)
    lg = logits
    hits = []
    top = None
    for r in range(TOP_K):
        gmax = jnp.max(lg, axis=-1, keepdims=True)
        pick = jnp.min(jnp.where(lg == gmax, lane, LANES), axis=-1, keepdims=True)
        hit = lane == pick
        if r == 0:
            top = gmax
        hits.append(hit)
        lg = jnp.where(hit, -jnp.inf, lg)
    sel = jnp.where(lg == -jnp.inf, 1.0, 0.0)
    wgt = sel * jnp.exp(logits - top)
    wgt = wgt / jnp.sum(wgt, axis=-1, keepdims=True)

    r_i = lax.broadcasted_iota(jnp.int32, (t, t), 0)
    c_i = lax.broadcasted_iota(jnp.int32, (t, t), 1)
    lower = jnp.where(c_i < r_i, 1.0, 0.0).astype(BF16)
    rank_in_tile = jnp.dot(lower, sel.astype(BF16), preferred_element_type=F32)
    tcnt = jnp.sum(sel, axis=0, keepdims=True)
    tpad = jnp.floor((tcnt + (RUN_CHUNK - 1)) * (1.0 / RUN_CHUNK)) * RUN_CHUNK
    e_r = lax.broadcasted_iota(jnp.int32, (LANES, LANES), 0)
    e_c = lax.broadcasted_iota(jnp.int32, (LANES, LANES), 1)
    before = jnp.where(e_r < e_c, 1.0, 0.0).astype(BF16)
    tbase = jnp.dot(jnp.broadcast_to(tpad, (SUBLANES, LANES)).astype(BF16), before,
                    preferred_element_type=F32)[0:1, :]
    run = run_scr[0:1, :]
    run_new = run + tcnt
    run_scr[...] = jnp.broadcast_to(run_new, run_scr.shape)
    cnt_ref[...] = jnp.broadcast_to(run_new, cnt_ref.shape)
    row = lax.broadcasted_iota(jnp.int32, (SUBLANES, LANES), 0)
    meta_ref[0] = jnp.where(row == 0, tcnt, jnp.where(row == 1, run, jnp.where(row == 2, tbase, 0.0)))

    slot_of = rank_in_tile + tbase
    route = jnp.zeros((t, LANES), F32)
    for r in range(TOP_K):
        w_r = jnp.sum(jnp.where(hits[r], wgt, 0.0), axis=-1, keepdims=True)
        slot_r = jnp.sum(jnp.where(hits[r], slot_of, 0.0), axis=-1, keepdims=True)
        route = jnp.where(lane == r, w_r, route)
        route = jnp.where(lane == TOP_K + r, slot_r, route)
    route_ref[...] = route


def _for_each_run_chunk(meta_ref, pstart_ref, fn):
    def per_expert(e, carry):
        n_chunks = lax.shift_right_logical(meta_ref[e] + (RUN_CHUNK - 1), RUN_CHUNK.bit_length() - 1)
        slot0 = meta_ref[2 * LANES + e]
        row0 = pstart_ref[e] + meta_ref[LANES + e]

        def per_chunk(c, cc):
            fn(slot0 + c * RUN_CHUNK, row0 + c * RUN_CHUNK)
            return cc

        lax.fori_loop(0, n_chunks, per_chunk, 0)
        return carry

    lax.fori_loop(0, N_EXPERTS, per_expert, 0)


def _slab_rows(first_row, n_rows):
    return pl.ds(pl.multiple_of(first_row * ROW_SLABS, ROW_SLABS), n_rows * ROW_SLABS)


def _dispatch_kernel(pstart_ref, pend_ref, nact_ref, meta_ref, meta_prev_ref, route_ref, h_ref,
                     xr_hbm, stage, zero_scr, sems, zsem):
    t = TOKEN_TILE
    i = pl.program_id(0)
    half = lax.rem(i, 2)

    def run_copy(buf_half, slot, row):
        src = stage.at[_slab_rows(buf_half * RUN_SLOTS + slot, RUN_CHUNK), :]
        return pltpu.make_async_copy(src, xr_hbm.at[_slab_rows(row, RUN_CHUNK), :], sems.at[buf_half])

    @pl.when(i == 0)
    def _():
        zero_scr[...] = jnp.zeros_like(zero_scr)
        n_blocks = xr_hbm.shape[0] // (EXPERT_ROWS * ROW_SLABS)

        def block_copy(first_row):
            return pltpu.make_async_copy(zero_scr, xr_hbm.at[_slab_rows(first_row, EXPERT_ROWS), :], zsem)

        def zero_last(e, c):
            block_copy(pend_ref[e] - EXPERT_ROWS).start()
            return c

        def zero_tail(blk, c):
            block_copy(blk * EXPERT_ROWS).start()
            return c

        lax.fori_loop(0, N_EXPERTS, zero_last, 0)
        lax.fori_loop(nact_ref[0], n_blocks, zero_tail, 0)
        lax.fori_loop(0, N_EXPERTS + n_blocks - nact_ref[0], lambda r, c: (block_copy(0).wait(), c)[1], 0)

    slot_rows = route_ref[...].T[TOP_K:2 * TOP_K, :]
    s_i = lax.broadcasted_iota(jnp.int32, (RUN_SLOTS, t), 0).astype(F32)
    pick = jnp.zeros((RUN_SLOTS, t), F32)
    for k in range(TOP_K):
        pick = jnp.where(s_i == slot_rows[k:k + 1, :], 1.0, pick)
    rows = jnp.dot(pick.astype(BF16), h_ref[...], preferred_element_type=F32)
    base = half * (RUN_SLOTS * ROW_SLABS)
    for c in range(ROW_SLABS):
        stage[pl.ds(base + c, RUN_SLOTS, stride=ROW_SLABS), :] = rows[:, c * LANES:(c + 1) * LANES]

    @pl.when(i > 0)
    def _():
        _for_each_run_chunk(meta_prev_ref, pstart_ref, lambda s, r: run_copy(1 - half, 0, 0).wait())

    _for_each_run_chunk(meta_ref, pstart_ref, lambda s, r: run_copy(half, s, r).start())

    @pl.when(i == pl.num_programs(0) - 1)
    def _():
        _for_each_run_chunk(meta_ref, pstart_ref, lambda s, r: run_copy(half, 0, 0).wait())


def _expert_kernel(blk_e_ref, nact_ref, xr_ref, wgu_ref, bgu_ref, wdn_ref, bdn_ref, yr_ref,
                   wgu_bf, wdn_bf):
    r = EXPERT_ROWS
    i = pl.program_id(0)
    active = i < nact_ref[0]
    new_expert = jnp.logical_or(i == 0, blk_e_ref[i] != blk_e_ref[jnp.maximum(i - 1, 0)])

    @pl.when(jnp.logical_and(active, new_expert))
    def _():
        def cast_rows(c, carry):
            rows = pl.ds(pl.multiple_of(c * LANES, LANES), LANES)
            wgu_bf[rows, :] = wgu_ref[0, rows, :].astype(BF16)
            wdn_bf[rows, :] = wdn_ref[0, rows, :].astype(BF16)
            return carry

        lax.fori_loop(0, D_MODEL // LANES, cast_rows, 0)

    @pl.when(active)
    def _():
        x = jnp.concatenate(
            [xr_ref[pl.ds(c, r, stride=ROW_SLABS), :] for c in range(ROW_SLABS)], axis=1)
        gu = jnp.dot(x.astype(BF16), wgu_bf[...], preferred_element_type=F32) + bgu_ref[0]
        g = jnp.minimum(gu[:, :D_EXPERT], SWIGLU_LIMIT)
        u = jnp.clip(gu[:, D_EXPERT:], -SWIGLU_LIMIT, SWIGLU_LIMIT)
        act = (u + 1.0) * (g * jax.nn.sigmoid(SWIGLU_ALPHA * g))
        y = jnp.dot(act.astype(BF16), wdn_bf[...], preferred_element_type=F32) + bdn_ref[0]
        for c in range(ROW_SLABS):
            yr_ref[pl.ds(c, r, stride=ROW_SLABS), :] = y[:, c * LANES:(c + 1) * LANES]

    @pl.when(jnp.logical_not(active))
    def _():
        yr_ref[...] = jnp.zeros_like(yr_ref)


def _final_kernel(pstart_ref, meta_ref, meta_next_ref, x1_ref, route_ref, p_ref, gple_ref,
                  wpg_ref, wpp_ref, yr_hbm, o_ref, gstage, sems):
    t = TOKEN_TILE
    i = pl.program_id(0)
    half = lax.rem(i, 2)

    def run_copy(buf_half, slot, row):
        dst = gstage.at[_slab_rows(buf_half * RUN_SLOTS + slot, RUN_CHUNK), :]
        return pltpu.make_async_copy(yr_hbm.at[_slab_rows(row, RUN_CHUNK), :], dst, sems.at[buf_half])

    @pl.when(i == 0)
    def _():
        gstage[...] = jnp.zeros_like(gstage)
        _for_each_run_chunk(meta_ref, pstart_ref, lambda s, r: run_copy(0, s, r).start())

    @pl.when(i + 1 < pl.num_programs(0))
    def _():
        _for_each_run_chunk(meta_next_ref, pstart_ref, lambda s, r: run_copy(1 - half, s, r).start())

    _for_each_run_chunk(meta_ref, pstart_ref, lambda s, r: run_copy(half, 0, 0).wait())

    base = half * (RUN_SLOTS * ROW_SLABS)
    rows = jnp.concatenate(
        [gstage[pl.ds(base + c, RUN_SLOTS, stride=ROW_SLABS), :] for c in range(ROW_SLABS)], axis=1)
    route = route_ref[...]
    s_i = lax.broadcasted_iota(jnp.int32, (t, RUN_SLOTS), 1).astype(F32)
    wmat = jnp.zeros((t, RUN_SLOTS), F32)
    for k in range(TOP_K):
        wmat = jnp.where(s_i == route[:, TOP_K + k:TOP_K + k + 1], route[:, k:k + 1], wmat)
    rows_hi = rows.astype(BF16)
    rows_lo = (rows - rows_hi.astype(F32)).astype(BF16)
    w_hi = wmat.astype(BF16)
    w_lo = (wmat - w_hi.astype(F32)).astype(BF16)
    y = (jnp.dot(w_hi, rows_hi, preferred_element_type=F32)
         + jnp.dot(w_lo, rows_hi, preferred_element_type=F32)
         + jnp.dot(w_hi, rows_lo, preferred_element_type=F32))
    x2 = x1_ref[...] + y
    hp = _rms(x2, gple_ref[...]).astype(BF16)
    gate = jax.nn.sigmoid(jnp.dot(hp, wpg_ref[...], preferred_element_type=F32))
    emb = jnp.dot(p_ref[...].astype(BF16), wpp_ref[...], preferred_element_type=F32)
    o_ref[...] = x2 + gate * emb


def _rope_tables(s, half, x1_starts, x2_starts):
    inv_freq = ROPE_THETA ** (-(np.arange(half, dtype=np.float64) / half))
    ang = np.arange(s, dtype=np.float64)[:, None] * inv_freq[None, :]
    cos, sin = np.cos(ang), np.sin(ang)
    cos_t, sin_t = np.ones((s, LANES), np.float32), np.zeros((s, LANES), np.float32)
    for st in x1_starts:
        cos_t[:, st:st + half] = cos
        sin_t[:, st:st + half] = -sin
    for st in x2_starts:
        cos_t[:, st:st + half] = cos
        sin_t[:, st:st + half] = sin
    return jnp.asarray(cos_t), jnp.asarray(sin_t)


_MOBA_LANE_COLS = tuple(list(range(0, 32)) + list(range(64, 96)) + list(range(32, 64))
                        + list(range(96, 128)))
_MLA_LANE_DIMS = tuple(list(range(80, 96)) + list(range(0, 48)) + list(range(64, 80))
                       + list(range(48, 64)) + [MLA_QK_DIM] * 32)


def _moba_lanes(w):
    k, width = w.shape
    cols = jnp.asarray(_MOBA_LANE_COLS, jnp.int32)
    return w.reshape(k, width // LANES, LANES)[:, :, cols].reshape(k, width)


def _mla_lanes(w, heads):
    k = w.shape[0]
    w = jnp.pad(w.reshape(k, heads, MLA_QK_DIM), ((0, 0), (0, 0), (0, 1)))
    return w[:, :, jnp.asarray(_MLA_LANE_DIMS, jnp.int32)].reshape(k, heads * LANES)


def _row(v):
    return v.reshape(1, -1).astype(F32)


def _layer(x, p_i, g_mix, w_in, moba_q_norm, moba_k_norm, mla_q_lat_norm, w_uq, mla_kv_lat_norm,
           w_ukv, mla_q_norm, mla_k_norm, w_branch_a, w_branch_b, w_out, g_ffn, w_router, b_router,
           w_gate_up, b_gate_up, w_down, b_down, g_ple, w_ple_gate, w_ple_proj):
    b, s, d = x.shape
    n = b * s
    assert d == D_MODEL and s % ATTN_TILE == 0
    assert s // MOBA_BLOCK <= GATE_ROWS and TOKEN_TILE == ATTN_TILE == MOBA_BLOCK
    assert D_EXPERT == D_MODEL and RUN_CHUNK == SUBLANES
    n_tiles = n // TOKEN_TILE
    tiles_per_seq = s // TOKEN_TILE
    xf = x.reshape(n, d)

    off = [0]
    for wdt in (MOBA_WIDTH, MOBA_WIDTH, MOBA_WIDTH, MLA_Q_LORA, MLA_KV_LORA, MLA_ROPE_DIM, D_MODEL, D_MODEL):
        off.append(off[-1] + wdt)
    seg = [w_in[:, off[i]:off[i + 1]] for i in range(8)]
    kpe_cols = _mla_lanes(jnp.pad(seg[5], ((0, 0), (MLA_NOPE_DIM, 0))), 1)
    w_in_p = jnp.concatenate([_moba_lanes(seg[0]), _moba_lanes(seg[1])] + seg[2:5] + [kpe_cols]
                             + seg[6:], axis=1).astype(BF16)
    assert w_in_p.shape[1] == D_IN_PACKED
    w_uq_p = _mla_lanes(w_uq, MLA_HEADS).astype(BF16)
    w_ukv_h = w_ukv.reshape(MLA_KV_LORA, MLA_HEADS, MLA_NOPE_DIM + MLA_V_DIM)
    w_uk_p = _mla_lanes(jnp.pad(w_ukv_h[:, :, :MLA_NOPE_DIM], ((0, 0), (0, 0), (0, MLA_ROPE_DIM)))
                        .reshape(MLA_KV_LORA, -1), MLA_HEADS).astype(BF16)
    w_uv = w_ukv_h[:, :, MLA_NOPE_DIM:].reshape(MLA_KV_LORA, MLA_WIDTH).astype(BF16)
    gqa = _moba_lanes(_row(jnp.tile(moba_q_norm, 2))) * (MOBA_HEAD_DIM ** -0.5)
    gka = _moba_lanes(_row(jnp.tile(moba_k_norm, 2)))
    gqb = _mla_lanes(_row(mla_q_norm), 1) * (MLA_QK_DIM ** -0.5)
    gkb = _mla_lanes(_row(mla_k_norm), 1)
    half_a, half_b = MOBA_HEAD_DIM // 2, MLA_ROPE_DIM // 2
    cosa, sina = _rope_tables(s, half_a, (0, half_a), (LANES // 2, LANES // 2 + half_a))
    cosb, sinb = _rope_tables(s, half_b, (LANES // 2,), (0,))

    tok = lambda width: pl.BlockSpec((TOKEN_TILE, width), lambda i: (i, 0))
    whole = lambda arr: pl.BlockSpec(arr.shape, lambda i: (0,) * arr.ndim)
    seq_tab = pl.BlockSpec((TOKEN_TILE, LANES), lambda i: (i % tiles_per_seq, 0))
    vt_spec = pl.BlockSpec((1, MOBA_WIDTH, TOKEN_TILE), lambda i: (i, 0, 0))
    params = pltpu.CompilerParams(dimension_semantics=("arbitrary",), vmem_limit_bytes=VMEM_LIMIT)

    consts1 = [_row(g_mix), w_in_p, gqa, gka]
    consts2 = [_row(mla_q_lat_norm), w_uq_p, _row(mla_kv_lat_norm), w_uk_p, w_uv, gqb, gkb]
    qa, ka, va, kmean, qb, kb, vb, ga, gb = pl.pallas_call(
        _inproj_kernel,
        grid=(n_tiles,),
        in_specs=([tok(d)] + [whole(a) for a in consts1] + [seq_tab] * 2
                  + [whole(a) for a in consts2] + [seq_tab] * 2),
        out_specs=[tok(MOBA_WIDTH), tok(MOBA_WIDTH), vt_spec,
                   pl.BlockSpec((1, 1, MOBA_WIDTH), lambda i: (i, 0, 0)),
                   tok(MLA_HEADS * LANES), tok(MLA_HEADS * LANES), vt_spec,
                   tok(d), tok(d)],
        out_shape=[jax.ShapeDtypeStruct((n, MOBA_WIDTH), BF16)] * 2
        + [jax.ShapeDtypeStruct((n_tiles, MOBA_WIDTH, TOKEN_TILE), BF16)]
        + [jax.ShapeDtypeStruct((n_tiles, 1, MOBA_WIDTH), F32)]
        + [jax.ShapeDtypeStruct((n, MLA_HEADS * LANES), BF16)] * 2
        + [jax.ShapeDtypeStruct((n_tiles, MLA_WIDTH, TOKEN_TILE), BF16)]
        + [jax.ShapeDtypeStruct((n, d), BF16)] * 2,
        compiler_params=params,
        name="in_projection",
    )(xf, *consts1, cosa, sina, *consts2, cosb, sinb)

    kmean = kmean.reshape(b, tiles_per_seq, MOBA_WIDTH)
    kmean = jnp.pad(kmean, ((0, 0), (0, GATE_ROWS - tiles_per_seq), (0, 0))).astype(BF16)

    r3 = lambda a: a.reshape(b, s, a.shape[-1])
    r4 = lambda a: a.reshape(b, tiles_per_seq, a.shape[1], TOKEN_TILE)
    ya = _attention(r3(qa), r3(ka), r4(va), kmean, moba=True).reshape(n, MOBA_WIDTH)
    yb = _attention(r3(qb), r3(kb), r4(vb), None, moba=False).reshape(n, MLA_WIDTH)

    wr_p = jnp.pad(w_router, ((0, 0), (0, LANES - N_EXPERTS))).astype(BF16)
    br_p = jnp.pad(b_router.astype(F32), (0, LANES - N_EXPERTS), constant_values=NEG).reshape(1, LANES)
    consts3 = [w_branch_a.astype(BF16), w_branch_b.astype(BF16), w_out.astype(BF16), _row(g_ffn),
               wr_p, br_p]
    x1, h_ffn, route, meta, cnt = pl.pallas_call(
        _merge_kernel,
        grid=(n_tiles,),
        in_specs=[tok(d), tok(MOBA_WIDTH), tok(MLA_WIDTH), tok(d), tok(d)]
        + [whole(a) for a in consts3],
        out_specs=[tok(d), tok(d), tok(LANES),
                   pl.BlockSpec((1, SUBLANES, LANES), lambda i: (i, 0, 0)),
                   pl.BlockSpec((SUBLANES, LANES), lambda i: (0, 0))],
        out_shape=[jax.ShapeDtypeStruct((n, d), F32),
                   jax.ShapeDtypeStruct((n, d), BF16),
                   jax.ShapeDtypeStruct((n, LANES), F32),
                   jax.ShapeDtypeStruct((n_tiles, SUBLANES, LANES), F32),
                   jax.ShapeDtypeStruct((SUBLANES, LANES), F32)],
        scratch_shapes=[pltpu.VMEM((SUBLANES, LANES), F32)],
        compiler_params=params,
        name="merge_router",
    )(xf, ya, yb, ga, gb, *consts3)

    rb = EXPERT_ROWS
    n_blocks = -(-(n * TOP_K + N_EXPERTS * (RUN_CHUNK - 1)) // rb) + N_EXPERTS
    n_rows = n_blocks * rb
    counts = cnt[0, :N_EXPERTS].astype(jnp.int32)
    pcounts = ((counts + (RUN_CHUNK - 1) + rb - 1) // rb) * rb
    pends = jnp.cumsum(pcounts).astype(jnp.int32)
    pstarts = (pends - pcounts).astype(jnp.int32)
    nact = (pends[-1] // rb).astype(jnp.int32).reshape(1)
    blk = jnp.minimum(jnp.arange(n_blocks, dtype=jnp.int32), nact[0] - 1)
    blk_e = jnp.sum((pends[None, :] <= (blk * rb)[:, None]).astype(jnp.int32), axis=1)
    blk_e = jnp.minimum(blk_e, N_EXPERTS - 1)
    meta_i = meta[:, :4, :].astype(jnp.int32).reshape(n_tiles * 4 * LANES)
    meta_spec = lambda shift: pl.BlockSpec(
        (4 * LANES,), lambda i, *_: (jnp.clip(i + shift, 0, n_tiles - 1),), memory_space=pltpu.SMEM)

    xr = pl.pallas_call(
        _dispatch_kernel,
        grid_spec=pltpu.PrefetchScalarGridSpec(
            num_scalar_prefetch=3,
            grid=(n_tiles,),
            in_specs=[meta_spec(0), meta_spec(-1),
                      pl.BlockSpec((TOKEN_TILE, LANES), lambda i, *_: (i, 0)),
                      pl.BlockSpec((TOKEN_TILE, d), lambda i, *_: (i, 0))],
            out_specs=pl.BlockSpec(memory_space=pl.ANY),
            scratch_shapes=[pltpu.VMEM((2 * RUN_SLOTS * ROW_SLABS, LANES), F32),
                            pltpu.VMEM((EXPERT_ROWS * ROW_SLABS, LANES), F32),
                            pltpu.SemaphoreType.DMA((2,)), pltpu.SemaphoreType.DMA(())]),
        out_shape=jax.ShapeDtypeStruct((n_rows * ROW_SLABS, LANES), F32),
        compiler_params=params,
        name="dispatch_rows",
    )(pstarts, pends, nact, meta_i, meta_i, route, h_ffn)

    act_blk = lambda i, be, na: jnp.minimum(i, na[0] - 1)
    yr = pl.pallas_call(
        _expert_kernel,
        grid_spec=pltpu.PrefetchScalarGridSpec(
            num_scalar_prefetch=2,
            grid=(n_blocks,),
            in_specs=[
                pl.BlockSpec((rb * ROW_SLABS, LANES), lambda i, be, na: (act_blk(i, be, na), 0)),
                pl.BlockSpec((1, d, 2 * D_EXPERT), lambda i, be, na: (be[i], 0, 0)),
                pl.BlockSpec((1, 1, 2 * D_EXPERT), lambda i, be, na: (be[i], 0, 0)),
                pl.BlockSpec((1, D_EXPERT, d), lambda i, be, na: (be[i], 0, 0)),
                pl.BlockSpec((1, 1, d), lambda i, be, na: (be[i], 0, 0)),
            ],
            out_specs=pl.BlockSpec((rb * ROW_SLABS, LANES), lambda i, be, na: (i, 0)),
            scratch_shapes=[pltpu.VMEM((d, 2 * D_EXPERT), BF16), pltpu.VMEM((D_EXPERT, d), BF16)]),
        out_shape=jax.ShapeDtypeStruct((n_rows * ROW_SLABS, LANES), F32),
        compiler_params=params,
        name="experts",
    )(blk_e, nact, xr, w_gate_up.astype(F32),
      b_gate_up.reshape(N_EXPERTS, 1, -1).astype(F32), w_down.astype(F32),
      b_down.reshape(N_EXPERTS, 1, -1).astype(F32))

    ftok = lambda width: pl.BlockSpec((TOKEN_TILE, width), lambda i, ps: (i, 0))
    fwhole = lambda arr: pl.BlockSpec(arr.shape, lambda i, ps: (0,) * arr.ndim)
    consts4 = [_row(g_ple), w_ple_gate.astype(BF16), w_ple_proj.astype(BF16)]
    out = pl.pallas_call(
        _final_kernel,
        grid_spec=pltpu.PrefetchScalarGridSpec(
            num_scalar_prefetch=1,
            grid=(n_tiles,),
            in_specs=[meta_spec(0), meta_spec(1), ftok(d), ftok(LANES), ftok(PLE_DIM)]
            + [fwhole(a) for a in consts4] + [pl.BlockSpec(memory_space=pl.ANY)],
            out_specs=ftok(d),
            scratch_shapes=[pltpu.VMEM((2 * RUN_SLOTS * ROW_SLABS, LANES), F32),
                            pltpu.SemaphoreType.DMA((2,))]),
        out_shape=jax.ShapeDtypeStruct((n, d), F32),
        compiler_params=params,
        name="combine_ple",
    )(pstarts, meta_i, meta_i, x1, route, p_i.reshape(n, PLE_DIM), *consts4, yr)
    return out.reshape(b, s, d)


def kernel(x, p, g_mix, w_in, moba_q_norm, moba_k_norm, mla_q_lat_norm, w_uq, mla_kv_lat_norm, w_ukv, mla_q_norm, mla_k_norm, w_branch_a, w_branch_b, w_out, g_ffn, w_router, b_router, w_gate_up, b_gate_up, w_down, b_down, g_ple, w_ple_gate, w_ple_proj):
    for i in range(p.shape[0]):
        x = _layer(x, p[i], g_mix[i], w_in[i], moba_q_norm[i], moba_k_norm[i], mla_q_lat_norm[i],
                   w_uq[i], mla_kv_lat_norm[i], w_ukv[i], mla_q_norm[i], mla_k_norm[i],
                   w_branch_a[i], w_branch_b[i], w_out[i], g_ffn[i], w_router[i], b_router[i],
                   w_gate_up[i], b_gate_up[i], w_down[i], b_down[i], g_ple[i], w_ple_gate[i],
                   w_ple_proj[i])
    return x
```

```python
import functools

import numpy as np
import jax
import jax.numpy as jnp
from jax import lax
from jax.experimental import pallas as pl
from jax.experimental.pallas import tpu as pltpu

F32 = jnp.float32
BF16 = jnp.bfloat16

D_MODEL = 1024
PLE_DIM = 256
EPS = 1e-6
ROPE_THETA = 10000.0
MOBA_HEADS = 8
MOBA_HEAD_DIM = 64
MOBA_BLOCK = 256
MOBA_TOPK = 3
MOBA_WIDTH = MOBA_HEADS * MOBA_HEAD_DIM
MLA_HEADS = 8
MLA_Q_LORA = 256
MLA_KV_LORA = 128
MLA_NOPE_DIM = 64
MLA_ROPE_DIM = 32
MLA_V_DIM = 64
MLA_QK_DIM = MLA_NOPE_DIM + MLA_ROPE_DIM
MLA_WIDTH = MLA_HEADS * MLA_V_DIM
N_EXPERTS = 32
TOP_K = 4
D_EXPERT = 1024
SWIGLU_LIMIT = 7.0
SWIGLU_ALPHA = 1.702

LANES = 128
SUBLANES = 8
ROW_SLABS = D_MODEL // LANES
VMEM_LIMIT = 56 * 1024 * 1024

TOKEN_TILE = 256
ATTN_TILE = 256
ATTN_GROUPS = 4
EXPERT_ROWS = 256
RUN_CHUNK = 8
RUN_SLOTS = -(-(TOKEN_TILE * TOP_K + N_EXPERTS * (RUN_CHUNK - 1)) // 256) * 256

NEG = -1e30
MASK_BIAS = -1e9

C_QA, C_KA, C_VA = 0, 512, 1024
C_CQ, C_CKV, C_KPE = 1536, 1792, 1920
C_GA, C_GB = 2048, 3072
D_IN_PACKED = 4096


def _rms(x, gain):
    return x * lax.rsqrt(jnp.mean(x * x, axis=-1, keepdims=True) + EPS) * gain


def _rope(t, cos, sin):
    return t * cos + pltpu.roll(t, LANES // 2, 1) * sin


def _moba_even_head(lane):
    return (lane & (MOBA_HEAD_DIM // 2)) == 0


def _inproj_kernel(x_ref, gmix_ref, win_ref, gqa_ref, gka_ref, cosa_ref, sina_ref,
                   gql_ref, wuq_ref, gkvl_ref, wuk_ref, wuv_ref, gqb_ref, gkb_ref,
                   cosb_ref, sinb_ref,
                   qa_ref, ka_ref, vat_ref, kmean_ref, qb_ref, kb_ref, vbt_ref, ga_ref, gb_ref):
    hn = _rms(x_ref[...], gmix_ref[...]).astype(BF16)

    def proj(c0, width):
        return jnp.dot(hn, win_ref[:, c0:c0 + width], preferred_element_type=F32)

    first = _moba_even_head(lax.broadcasted_iota(jnp.int32, (TOKEN_TILE, LANES), 1))
    cosa, sina = cosa_ref[...], sina_ref[...]

    def moba_norm_rope(t, gain):
        sq = t * t
        ss0 = jnp.sum(jnp.where(first, sq, 0.0), axis=-1, keepdims=True)
        ss1 = jnp.sum(jnp.where(first, 0.0, sq), axis=-1, keepdims=True)
        ms = jnp.where(first, ss0, ss1) * (1.0 / MOBA_HEAD_DIM)
        t = t * lax.rsqrt(ms + EPS) * gain
        return _rope(t, cosa, sina)

    qa = proj(C_QA, MOBA_WIDTH)
    ka = proj(C_KA, MOBA_WIDTH)
    for c in range(MOBA_WIDTH // LANES):
        sl = slice(c * LANES, (c + 1) * LANES)
        qa_ref[:, sl] = moba_norm_rope(qa[:, sl], gqa_ref[...]).astype(BF16)
        kc = moba_norm_rope(ka[:, sl], gka_ref[...])
        ka_ref[:, sl] = kc.astype(BF16)
        kmean_ref[0, :, sl] = jnp.mean(kc, axis=0, keepdims=True)
    vat_ref[0] = proj(C_VA, MOBA_WIDTH).T.astype(BF16)

    cosb, sinb = cosb_ref[...], sinb_ref[...]

    def mla_norm_rope(t, gain):
        ms = jnp.sum(t * t, axis=-1, keepdims=True) * (1.0 / MLA_QK_DIM)
        t = t * lax.rsqrt(ms + EPS) * gain
        return _rope(t, cosb, sinb)

    cq = _rms(proj(C_CQ, MLA_Q_LORA), gql_ref[...]).astype(BF16)
    qb = jnp.dot(cq, wuq_ref[...], preferred_element_type=F32)
    ckv = _rms(proj(C_CKV, MLA_KV_LORA), gkvl_ref[...]).astype(BF16)
    kn = jnp.dot(ckv, wuk_ref[...], preferred_element_type=F32)
    kpe = proj(C_KPE, LANES)
    for h in range(MLA_HEADS):
        sl = slice(h * LANES, (h + 1) * LANES)
        qb_ref[:, sl] = mla_norm_rope(qb[:, sl], gqb_ref[...]).astype(BF16)
        kb_ref[:, sl] = mla_norm_rope(kn[:, sl] + kpe, gkb_ref[...]).astype(BF16)
    vbt_ref[0] = jnp.dot(ckv, wuv_ref[...], preferred_element_type=F32).T.astype(BF16)

    ga_ref[...] = jax.nn.sigmoid(proj(C_GA, D_MODEL)).astype(BF16)
    gb_ref[...] = jax.nn.sigmoid(proj(C_GB, D_MODEL)).astype(BF16)


_NT = (((1,), (1,)), ((), ()))


GATE_ROWS = 16


def _attn_kernel(*refs, moba):
    if moba:
        q_ref, k_ref, vt_ref, kmean_ref, o_ref = refs
    else:
        q_ref, k_ref, vt_ref, o_ref = refs
    t = ATTN_TILE
    hd = MOBA_HEAD_DIM
    n_heads = 2 * ATTN_GROUPS
    qi = pl.program_id(2)
    key_i = lax.broadcasted_iota(jnp.int32, (t, t), 0)
    qry_i = lax.broadcasted_iota(jnp.int32, (t, t), 1)

    blk = lax.broadcasted_iota(jnp.int32, (GATE_ROWS, t), 0)
    heads, biases = [], []
    for hh in range(n_heads):
        if moba:
            lane = lax.broadcasted_iota(jnp.int32, (t, LANES), 1)
            even = _moba_even_head(lane)
            head_lanes = even if hh % 2 == 0 else jnp.logical_not(even)
            kcols = slice((hh // 2) * LANES, (hh // 2 + 1) * LANES)
            q = jnp.where(head_lanes, q_ref[0, :, kcols], jnp.zeros((), BF16))
            gate = lax.dot_general(kmean_ref[0, :, kcols], q, _NT, preferred_element_type=F32)
            g = jnp.where(blk < qi, gate, -jnp.inf)
            keep = jnp.zeros((GATE_ROWS, t), F32)
            for _ in range(MOBA_TOPK):
                gmax = jnp.max(g, axis=0, keepdims=True)
                pick = jnp.min(jnp.where(g == gmax, blk, GATE_ROWS), axis=0, keepdims=True)
                hit = blk == jnp.where(gmax > -jnp.inf, pick, GATE_ROWS)
                keep = jnp.where(hit, 1.0, keep)
                g = jnp.where(hit, -jnp.inf, g)
            biases.append(jnp.where(keep > 0.0, 0.0, MASK_BIAS))
        else:
            kcols = slice(hh * LANES, (hh + 1) * LANES)
            q = q_ref[0, :, kcols]
        heads.append((q, kcols))

    def update(s, vt_blk, state):
        m_prev, l_prev, acc = state
        m_new = jnp.maximum(m_prev, jnp.max(s, axis=0, keepdims=True))
        alpha = jnp.exp(m_prev - m_new)
        p = jnp.exp(s - m_new)
        l_new = alpha * l_prev + jnp.sum(p, axis=0, keepdims=True)
        acc = alpha * acc + jnp.dot(vt_blk, p.astype(BF16), preferred_element_type=F32)
        return m_new, l_new, acc

    def past_block(j, states):
        start = pl.multiple_of(j * t, t)
        scores = [lax.dot_general(k_ref[0, pl.ds(start, t), kcols], q, _NT,
                                  preferred_element_type=F32) for q, kcols in heads]
        out = []
        for hh, s in enumerate(scores):
            if moba:
                s = jnp.sum(jnp.where(blk == j, biases[hh], 0.0), axis=0, keepdims=True) + s
            out.append(update(s, vt_ref[0, j, hh * hd:(hh + 1) * hd, :], states[hh]))
        return tuple(out)

    init = (jnp.full((1, t), NEG, F32), jnp.zeros((1, t), F32), jnp.zeros((hd, t), F32))
    states = lax.fori_loop(0, qi, past_block, (init,) * n_heads)

    diag_start = pl.multiple_of(qi * t, t)
    scores = [lax.dot_general(k_ref[0, pl.ds(diag_start, t), kcols], q, _NT,
                              preferred_element_type=F32) for q, kcols in heads]
    outs = []
    for hh, s in enumerate(scores):
        s = jnp.where(key_i <= qry_i, s, NEG)
        _, l_fin, acc = update(s, vt_ref[0, qi, hh * hd:(hh + 1) * hd, :], states[hh])
        outs.append(acc / l_fin)
    o_ref[0] = jnp.concatenate(outs, axis=0).T.astype(BF16)


def _attention(q, k, vt, kmean, *, moba):
    b, s, _ = q.shape
    v_cols = ATTN_GROUPS * LANES
    steps = vt.shape[2] // v_cols
    nblk = s // ATTN_TILE
    qk_cols = v_cols if moba else 2 * v_cols
    in_specs = [
        pl.BlockSpec((1, ATTN_TILE, qk_cols), lambda bi, gi, qi: (bi, qi, gi)),
        pl.BlockSpec((1, s, qk_cols), lambda bi, gi, qi: (bi, 0, gi)),
        pl.BlockSpec((1, nblk, v_cols, ATTN_TILE), lambda bi, gi, qi: (bi, 0, gi, 0)),
    ]
    args = [q, k, vt]
    if moba:
        in_specs.append(pl.BlockSpec((1, GATE_ROWS, v_cols), lambda bi, gi, qi: (bi, 0, gi)))
        args.append(kmean)
    return pl.pallas_call(
        functools.partial(_attn_kernel, moba=moba),
        grid=(b, steps, nblk),
        in_specs=in_specs,
        out_specs=pl.BlockSpec((1, ATTN_TILE, v_cols), lambda bi, gi, qi: (bi, qi, gi)),
        out_shape=jax.ShapeDtypeStruct((b, s, steps * v_cols), BF16),
        compiler_params=pltpu.CompilerParams(
            dimension_semantics=("arbitrary", "arbitrary", "arbitrary"),
            vmem_limit_bytes=VMEM_LIMIT),
        name="moba_attention" if moba else "mla_attention",
    )(*args)


def _merge_kernel(x_ref, ya_ref, yb_ref, ga_ref, gb_ref, wa_ref, wb_ref, wo_ref, gffn_ref,
                  wr_ref, br_ref, x1_ref, h_ref, route_ref, meta_ref, cnt_ref, run_scr):
    t = TOKEN_TILE

    @pl.when(pl.program_id(0) == 0)
    def _():
        run_scr[...] = jnp.zeros_like(run_scr)

    merged = (ga_ref[...].astype(F32) * jnp.dot(ya_ref[...], wa_ref[...], preferred_element_type=F32)
              + gb_ref[...].astype(F32) * jnp.dot(yb_ref[...], wb_ref[...], preferred_element_type=F32))
    x1 = x_ref[...] + jnp.dot(merged.astype(BF16), wo_ref[...], preferred_element_type=F32)
    x1_ref[...] = x1
    h = _rms(x1, gffn_ref[...]).astype(BF16)
    h_ref[...] = h

    logits = jnp.dot(h, wr_ref[...], preferred_element_type=F32) + br_ref[...]
    lane = lax.broadcasted_iota(jnp.int32, (t, LANES), 1)
    lg = logits
    hits = []
    top = None
    for r in range(TOP_K):
        gmax = jnp.max(lg, axis=-1, keepdims=True)
        pick = jnp.min(jnp.where(lg == gmax, lane, LANES), axis=-1, keepdims=True)
        hit = lane == pick
        if r == 0:
            top = gmax
        hits.append(hit)
        lg = jnp.where(hit, -jnp.inf, lg)
    sel = jnp.where(lg == -jnp.inf, 1.0, 0.0)
    wgt = sel * jnp.exp(logits - top)
    wgt = wgt / jnp.sum(wgt, axis=-1, keepdims=True)

    r_i = lax.broadcasted_iota(jnp.int32, (t, t), 0)
    c_i = lax.broadcasted_iota(jnp.int32, (t, t), 1)
    lower = jnp.where(c_i < r_i, 1.0, 0.0).astype(BF16)
    rank_in_tile = jnp.dot(lower, sel.astype(BF16), preferred_element_type=F32)
    tcnt = jnp.sum(sel, axis=0, keepdims=True)
    tpad = jnp.floor((tcnt + (RUN_CHUNK - 1)) * (1.0 / RUN_CHUNK)) * RUN_CHUNK
    e_r = lax.broadcasted_iota(jnp.int32, (LANES, LANES), 0)
    e_c = lax.broadcasted_iota(jnp.int32, (LANES, LANES), 1)
    before = jnp.where(e_r < e_c, 1.0, 0.0).astype(BF16)
    tbase = jnp.dot(jnp.broadcast_to(tpad, (SUBLANES, LANES)).astype(BF16), before,
                    preferred_element_type=F32)[0:1, :]
    run = run_scr[0:1, :]
    run_new = run + tcnt
    run_scr[...] = jnp.broadcast_to(run_new, run_scr.shape)
    cnt_ref[...] = jnp.broadcast_to(run_new, cnt_ref.shape)
    row = lax.broadcasted_iota(jnp.int32, (SUBLANES, LANES), 0)
    meta_ref[0] = jnp.where(row == 0, tcnt, jnp.where(row == 1, run, jnp.where(row == 2, tbase, 0.0)))

    slot_of = rank_in_tile + tbase
    route = jnp.zeros((t, LANES), F32)
    for r in range(TOP_K):
        w_r = jnp.sum(jnp.where(hits[r], wgt, 0.0), axis=-1, keepdims=True)
        slot_r = jnp.sum(jnp.where(hits[r], slot_of, 0.0), axis=-1, keepdims=True)
        route = jnp.where(lane == r, w_r, route)
        route = jnp.where(lane == TOP_K + r, slot_r, route)
    route_ref[...] = route


def _for_each_run_chunk(meta_ref, pstart_ref, fn):
    def per_expert(e, carry):
        n_chunks = lax.shift_right_logical(meta_ref[e] + (RUN_CHUNK - 1), RUN_CHUNK.bit_length() - 1)
        slot0 = meta_ref[2 * LANES + e]
        row0 = pstart_ref[e] + meta_ref[LANES + e]

        def per_chunk(c, cc):
            fn(slot0 + c * RUN_CHUNK, row0 + c * RUN_CHUNK)
            return cc

        lax.fori_loop(0, n_chunks, per_chunk, 0)
        return carry

    lax.fori_loop(0, N_EXPERTS, per_expert, 0)


def _slab_rows(first_row, n_rows):
    return pl.ds(pl.multiple_of(first_row * ROW_SLABS, ROW_SLABS), n_rows * ROW_SLABS)


def _dispatch_kernel(pstart_ref, pend_ref, nact_ref, meta_ref, meta_prev_ref, route_ref, h_ref,
                     xr_hbm, stage, zero_scr, sems, zsem):
    t = TOKEN_TILE
    i = pl.program_id(0)
    half = lax.rem(i, 2)

    def run_copy(buf_half, slot, row):
        src = stage.at[_slab_rows(buf_half * RUN_SLOTS + slot, RUN_CHUNK), :]
        return pltpu.make_async_copy(src, xr_hbm.at[_slab_rows(row, RUN_CHUNK), :], sems.at[buf_half])

    @pl.when(i == 0)
    def _():
        zero_scr[...] = jnp.zeros_like(zero_scr)
        n_blocks = xr_hbm.shape[0] // (EXPERT_ROWS * ROW_SLABS)

        def block_copy(first_row):
            return pltpu.make_async_copy(zero_scr, xr_hbm.at[_slab_rows(first_row, EXPERT_ROWS), :], zsem)

        def zero_last(e, c):
            block_copy(pend_ref[e] - EXPERT_ROWS).start()
            return c

        def zero_tail(blk, c):
            block_copy(blk * EXPERT_ROWS).start()
            return c

        lax.fori_loop(0, N_EXPERTS, zero_last, 0)
        lax.fori_loop(nact_ref[0], n_blocks, zero_tail, 0)
        lax.fori_loop(0, N_EXPERTS + n_blocks - nact_ref[0], lambda r, c: (block_copy(0).wait(), c)[1], 0)

    slot_rows = route_ref[...].T[TOP_K:2 * TOP_K, :]
    s_i = lax.broadcasted_iota(jnp.int32, (RUN_SLOTS, t), 0).astype(F32)
    pick = jnp.zeros((RUN_SLOTS, t), F32)
    for k in range(TOP_K):
        pick = jnp.where(s_i == slot_rows[k:k + 1, :], 1.0, pick)
    rows = jnp.dot(pick.astype(BF16), h_ref[...], preferred_element_type=F32)
    base = half * (RUN_SLOTS * ROW_SLABS)
    for c in range(ROW_SLABS):
        stage[pl.ds(base + c, RUN_SLOTS, stride=ROW_SLABS), :] = rows[:, c * LANES:(c + 1) * LANES]

    @pl.when(i > 0)
    def _():
        _for_each_run_chunk(meta_prev_ref, pstart_ref, lambda s, r: run_copy(1 - half, 0, 0).wait())

    _for_each_run_chunk(meta_ref, pstart_ref, lambda s, r: run_copy(half, s, r).start())

    @pl.when(i == pl.num_programs(0) - 1)
    def _():
        _for_each_run_chunk(meta_ref, pstart_ref, lambda s, r: run_copy(half, 0, 0).wait())


def _expert_kernel(blk_e_ref, nact_ref, xr_ref, wgu_hbm, bgu_ref, wdn_hbm, bdn_ref, yr_ref,
                   wgu_f32, wdn_f32, wgu_bf, wdn_bf, sems):
    r = EXPERT_ROWS
    i = pl.program_id(0)
    e = blk_e_ref[i]
    active = i < nact_ref[0]
    new_expert = jnp.logical_or(i == 0, e != blk_e_ref[jnp.maximum(i - 1, 0)])

    def weight_copies(expert):
        half = lax.rem(expert, 2)
        return (pltpu.make_async_copy(wgu_hbm.at[expert], wgu_f32.at[half], sems.at[0, half]),
                pltpu.make_async_copy(wdn_hbm.at[expert], wdn_f32.at[half], sems.at[1, half]))

    @pl.when(i == 0)
    def _():
        for cp in weight_copies(e):
            cp.start()

    @pl.when(jnp.logical_and(active, new_expert))
    def _():
        for cp in weight_copies(e):
            cp.wait()

        @pl.when(e + 1 < N_EXPERTS)
        def _():
            for cp in weight_copies(e + 1):
                cp.start()

        half = lax.rem(e, 2)

        def cast_rows(c, carry):
            rows = pl.ds(pl.multiple_of(c * LANES, LANES), LANES)
            wgu_bf[rows, :] = wgu_f32[half, rows, :].astype(BF16)
            wdn_bf[rows, :] = wdn_f32[half, rows, :].astype(BF16)
            return carry

        lax.fori_loop(0, D_MODEL // LANES, cast_rows, 0)

    @pl.when(active)
    def _():
        x = jnp.concatenate(
            [xr_ref[pl.ds(c, r, stride=ROW_SLABS), :] for c in range(ROW_SLABS)], axis=1)
        gu = jnp.dot(x.astype(BF16), wgu_bf[...], preferred_element_type=F32) + bgu_ref[0]
        g = jnp.minimum(gu[:, :D_EXPERT], SWIGLU_LIMIT)
        u = jnp.clip(gu[:, D_EXPERT:], -SWIGLU_LIMIT, SWIGLU_LIMIT)
        act = (u + 1.0) * (g * jax.nn.sigmoid(SWIGLU_ALPHA * g))
        y = jnp.dot(act.astype(BF16), wdn_bf[...], preferred_element_type=F32) + bdn_ref[0]
        for c in range(ROW_SLABS):
            yr_ref[pl.ds(c, r, stride=ROW_SLABS), :] = y[:, c * LANES:(c + 1) * LANES]

    @pl.when(jnp.logical_not(active))
    def _():
        yr_ref[...] = jnp.zeros_like(yr_ref)


def _final_kernel(pstart_ref, meta_ref, meta_next_ref, x1_ref, route_ref, p_ref, gple_ref,
                  wpg_ref, wpp_ref, yr_hbm, o_ref, gstage, sems):
    t = TOKEN_TILE
    i = pl.program_id(0)
    half = lax.rem(i, 2)

    def run_copy(buf_half, slot, row):
        dst = gstage.at[_slab_rows(buf_half * RUN_SLOTS + slot, RUN_CHUNK), :]
        return pltpu.make_async_copy(yr_hbm.at[_slab_rows(row, RUN_CHUNK), :], dst, sems.at[buf_half])

    @pl.when(i == 0)
    def _():
        gstage[...] = jnp.zeros_like(gstage)
        _for_each_run_chunk(meta_ref, pstart_ref, lambda s, r: run_copy(0, s, r).start())

    @pl.when(i + 1 < pl.num_programs(0))
    def _():
        _for_each_run_chunk(meta_next_ref, pstart_ref, lambda s, r: run_copy(1 - half, s, r).start())

    _for_each_run_chunk(meta_ref, pstart_ref, lambda s, r: run_copy(half, 0, 0).wait())

    base = half * (RUN_SLOTS * ROW_SLABS)
    rows = jnp.concatenate(
        [gstage[pl.ds(base + c, RUN_SLOTS, stride=ROW_SLABS), :] for c in range(ROW_SLABS)], axis=1)
    route = route_ref[...]
    s_i = lax.broadcasted_iota(jnp.int32, (t, RUN_SLOTS), 1).astype(F32)
    wmat = jnp.zeros((t, RUN_SLOTS), F32)
    for k in range(TOP_K):
        wmat = jnp.where(s_i == route[:, TOP_K + k:TOP_K + k + 1], route[:, k:k + 1], wmat)
    rows_hi = rows.astype(BF16)
    rows_lo = (rows - rows_hi.astype(F32)).astype(BF16)
    w_hi = wmat.astype(BF16)
    w_lo = (wmat - w_hi.astype(F32)).astype(BF16)
    y = (jnp.dot(w_hi, rows_hi, preferred_element_type=F32)
         + jnp.dot(w_lo, rows_hi, preferred_element_type=F32)
         + jnp.dot(w_hi, rows_lo, preferred_element_type=F32))
    x2 = x1_ref[...] + y
    hp = _rms(x2, gple_ref[...]).astype(BF16)
    gate = jax.nn.sigmoid(jnp.dot(hp, wpg_ref[...], preferred_element_type=F32))
    emb = jnp.dot(p_ref[...].astype(BF16), wpp_ref[...], preferred_element_type=F32)
    o_ref[...] = x2 + gate * emb


def _rope_tables(s, half, x1_starts, x2_starts):
    inv_freq = ROPE_THETA ** (-(np.arange(half, dtype=np.float64) / half))
    ang = np.arange(s, dtype=np.float64)[:, None] * inv_freq[None, :]
    cos, sin = np.cos(ang), np.sin(ang)
    cos_t, sin_t = np.ones((s, LANES), np.float32), np.zeros((s, LANES), np.float32)
    for st in x1_starts:
        cos_t[:, st:st + half] = cos
        sin_t[:, st:st + half] = -sin
    for st in x2_starts:
        cos_t[:, st:st + half] = cos
        sin_t[:, st:st + half] = sin
    return jnp.asarray(cos_t), jnp.asarray(sin_t)


_MOBA_LANE_COLS = tuple(list(range(0, 32)) + list(range(64, 96)) + list(range(32, 64))
                        + list(range(96, 128)))
_MLA_LANE_DIMS = tuple(list(range(80, 96)) + list(range(0, 48)) + list(range(64, 80))
                       + list(range(48, 64)) + [MLA_QK_DIM] * 32)


def _moba_lanes(w):
    k, width = w.shape
    cols = jnp.asarray(_MOBA_LANE_COLS, jnp.int32)
    return w.reshape(k, width // LANES, LANES)[:, :, cols].reshape(k, width)


def _mla_lanes(w, heads):
    k = w.shape[0]
    w = jnp.pad(w.reshape(k, heads, MLA_QK_DIM), ((0, 0), (0, 0), (0, 1)))
    return w[:, :, jnp.asarray(_MLA_LANE_DIMS, jnp.int32)].reshape(k, heads * LANES)


def _row(v):
    return v.reshape(1, -1).astype(F32)


def _layer(x, p_i, g_mix, w_in, moba_q_norm, moba_k_norm, mla_q_lat_norm, w_uq, mla_kv_lat_norm,
           w_ukv, mla_q_norm, mla_k_norm, w_branch_a, w_branch_b, w_out, g_ffn, w_router, b_router,
           w_gate_up, b_gate_up, w_down, b_down, g_ple, w_ple_gate, w_ple_proj):
    b, s, d = x.shape
    n = b * s
    assert d == D_MODEL and s % ATTN_TILE == 0
    assert s // MOBA_BLOCK <= GATE_ROWS and TOKEN_TILE == ATTN_TILE == MOBA_BLOCK
    assert D_EXPERT == D_MODEL and RUN_CHUNK == SUBLANES
    n_tiles = n // TOKEN_TILE
    tiles_per_seq = s // TOKEN_TILE
    xf = x.reshape(n, d)

    off = [0]
    for wdt in (MOBA_WIDTH, MOBA_WIDTH, MOBA_WIDTH, MLA_Q_LORA, MLA_KV_LORA, MLA_ROPE_DIM, D_MODEL, D_MODEL):
        off.append(off[-1] + wdt)
    seg = [w_in[:, off[i]:off[i + 1]] for i in range(8)]
    kpe_cols = _mla_lanes(jnp.pad(seg[5], ((0, 0), (MLA_NOPE_DIM, 0))), 1)
    w_in_p = jnp.concatenate([_moba_lanes(seg[0]), _moba_lanes(seg[1])] + seg[2:5] + [kpe_cols]
                             + seg[6:], axis=1).astype(BF16)
    assert w_in_p.shape[1] == D_IN_PACKED
    w_uq_p = _mla_lanes(w_uq, MLA_HEADS).astype(BF16)
    w_ukv_h = w_ukv.reshape(MLA_KV_LORA, MLA_HEADS, MLA_NOPE_DIM + MLA_V_DIM)
    w_uk_p = _mla_lanes(jnp.pad(w_ukv_h[:, :, :MLA_NOPE_DIM], ((0, 0), (0, 0), (0, MLA_ROPE_DIM)))
                        .reshape(MLA_KV_LORA, -1), MLA_HEADS).astype(BF16)
    w_uv = w_ukv_h[:, :, MLA_NOPE_DIM:].reshape(MLA_KV_LORA, MLA_WIDTH).astype(BF16)
    gqa = _moba_lanes(_row(jnp.tile(moba_q_norm, 2))) * (MOBA_HEAD_DIM ** -0.5)
    gka = _moba_lanes(_row(jnp.tile(moba_k_norm, 2)))
    gqb = _mla_lanes(_row(mla_q_norm), 1) * (MLA_QK_DIM ** -0.5)
    gkb = _mla_lanes(_row(mla_k_norm), 1)
    half_a, half_b = MOBA_HEAD_DIM // 2, MLA_ROPE_DIM // 2
    cosa, sina = _rope_tables(s, half_a, (0, half_a), (LANES // 2, LANES // 2 + half_a))
    cosb, sinb = _rope_tables(s, half_b, (LANES // 2,), (0,))

    tok = lambda width: pl.BlockSpec((TOKEN_TILE, width), lambda i: (i, 0))
    whole = lambda arr: pl.BlockSpec(arr.shape, lambda i: (0,) * arr.ndim)
    seq_tab = pl.BlockSpec((TOKEN_TILE, LANES), lambda i: (i % tiles_per_seq, 0))
    vt_spec = pl.BlockSpec((1, MOBA_WIDTH, TOKEN_TILE), lambda i: (i, 0, 0))
    params = pltpu.CompilerParams(dimension_semantics=("arbitrary",), vmem_limit_bytes=VMEM_LIMIT)

    consts1 = [_row(g_mix), w_in_p, gqa, gka]
    consts2 = [_row(mla_q_lat_norm), w_uq_p, _row(mla_kv_lat_norm), w_uk_p, w_uv, gqb, gkb]
    qa, ka, va, kmean, qb, kb, vb, ga, gb = pl.pallas_call(
        _inproj_kernel,
        grid=(n_tiles,),
        in_specs=([tok(d)] + [whole(a) for a in consts1] + [seq_tab] * 2
                  + [whole(a) for a in consts2] + [seq_tab] * 2),
        out_specs=[tok(MOBA_WIDTH), tok(MOBA_WIDTH), vt_spec,
                   pl.BlockSpec((1, 1, MOBA_WIDTH), lambda i: (i, 0, 0)),
                   tok(MLA_HEADS * LANES), tok(MLA_HEADS * LANES), vt_spec,
                   tok(d), tok(d)],
        out_shape=[jax.ShapeDtypeStruct((n, MOBA_WIDTH), BF16)] * 2
        + [jax.ShapeDtypeStruct((n_tiles, MOBA_WIDTH, TOKEN_TILE), BF16)]
        + [jax.ShapeDtypeStruct((n_tiles, 1, MOBA_WIDTH), F32)]
        + [jax.ShapeDtypeStruct((n, MLA_HEADS * LANES), BF16)] * 2
        + [jax.ShapeDtypeStruct((n_tiles, MLA_WIDTH, TOKEN_TILE), BF16)]
        + [jax.ShapeDtypeStruct((n, d), BF16)] * 2,
        compiler_params=params,
        name="in_projection",
    )(xf, *consts1, cosa, sina, *consts2, cosb, sinb)

    kmean = kmean.reshape(b, tiles_per_seq, MOBA_WIDTH)
    kmean = jnp.pad(kmean, ((0, 0), (0, GATE_ROWS - tiles_per_seq), (0, 0))).astype(BF16)

    r3 = lambda a: a.reshape(b, s, a.shape[-1])
    r4 = lambda a: a.reshape(b, tiles_per_seq, a.shape[1], TOKEN_TILE)
    ya = _attention(r3(qa), r3(ka), r4(va), kmean, moba=True).reshape(n, MOBA_WIDTH)
    yb = _attention(r3(qb), r3(kb), r4(vb), None, moba=False).reshape(n, MLA_WIDTH)

    wr_p = jnp.pad(w_router, ((0, 0), (0, LANES - N_EXPERTS))).astype(BF16)
    br_p = jnp.pad(b_router.astype(F32), (0, LANES - N_EXPERTS), constant_values=NEG).reshape(1, LANES)
    consts3 = [w_branch_a.astype(BF16), w_branch_b.astype(BF16), w_out.astype(BF16), _row(g_ffn),
               wr_p, br_p]
    x1, h_ffn, route, meta, cnt = pl.pallas_call(
        _merge_kernel,
        grid=(n_tiles,),
        in_specs=[tok(d), tok(MOBA_WIDTH), tok(MLA_WIDTH), tok(d), tok(d)]
        + [whole(a) for a in consts3],
        out_specs=[tok(d), tok(d), tok(LANES),
                   pl.BlockSpec((1, SUBLANES, LANES), lambda i: (i, 0, 0)),
                   pl.BlockSpec((SUBLANES, LANES), lambda i: (0, 0))],
        out_shape=[jax.ShapeDtypeStruct((n, d), F32),
                   jax.ShapeDtypeStruct((n, d), BF16),
                   jax.ShapeDtypeStruct((n, LANES), F32),
                   jax.ShapeDtypeStruct((n_tiles, SUBLANES, LANES), F32),
                   jax.ShapeDtypeStruct((SUBLANES, LANES), F32)],
        scratch_shapes=[pltpu.VMEM((SUBLANES, LANES), F32)],
        compiler_params=params,
        name="merge_router",
    )(xf, ya, yb, ga, gb, *consts3)

    rb = EXPERT_ROWS
    n_blocks = -(-(n * TOP_K + N_EXPERTS * (RUN_CHUNK - 1)) // rb) + N_EXPERTS
    n_rows = n_blocks * rb
    counts = cnt[0, :N_EXPERTS].astype(jnp.int32)
    pcounts = ((counts + (RUN_CHUNK - 1) + rb - 1) // rb) * rb
    pends = jnp.cumsum(pcounts).astype(jnp.int32)
    pstarts = (pends - pcounts).astype(jnp.int32)
    nact = (pends[-1] // rb).astype(jnp.int32).reshape(1)
    blk = jnp.minimum(jnp.arange(n_blocks, dtype=jnp.int32), nact[0] - 1)
    blk_e = jnp.sum((pends[None, :] <= (blk * rb)[:, None]).astype(jnp.int32), axis=1)
    blk_e = jnp.minimum(blk_e, N_EXPERTS - 1)
    meta_i = meta[:, :4, :].astype(jnp.int32).reshape(n_tiles * 4 * LANES)
    meta_spec = lambda shift: pl.BlockSpec(
        (4 * LANES,), lambda i, *_: (jnp.clip(i + shift, 0, n_tiles - 1),), memory_space=pltpu.SMEM)

    xr = pl.pallas_call(
        _dispatch_kernel,
        grid_spec=pltpu.PrefetchScalarGridSpec(
            num_scalar_prefetch=3,
            grid=(n_tiles,),
            in_specs=[meta_spec(0), meta_spec(-1),
                      pl.BlockSpec((TOKEN_TILE, LANES), lambda i, *_: (i, 0)),
                      pl.BlockSpec((TOKEN_TILE, d), lambda i, *_: (i, 0))],
            out_specs=pl.BlockSpec(memory_space=pl.ANY),
            scratch_shapes=[pltpu.VMEM((2 * RUN_SLOTS * ROW_SLABS, LANES), F32),
                            pltpu.VMEM((EXPERT_ROWS * ROW_SLABS, LANES), F32),
                            pltpu.SemaphoreType.DMA((2,)), pltpu.SemaphoreType.DMA(())]),
        out_shape=jax.ShapeDtypeStruct((n_rows * ROW_SLABS, LANES), F32),
        compiler_params=params,
        name="dispatch_rows",
    )(pstarts, pends, nact, meta_i, meta_i, route, h_ffn)

    act_blk = lambda i, be, na: jnp.minimum(i, na[0] - 1)
    yr = pl.pallas_call(
        _expert_kernel,
        grid_spec=pltpu.PrefetchScalarGridSpec(
            num_scalar_prefetch=2,
            grid=(n_blocks,),
            in_specs=[
                pl.BlockSpec((rb * ROW_SLABS, LANES), lambda i, be, na: (act_blk(i, be, na), 0)),
                pl.BlockSpec(memory_space=pl.ANY),
                pl.BlockSpec((1, 1, 2 * D_EXPERT), lambda i, be, na: (be[i], 0, 0)),
                pl.BlockSpec(memory_space=pl.ANY),
                pl.BlockSpec((1, 1, d), lambda i, be, na: (be[i], 0, 0)),
            ],
            out_specs=pl.BlockSpec((rb * ROW_SLABS, LANES), lambda i, be, na: (i, 0)),
            scratch_shapes=[pltpu.VMEM((2, d, 2 * D_EXPERT), F32), pltpu.VMEM((2, D_EXPERT, d), F32),
                            pltpu.VMEM((d, 2 * D_EXPERT), BF16), pltpu.VMEM((D_EXPERT, d), BF16),
                            pltpu.SemaphoreType.DMA((2, 2))]),
        out_shape=jax.ShapeDtypeStruct((n_rows * ROW_SLABS, LANES), F32),
        compiler_params=params,
        name="experts",
    )(blk_e, nact, xr, w_gate_up.astype(F32),
      b_gate_up.reshape(N_EXPERTS, 1, -1).astype(F32), w_down.astype(F32),
      b_down.reshape(N_EXPERTS, 1, -1).astype(F32))

    ftok = lambda width: pl.BlockSpec((TOKEN_TILE, width), lambda i, ps: (i, 0))
    fwhole = lambda arr: pl.BlockSpec(arr.shape, lambda i, ps: (0,) * arr.ndim)
    consts4 = [_row(g_ple), w_ple_gate.astype(BF16), w_ple_proj.astype(BF16)]
    out = pl.pallas_call(
        _final_kernel,
        grid_spec=pltpu.PrefetchScalarGridSpec(
            num_scalar_prefetch=1,
            grid=(n_tiles,),
            in_specs=[meta_spec(0), meta_spec(1), ftok(d), ftok(LANES), ftok(PLE_DIM)]
            + [fwhole(a) for a in consts4] + [pl.BlockSpec(memory_space=pl.ANY)],
            out_specs=ftok(d),
            scratch_shapes=[pltpu.VMEM((2 * RUN_SLOTS * ROW_SLABS, LANES), F32),
                            pltpu.SemaphoreType.DMA((2,))]),
        out_shape=jax.ShapeDtypeStruct((n, d), F32),
        compiler_params=params,
        name="combine_ple",
    )(pstarts, meta_i, meta_i, x1, route, p_i.reshape(n, PLE_DIM), *consts4, yr)
    return out.reshape(b, s, d)


def kernel(x, p, g_mix, w_in, moba_q_norm, moba_k_norm, mla_q_lat_norm, w_uq, mla_kv_lat_norm, w_ukv, mla_q_norm, mla_k_norm, w_branch_a, w_branch_b, w_out, g_ffn, w_router, b_router, w_gate_up, b_gate_up, w_down, b_down, g_ple, w_ple_gate, w_ple_proj):
    for i in range(p.shape[0]):
        x = _layer(x, p[i], g_mix[i], w_in[i], moba_q_norm[i], moba_k_norm[i], mla_q_lat_norm[i],
                   w_uq[i], mla_kv_lat_norm[i], w_ukv[i], mla_q_norm[i], mla_k_norm[i],
                   w_branch_a[i], w_branch_b[i], w_out[i], g_ffn[i], w_router[i], b_router[i],
                   w_gate_up[i], b_gate_up[i], w_down[i], b_down[i], g_ple[i], w_ple_gate[i],
                   w_ple_proj[i])
    return x
```

```python
import functools

import numpy as np
import jax
import jax.numpy as jnp
from jax import lax
from jax.experimental import pallas as pl
from jax.experimental.pallas import tpu as pltpu

F32 = jnp.float32
BF16 = jnp.bfloat16

D_MODEL = 1024
PLE_DIM = 256
EPS = 1e-6
ROPE_THETA = 10000.0
MOBA_HEADS = 8
MOBA_HEAD_DIM = 64
MOBA_BLOCK = 256
MOBA_TOPK = 3
MOBA_WIDTH = MOBA_HEADS * MOBA_HEAD_DIM
MLA_HEADS = 8
MLA_Q_LORA = 256
MLA_KV_LORA = 128
MLA_NOPE_DIM = 64
MLA_ROPE_DIM = 32
MLA_V_DIM = 64
MLA_QK_DIM = MLA_NOPE_DIM + MLA_ROPE_DIM
MLA_WIDTH = MLA_HEADS * MLA_V_DIM
N_EXPERTS = 32
TOP_K = 4
D_EXPERT = 1024
SWIGLU_LIMIT = 7.0
SWIGLU_ALPHA = 1.702

LANES = 128
SUBLANES = 8
ROW_SLABS = D_MODEL // LANES
VMEM_LIMIT = 56 * 1024 * 1024

TOKEN_TILE = 256
ATTN_TILE = 256
ATTN_GROUPS = 4
EXPERT_ROWS = 256
RUN_CHUNK = 8
RUN_SLOTS = -(-(TOKEN_TILE * TOP_K + N_EXPERTS * (RUN_CHUNK - 1)) // 256) * 256

NEG = -1e30
MASK_BIAS = -1e9

C_QA, C_KA, C_VA = 0, 512, 1024
C_CQ, C_CKV, C_KPE = 1536, 1792, 1920
C_GA, C_GB = 2048, 3072
D_IN_PACKED = 4096


def _rms(x, gain):
    return x * lax.rsqrt(jnp.mean(x * x, axis=-1, keepdims=True) + EPS) * gain


def _rope(t, cos, sin):
    return t * cos + pltpu.roll(t, LANES // 2, 1) * sin


def _moba_even_head(lane):
    return (lane & (MOBA_HEAD_DIM // 2)) == 0


def _inproj_kernel(x_ref, gmix_ref, win_ref, gqa_ref, gka_ref, cosa_ref, sina_ref,
                   gql_ref, wuq_ref, gkvl_ref, wuk_ref, wuv_ref, gqb_ref, gkb_ref,
                   cosb_ref, sinb_ref,
                   qa_ref, ka_ref, vat_ref, kmean_ref, qb_ref, kb_ref, vbt_ref, ga_ref, gb_ref):
    hn = _rms(x_ref[...], gmix_ref[...]).astype(BF16)

    def proj(c0, width):
        return jnp.dot(hn, win_ref[:, c0:c0 + width], preferred_element_type=F32)

    first = _moba_even_head(lax.broadcasted_iota(jnp.int32, (TOKEN_TILE, LANES), 1))
    cosa, sina = cosa_ref[...], sina_ref[...]

    def moba_norm_rope(t, gain):
        sq = t * t
        ss0 = jnp.sum(jnp.where(first, sq, 0.0), axis=-1, keepdims=True)
        ss1 = jnp.sum(jnp.where(first, 0.0, sq), axis=-1, keepdims=True)
        ms = jnp.where(first, ss0, ss1) * (1.0 / MOBA_HEAD_DIM)
        t = t * lax.rsqrt(ms + EPS) * gain
        return _rope(t, cosa, sina)

    qa = proj(C_QA, MOBA_WIDTH)
    ka = proj(C_KA, MOBA_WIDTH)
    for c in range(MOBA_WIDTH // LANES):
        sl = slice(c * LANES, (c + 1) * LANES)
        qa_ref[:, sl] = moba_norm_rope(qa[:, sl], gqa_ref[...]).astype(BF16)
        kc = moba_norm_rope(ka[:, sl], gka_ref[...])
        ka_ref[:, sl] = kc.astype(BF16)
        kmean_ref[0, :, sl] = jnp.mean(kc, axis=0, keepdims=True)
    vat_ref[0] = proj(C_VA, MOBA_WIDTH).T.astype(BF16)

    cosb, sinb = cosb_ref[...], sinb_ref[...]

    def mla_norm_rope(t, gain):
        ms = jnp.sum(t * t, axis=-1, keepdims=True) * (1.0 / MLA_QK_DIM)
        t = t * lax.rsqrt(ms + EPS) * gain
        return _rope(t, cosb, sinb)

    cq = _rms(proj(C_CQ, MLA_Q_LORA), gql_ref[...]).astype(BF16)
    qb = jnp.dot(cq, wuq_ref[...], preferred_element_type=F32)
    ckv = _rms(proj(C_CKV, MLA_KV_LORA), gkvl_ref[...]).astype(BF16)
    kn = jnp.dot(ckv, wuk_ref[...], preferred_element_type=F32)
    kpe = proj(C_KPE, LANES)
    for h in range(MLA_HEADS):
        sl = slice(h * LANES, (h + 1) * LANES)
        qb_ref[:, sl] = mla_norm_rope(qb[:, sl], gqb_ref[...]).astype(BF16)
        kb_ref[:, sl] = mla_norm_rope(kn[:, sl] + kpe, gkb_ref[...]).astype(BF16)
    vbt_ref[0] = jnp.dot(ckv, wuv_ref[...], preferred_element_type=F32).T.astype(BF16)

    ga_ref[...] = jax.nn.sigmoid(proj(C_GA, D_MODEL)).astype(BF16)
    gb_ref[...] = jax.nn.sigmoid(proj(C_GB, D_MODEL)).astype(BF16)


_NT = (((1,), (1,)), ((), ()))


GATE_ROWS = 16


def _attn_kernel(*refs, moba):
    if moba:
        q_ref, k_ref, vt_ref, kmean_ref, o_ref = refs
    else:
        q_ref, k_ref, vt_ref, o_ref = refs
    t = ATTN_TILE
    hd = MOBA_HEAD_DIM
    n_heads = 2 * ATTN_GROUPS
    qi = pl.program_id(2)
    key_i = lax.broadcasted_iota(jnp.int32, (t, t), 0)
    qry_i = lax.broadcasted_iota(jnp.int32, (t, t), 1)

    blk = lax.broadcasted_iota(jnp.int32, (GATE_ROWS, t), 0)
    heads, biases = [], []
    for hh in range(n_heads):
        if moba:
            lane = lax.broadcasted_iota(jnp.int32, (t, LANES), 1)
            even = _moba_even_head(lane)
            head_lanes = even if hh % 2 == 0 else jnp.logical_not(even)
            kcols = slice((hh // 2) * LANES, (hh // 2 + 1) * LANES)
            q = jnp.where(head_lanes, q_ref[0, :, kcols], jnp.zeros((), BF16))
            gate = lax.dot_general(kmean_ref[0, :, kcols], q, _NT, preferred_element_type=F32)
            g = jnp.where(blk < qi, gate, -jnp.inf)
            keep = jnp.zeros((GATE_ROWS, t), F32)
            for _ in range(MOBA_TOPK):
                gmax = jnp.max(g, axis=0, keepdims=True)
                pick = jnp.min(jnp.where(g == gmax, blk, GATE_ROWS), axis=0, keepdims=True)
                hit = blk == jnp.where(gmax > -jnp.inf, pick, GATE_ROWS)
                keep = jnp.where(hit, 1.0, keep)
                g = jnp.where(hit, -jnp.inf, g)
            biases.append(jnp.where(keep > 0.0, 0.0, MASK_BIAS))
        else:
            kcols = slice(hh * LANES, (hh + 1) * LANES)
            q = q_ref[0, :, kcols]
        heads.append((q, kcols))

    def update(s, vt_blk, state):
        m_prev, l_prev, acc = state
        m_new = jnp.maximum(m_prev, jnp.max(s, axis=0, keepdims=True))
        alpha = jnp.exp(m_prev - m_new)
        p = jnp.exp(s - m_new)
        l_new = alpha * l_prev + jnp.sum(p, axis=0, keepdims=True)
        acc = alpha * acc + jnp.dot(vt_blk, p.astype(BF16), preferred_element_type=F32)
        return m_new, l_new, acc

    def past_block(j, states):
        start = pl.multiple_of(j * t, t)
        scores = [lax.dot_general(k_ref[0, pl.ds(start, t), kcols], q, _NT,
                                  preferred_element_type=F32) for q, kcols in heads]
        out = []
        for hh, s in enumerate(scores):
            if moba:
                s = jnp.sum(jnp.where(blk == j, biases[hh], 0.0), axis=0, keepdims=True) + s
            out.append(update(s, vt_ref[0, j, hh * hd:(hh + 1) * hd, :], states[hh]))
        return tuple(out)

    init = (jnp.full((1, t), NEG, F32), jnp.zeros((1, t), F32), jnp.zeros((hd, t), F32))
    states = lax.fori_loop(0, qi, past_block, (init,) * n_heads)

    diag_start = pl.multiple_of(qi * t, t)
    scores = [lax.dot_general(k_ref[0, pl.ds(diag_start, t), kcols], q, _NT,
                              preferred_element_type=F32) for q, kcols in heads]
    outs = []
    for hh, s in enumerate(scores):
        s = jnp.where(key_i <= qry_i, s, NEG)
        _, l_fin, acc = update(s, vt_ref[0, qi, hh * hd:(hh + 1) * hd, :], states[hh])
        outs.append(acc / l_fin)
    o_ref[0] = jnp.concatenate(outs, axis=0).T.astype(BF16)


def _attention(q, k, vt, kmean, *, moba):
    b, s, _ = q.shape
    v_cols = ATTN_GROUPS * LANES
    steps = vt.shape[2] // v_cols
    nblk = s // ATTN_TILE
    qk_cols = v_cols if moba else 2 * v_cols
    in_specs = [
        pl.BlockSpec((1, ATTN_TILE, qk_cols), lambda bi, gi, qi: (bi, qi, gi)),
        pl.BlockSpec((1, s, qk_cols), lambda bi, gi, qi: (bi, 0, gi)),
        pl.BlockSpec((1, nblk, v_cols, ATTN_TILE), lambda bi, gi, qi: (bi, 0, gi, 0)),
    ]
    args = [q, k, vt]
    if moba:
        in_specs.append(pl.BlockSpec((1, GATE_ROWS, v_cols), lambda bi, gi, qi: (bi, 0, gi)))
        args.append(kmean)
    return pl.pallas_call(
        functools.partial(_attn_kernel, moba=moba),
        grid=(b, steps, nblk),
        in_specs=in_specs,
        out_specs=pl.BlockSpec((1, ATTN_TILE, v_cols), lambda bi, gi, qi: (bi, qi, gi)),
        out_shape=jax.ShapeDtypeStruct((b, s, steps * v_cols), BF16),
        compiler_params=pltpu.CompilerParams(
            dimension_semantics=("arbitrary", "arbitrary", "arbitrary"),
            vmem_limit_bytes=VMEM_LIMIT),
        name="moba_attention" if moba else "mla_attention",
    )(*args)


def _merge_kernel(x_ref, ya_ref, yb_ref, ga_ref, gb_ref, wa_ref, wb_ref, wo_ref, gffn_ref,
                  wr_ref, br_ref, x1_ref, h_ref, route_ref, meta_ref, cnt_ref, run_scr):
    t = TOKEN_TILE

    @pl.when(pl.program_id(0) == 0)
    def _():
        run_scr[...] = jnp.zeros_like(run_scr)

    merged = (ga_ref[...].astype(F32) * jnp.dot(ya_ref[...], wa_ref[...], preferred_element_type=F32)
              + gb_ref[...].astype(F32) * jnp.dot(yb_ref[...], wb_ref[...], preferred_element_type=F32))
    x1 = x_ref[...] + jnp.dot(merged.astype(BF16), wo_ref[...], preferred_element_type=F32)
    x1_ref[...] = x1
    h = _rms(x1, gffn_ref[...]).astype(BF16)
    h_ref[...] = h

    logits = jnp.dot(h, wr_ref[...], preferred_element_type=F32) + br_ref[...]
    lane = lax.broadcasted_iota(jnp.int32, (t, LANES), 1)
    lg = logits
    hits = []
    top = None
    for r in range(TOP_K):
        gmax = jnp.max(lg, axis=-1, keepdims=True)
        pick = jnp.min(jnp.where(lg == gmax, lane, LANES), axis=-1, keepdims=True)
        hit = lane == pick
        if r == 0:
            top = gmax
        hits.append(hit)
        lg = jnp.where(hit, -jnp.inf, lg)
    sel = jnp.where(lg == -jnp.inf, 1.0, 0.0)
    wgt = sel * jnp.exp(logits - top)
    wgt = wgt / jnp.sum(wgt, axis=-1, keepdims=True)

    r_i = lax.broadcasted_iota(jnp.int32, (t, t), 0)
    c_i = lax.broadcasted_iota(jnp.int32, (t, t), 1)
    lower = jnp.where(c_i < r_i, 1.0, 0.0).astype(BF16)
    rank_in_tile = jnp.dot(lower, sel.astype(BF16), preferred_element_type=F32)
    tcnt = jnp.sum(sel, axis=0, keepdims=True)
    tpad = jnp.floor((tcnt + (RUN_CHUNK - 1)) * (1.0 / RUN_CHUNK)) * RUN_CHUNK
    e_r = lax.broadcasted_iota(jnp.int32, (LANES, LANES), 0)
    e_c = lax.broadcasted_iota(jnp.int32, (LANES, LANES), 1)
    before = jnp.where(e_r < e_c, 1.0, 0.0).astype(BF16)
    tbase = jnp.dot(jnp.broadcast_to(tpad, (SUBLANES, LANES)).astype(BF16), before,
                    preferred_element_type=F32)[0:1, :]
    run = run_scr[0:1, :]
    run_new = run + tcnt
    run_scr[...] = jnp.broadcast_to(run_new, run_scr.shape)
    cnt_ref[...] = jnp.broadcast_to(run_new, cnt_ref.shape)
    row = lax.broadcasted_iota(jnp.int32, (SUBLANES, LANES), 0)
    n_chunks = jnp.sum(tpad, axis=-1, keepdims=True) * (1.0 / RUN_CHUNK)
    meta_ref[0] = jnp.where(row == 0, tcnt, jnp.where(row == 1, run, jnp.where(
        row == 2, tbase, jnp.where(row == 3, n_chunks, 0.0))))

    slot_of = rank_in_tile + tbase
    route = jnp.zeros((t, LANES), F32)
    for r in range(TOP_K):
        w_r = jnp.sum(jnp.where(hits[r], wgt, 0.0), axis=-1, keepdims=True)
        slot_r = jnp.sum(jnp.where(hits[r], slot_of, 0.0), axis=-1, keepdims=True)
        route = jnp.where(lane == r, w_r, route)
        route = jnp.where(lane == TOP_K + r, slot_r, route)
    route_ref[...] = route


def _for_each_run_chunk(meta_ref, pstart_ref, fn):
    def per_expert(e, carry):
        n_chunks = lax.shift_right_logical(meta_ref[e] + (RUN_CHUNK - 1), RUN_CHUNK.bit_length() - 1)
        slot0 = meta_ref[2 * LANES + e]
        row0 = pstart_ref[e] + meta_ref[LANES + e]

        def per_chunk(c, cc):
            fn(slot0 + c * RUN_CHUNK, row0 + c * RUN_CHUNK)
            return cc

        lax.fori_loop(0, n_chunks, per_chunk, 0)
        return carry

    lax.fori_loop(0, N_EXPERTS, per_expert, 0)


WAIT_BATCH = 16


def _wait_run_chunks(meta_ref, make_copy):
    n_chunks = meta_ref[3 * LANES]
    n_batches = lax.shift_right_logical(n_chunks, WAIT_BATCH.bit_length() - 1)
    lax.fori_loop(0, n_batches, lambda b, c: (make_copy(WAIT_BATCH * RUN_CHUNK).wait(), c)[1], 0)
    lax.fori_loop(0, n_chunks - n_batches * WAIT_BATCH,
                  lambda b, c: (make_copy(RUN_CHUNK).wait(), c)[1], 0)


def _slab_rows(first_row, n_rows):
    return pl.ds(pl.multiple_of(first_row * ROW_SLABS, ROW_SLABS), n_rows * ROW_SLABS)


def _dispatch_kernel(pstart_ref, pend_ref, nact_ref, meta_ref, meta_prev_ref, route_ref, h_ref,
                     xr_hbm, stage, zero_scr, sems, zsem):
    t = TOKEN_TILE
    i = pl.program_id(0)
    half = lax.rem(i, 2)

    def run_copy(buf_half, slot, row, n_rows=RUN_CHUNK):
        src = stage.at[_slab_rows(buf_half * RUN_SLOTS + slot, n_rows), :]
        return pltpu.make_async_copy(src, xr_hbm.at[_slab_rows(row, n_rows), :], sems.at[buf_half])

    @pl.when(i == 0)
    def _():
        zero_scr[...] = jnp.zeros_like(zero_scr)
        n_blocks = xr_hbm.shape[0] // (EXPERT_ROWS * ROW_SLABS)

        def block_copy(first_row):
            return pltpu.make_async_copy(zero_scr, xr_hbm.at[_slab_rows(first_row, EXPERT_ROWS), :], zsem)

        def zero_last(e, c):
            block_copy(pend_ref[e] - EXPERT_ROWS).start()
            return c

        def zero_tail(blk, c):
            block_copy(blk * EXPERT_ROWS).start()
            return c

        lax.fori_loop(0, N_EXPERTS, zero_last, 0)
        lax.fori_loop(nact_ref[0], n_blocks, zero_tail, 0)
        lax.fori_loop(0, N_EXPERTS + n_blocks - nact_ref[0], lambda r, c: (block_copy(0).wait(), c)[1], 0)

    slot_rows = route_ref[...].T[TOP_K:2 * TOP_K, :]
    s_i = lax.broadcasted_iota(jnp.int32, (RUN_SLOTS, t), 0).astype(F32)
    pick = jnp.zeros((RUN_SLOTS, t), F32)
    for k in range(TOP_K):
        pick = jnp.where(s_i == slot_rows[k:k + 1, :], 1.0, pick)
    rows = jnp.dot(pick.astype(BF16), h_ref[...], preferred_element_type=F32)
    base = half * (RUN_SLOTS * ROW_SLABS)
    for c in range(ROW_SLABS):
        stage[pl.ds(base + c, RUN_SLOTS, stride=ROW_SLABS), :] = rows[:, c * LANES:(c + 1) * LANES]

    @pl.when(i > 0)
    def _():
        _wait_run_chunks(meta_prev_ref, lambda n_rows: run_copy(1 - half, 0, 0, n_rows))

    _for_each_run_chunk(meta_ref, pstart_ref, lambda s, r: run_copy(half, s, r).start())

    @pl.when(i == pl.num_programs(0) - 1)
    def _():
        _wait_run_chunks(meta_ref, lambda n_rows: run_copy(half, 0, 0, n_rows))


def _expert_kernel(blk_e_ref, nact_ref, xr_ref, wgu_hbm, bgu_ref, wdn_hbm, bdn_ref, yr_ref,
                   wgu_f32, wdn_f32, wgu_bf, wdn_bf, sems):
    r = EXPERT_ROWS
    i = pl.program_id(0)
    e = blk_e_ref[i]
    active = i < nact_ref[0]
    new_expert = jnp.logical_or(i == 0, e != blk_e_ref[jnp.maximum(i - 1, 0)])

    def weight_copies(expert):
        half = lax.rem(expert, 2)
        return (pltpu.make_async_copy(wgu_hbm.at[expert], wgu_f32.at[half], sems.at[0, half]),
                pltpu.make_async_copy(wdn_hbm.at[expert], wdn_f32.at[half], sems.at[1, half]))

    @pl.when(i == 0)
    def _():
        for cp in weight_copies(e):
            cp.start()

    @pl.when(jnp.logical_and(active, new_expert))
    def _():
        for cp in weight_copies(e):
            cp.wait()

        @pl.when(e + 1 < N_EXPERTS)
        def _():
            for cp in weight_copies(e + 1):
                cp.start()

        half = lax.rem(e, 2)

        def cast_rows(c, carry):
            rows = pl.ds(pl.multiple_of(c * LANES, LANES), LANES)
            wgu_bf[rows, :] = wgu_f32[half, rows, :].astype(BF16)
            wdn_bf[rows, :] = wdn_f32[half, rows, :].astype(BF16)
            return carry

        lax.fori_loop(0, D_MODEL // LANES, cast_rows, 0)

    @pl.when(active)
    def _():
        x = jnp.concatenate(
            [xr_ref[pl.ds(c, r, stride=ROW_SLABS), :] for c in range(ROW_SLABS)], axis=1)
        gu = jnp.dot(x.astype(BF16), wgu_bf[...], preferred_element_type=F32) + bgu_ref[0]
        g = jnp.minimum(gu[:, :D_EXPERT], SWIGLU_LIMIT)
        u = jnp.clip(gu[:, D_EXPERT:], -SWIGLU_LIMIT, SWIGLU_LIMIT)
        act = (u + 1.0) * (g * jax.nn.sigmoid(SWIGLU_ALPHA * g))
        y = jnp.dot(act.astype(BF16), wdn_bf[...], preferred_element_type=F32) + bdn_ref[0]
        for c in range(ROW_SLABS):
            yr_ref[pl.ds(c, r, stride=ROW_SLABS), :] = y[:, c * LANES:(c + 1) * LANES]

    @pl.when(jnp.logical_not(active))
    def _():
        yr_ref[...] = jnp.zeros_like(yr_ref)


def _final_kernel(pstart_ref, meta_ref, meta_next_ref, x1_ref, route_ref, p_ref, gple_ref,
                  wpg_ref, wpp_ref, yr_hbm, o_ref, gstage, sems):
    t = TOKEN_TILE
    i = pl.program_id(0)
    half = lax.rem(i, 2)

    def run_copy(buf_half, slot, row, n_rows=RUN_CHUNK):
        dst = gstage.at[_slab_rows(buf_half * RUN_SLOTS + slot, n_rows), :]
        return pltpu.make_async_copy(yr_hbm.at[_slab_rows(row, n_rows), :], dst, sems.at[buf_half])

    @pl.when(i == 0)
    def _():
        gstage[...] = jnp.zeros_like(gstage)
        _for_each_run_chunk(meta_ref, pstart_ref, lambda s, r: run_copy(0, s, r).start())

    @pl.when(i + 1 < pl.num_programs(0))
    def _():
        _for_each_run_chunk(meta_next_ref, pstart_ref, lambda s, r: run_copy(1 - half, s, r).start())

    _wait_run_chunks(meta_ref, lambda n_rows: run_copy(half, 0, 0, n_rows))

    base = half * (RUN_SLOTS * ROW_SLABS)
    rows = jnp.concatenate(
        [gstage[pl.ds(base + c, RUN_SLOTS, stride=ROW_SLABS), :] for c in range(ROW_SLABS)], axis=1)
    route = route_ref[...]
    s_i = lax.broadcasted_iota(jnp.int32, (t, RUN_SLOTS), 1).astype(F32)
    wmat = jnp.zeros((t, RUN_SLOTS), F32)
    for k in range(TOP_K):
        wmat = jnp.where(s_i == route[:, TOP_K + k:TOP_K + k + 1], route[:, k:k + 1], wmat)
    rows_hi = rows.astype(BF16)
    rows_lo = (rows - rows_hi.astype(F32)).astype(BF16)
    w_hi = wmat.astype(BF16)
    w_lo = (wmat - w_hi.astype(F32)).astype(BF16)
    y = (jnp.dot(w_hi, rows_hi, preferred_element_type=F32)
         + jnp.dot(w_lo, rows_hi, preferred_element_type=F32)
         + jnp.dot(w_hi, rows_lo, preferred_element_type=F32))
    x2 = x1_ref[...] + y
    hp = _rms(x2, gple_ref[...]).astype(BF16)
    gate = jax.nn.sigmoid(jnp.dot(hp, wpg_ref[...], preferred_element_type=F32))
    emb = jnp.dot(p_ref[...].astype(BF16), wpp_ref[...], preferred_element_type=F32)
    o_ref[...] = x2 + gate * emb


def _rope_tables(s, half, x1_starts, x2_starts):
    inv_freq = ROPE_THETA ** (-(np.arange(half, dtype=np.float64) / half))
    ang = np.arange(s, dtype=np.float64)[:, None] * inv_freq[None, :]
    cos, sin = np.cos(ang), np.sin(ang)
    cos_t, sin_t = np.ones((s, LANES), np.float32), np.zeros((s, LANES), np.float32)
    for st in x1_starts:
        cos_t[:, st:st + half] = cos
        sin_t[:, st:st + half] = -sin
    for st in x2_starts:
        cos_t[:, st:st + half] = cos
        sin_t[:, st:st + half] = sin
    return jnp.asarray(cos_t), jnp.asarray(sin_t)


_MOBA_LANE_COLS = tuple(list(range(0, 32)) + list(range(64, 96)) + list(range(32, 64))
                        + list(range(96, 128)))
_MLA_LANE_DIMS = tuple(list(range(80, 96)) + list(range(0, 48)) + list(range(64, 80))
                       + list(range(48, 64)) + [MLA_QK_DIM] * 32)


def _moba_lanes(w):
    k, width = w.shape
    cols = jnp.asarray(_MOBA_LANE_COLS, jnp.int32)
    return w.reshape(k, width // LANES, LANES)[:, :, cols].reshape(k, width)


def _mla_lanes(w, heads):
    k = w.shape[0]
    w = jnp.pad(w.reshape(k, heads, MLA_QK_DIM), ((0, 0), (0, 0), (0, 1)))
    return w[:, :, jnp.asarray(_MLA_LANE_DIMS, jnp.int32)].reshape(k, heads * LANES)


def _row(v):
    return v.reshape(1, -1).astype(F32)


def _layer(x, p_i, g_mix, w_in, moba_q_norm, moba_k_norm, mla_q_lat_norm, w_uq, mla_kv_lat_norm,
           w_ukv, mla_q_norm, mla_k_norm, w_branch_a, w_branch_b, w_out, g_ffn, w_router, b_router,
           w_gate_up, b_gate_up, w_down, b_down, g_ple, w_ple_gate, w_ple_proj):
    b, s, d = x.shape
    n = b * s
    assert d == D_MODEL and s % ATTN_TILE == 0
    assert s // MOBA_BLOCK <= GATE_ROWS and TOKEN_TILE == ATTN_TILE == MOBA_BLOCK
    assert D_EXPERT == D_MODEL and RUN_CHUNK == SUBLANES
    n_tiles = n // TOKEN_TILE
    tiles_per_seq = s // TOKEN_TILE
    xf = x.reshape(n, d)

    off = [0]
    for wdt in (MOBA_WIDTH, MOBA_WIDTH, MOBA_WIDTH, MLA_Q_LORA, MLA_KV_LORA, MLA_ROPE_DIM, D_MODEL, D_MODEL):
        off.append(off[-1] + wdt)
    seg = [w_in[:, off[i]:off[i + 1]] for i in range(8)]
    kpe_cols = _mla_lanes(jnp.pad(seg[5], ((0, 0), (MLA_NOPE_DIM, 0))), 1)
    w_in_p = jnp.concatenate([_moba_lanes(seg[0]), _moba_lanes(seg[1])] + seg[2:5] + [kpe_cols]
                             + seg[6:], axis=1).astype(BF16)
    assert w_in_p.shape[1] == D_IN_PACKED
    w_uq_p = _mla_lanes(w_uq, MLA_HEADS).astype(BF16)
    w_ukv_h = w_ukv.reshape(MLA_KV_LORA, MLA_HEADS, MLA_NOPE_DIM + MLA_V_DIM)
    w_uk_p = _mla_lanes(jnp.pad(w_ukv_h[:, :, :MLA_NOPE_DIM], ((0, 0), (0, 0), (0, MLA_ROPE_DIM)))
                        .reshape(MLA_KV_LORA, -1), MLA_HEADS).astype(BF16)
    w_uv = w_ukv_h[:, :, MLA_NOPE_DIM:].reshape(MLA_KV_LORA, MLA_WIDTH).astype(BF16)
    gqa = _moba_lanes(_row(jnp.tile(moba_q_norm, 2))) * (MOBA_HEAD_DIM ** -0.5)
    gka = _moba_lanes(_row(jnp.tile(moba_k_norm, 2)))
    gqb = _mla_lanes(_row(mla_q_norm), 1) * (MLA_QK_DIM ** -0.5)
    gkb = _mla_lanes(_row(mla_k_norm), 1)
    half_a, half_b = MOBA_HEAD_DIM // 2, MLA_ROPE_DIM // 2
    cosa, sina = _rope_tables(s, half_a, (0, half_a), (LANES // 2, LANES // 2 + half_a))
    cosb, sinb = _rope_tables(s, half_b, (LANES // 2,), (0,))

    tok = lambda width: pl.BlockSpec((TOKEN_TILE, width), lambda i: (i, 0))
    whole = lambda arr: pl.BlockSpec(arr.shape, lambda i: (0,) * arr.ndim)
    seq_tab = pl.BlockSpec((TOKEN_TILE, LANES), lambda i: (i % tiles_per_seq, 0))
    vt_spec = pl.BlockSpec((1, MOBA_WIDTH, TOKEN_TILE), lambda i: (i, 0, 0))
    params = pltpu.CompilerParams(dimension_semantics=("arbitrary",), vmem_limit_bytes=VMEM_LIMIT)

    consts1 = [_row(g_mix), w_in_p, gqa, gka]
    consts2 = [_row(mla_q_lat_norm), w_uq_p, _row(mla_kv_lat_norm), w_uk_p, w_uv, gqb, gkb]
    qa, ka, va, kmean, qb, kb, vb, ga, gb = pl.pallas_call(
        _inproj_kernel,
        grid=(n_tiles,),
        in_specs=([tok(d)] + [whole(a) for a in consts1] + [seq_tab] * 2
                  + [whole(a) for a in consts2] + [seq_tab] * 2),
        out_specs=[tok(MOBA_WIDTH), tok(MOBA_WIDTH), vt_spec,
                   pl.BlockSpec((1, 1, MOBA_WIDTH), lambda i: (i, 0, 0)),
                   tok(MLA_HEADS * LANES), tok(MLA_HEADS * LANES), vt_spec,
                   tok(d), tok(d)],
        out_shape=[jax.ShapeDtypeStruct((n, MOBA_WIDTH), BF16)] * 2
        + [jax.ShapeDtypeStruct((n_tiles, MOBA_WIDTH, TOKEN_TILE), BF16)]
        + [jax.ShapeDtypeStruct((n_tiles, 1, MOBA_WIDTH), F32)]
        + [jax.ShapeDtypeStruct((n, MLA_HEADS * LANES), BF16)] * 2
        + [jax.ShapeDtypeStruct((n_tiles, MLA_WIDTH, TOKEN_TILE), BF16)]
        + [jax.ShapeDtypeStruct((n, d), BF16)] * 2,
        compiler_params=params,
        name="in_projection",
    )(xf, *consts1, cosa, sina, *consts2, cosb, sinb)

    kmean = kmean.reshape(b, tiles_per_seq, MOBA_WIDTH)
    kmean = jnp.pad(kmean, ((0, 0), (0, GATE_ROWS - tiles_per_seq), (0, 0))).astype(BF16)

    r3 = lambda a: a.reshape(b, s, a.shape[-1])
    r4 = lambda a: a.reshape(b, tiles_per_seq, a.shape[1], TOKEN_TILE)
    ya = _attention(r3(qa), r3(ka), r4(va), kmean, moba=True).reshape(n, MOBA_WIDTH)
    yb = _attention(r3(qb), r3(kb), r4(vb), None, moba=False).reshape(n, MLA_WIDTH)

    wr_p = jnp.pad(w_router, ((0, 0), (0, LANES - N_EXPERTS))).astype(BF16)
    br_p = jnp.pad(b_router.astype(F32), (0, LANES - N_EXPERTS), constant_values=NEG).reshape(1, LANES)
    consts3 = [w_branch_a.astype(BF16), w_branch_b.astype(BF16), w_out.astype(BF16), _row(g_ffn),
               wr_p, br_p]
    x1, h_ffn, route, meta, cnt = pl.pallas_call(
        _merge_kernel,
        grid=(n_tiles,),
        in_specs=[tok(d), tok(MOBA_WIDTH), tok(MLA_WIDTH), tok(d), tok(d)]
        + [whole(a) for a in consts3],
        out_specs=[tok(d), tok(d), tok(LANES),
                   pl.BlockSpec((1, SUBLANES, LANES), lambda i: (i, 0, 0)),
                   pl.BlockSpec((SUBLANES, LANES), lambda i: (0, 0))],
        out_shape=[jax.ShapeDtypeStruct((n, d), F32),
                   jax.ShapeDtypeStruct((n, d), BF16),
                   jax.ShapeDtypeStruct((n, LANES), F32),
                   jax.ShapeDtypeStruct((n_tiles, SUBLANES, LANES), F32),
                   jax.ShapeDtypeStruct((SUBLANES, LANES), F32)],
        scratch_shapes=[pltpu.VMEM((SUBLANES, LANES), F32)],
        compiler_params=params,
        name="merge_router",
    )(xf, ya, yb, ga, gb, *consts3)

    rb = EXPERT_ROWS
    n_blocks = -(-(n * TOP_K + N_EXPERTS * (RUN_CHUNK - 1)) // rb) + N_EXPERTS
    n_rows = n_blocks * rb
    counts = cnt[0, :N_EXPERTS].astype(jnp.int32)
    pcounts = ((counts + (RUN_CHUNK - 1) + rb - 1) // rb) * rb
    pends = jnp.cumsum(pcounts).astype(jnp.int32)
    pstarts = (pends - pcounts).astype(jnp.int32)
    nact = (pends[-1] // rb).astype(jnp.int32).reshape(1)
    blk = jnp.minimum(jnp.arange(n_blocks, dtype=jnp.int32), nact[0] - 1)
    blk_e = jnp.sum((pends[None, :] <= (blk * rb)[:, None]).astype(jnp.int32), axis=1)
    blk_e = jnp.minimum(blk_e, N_EXPERTS - 1)
    meta_i = meta[:, :4, :].astype(jnp.int32).reshape(n_tiles * 4 * LANES)
    meta_spec = lambda shift: pl.BlockSpec(
        (4 * LANES,), lambda i, *_: (jnp.clip(i + shift, 0, n_tiles - 1),), memory_space=pltpu.SMEM)

    xr = pl.pallas_call(
        _dispatch_kernel,
        grid_spec=pltpu.PrefetchScalarGridSpec(
            num_scalar_prefetch=3,
            grid=(n_tiles,),
            in_specs=[meta_spec(0), meta_spec(-1),
                      pl.BlockSpec((TOKEN_TILE, LANES), lambda i, *_: (i, 0)),
                      pl.BlockSpec((TOKEN_TILE, d), lambda i, *_: (i, 0))],
            out_specs=pl.BlockSpec(memory_space=pl.ANY),
            scratch_shapes=[pltpu.VMEM((2 * RUN_SLOTS * ROW_SLABS, LANES), F32),
                            pltpu.VMEM((EXPERT_ROWS * ROW_SLABS, LANES), F32),
                            pltpu.SemaphoreType.DMA((2,)), pltpu.SemaphoreType.DMA(())]),
        out_shape=jax.ShapeDtypeStruct((n_rows * ROW_SLABS, LANES), F32),
        compiler_params=params,
        name="dispatch_rows",
    )(pstarts, pends, nact, meta_i, meta_i, route, h_ffn)

    act_blk = lambda i, be, na: jnp.minimum(i, na[0] - 1)
    yr = pl.pallas_call(
        _expert_kernel,
        grid_spec=pltpu.PrefetchScalarGridSpec(
            num_scalar_prefetch=2,
            grid=(n_blocks,),
            in_specs=[
                pl.BlockSpec((rb * ROW_SLABS, LANES), lambda i, be, na: (act_blk(i, be, na), 0)),
                pl.BlockSpec(memory_space=pl.ANY),
                pl.BlockSpec((1, 1, 2 * D_EXPERT), lambda i, be, na: (be[i], 0, 0)),
                pl.BlockSpec(memory_space=pl.ANY),
                pl.BlockSpec((1, 1, d), lambda i, be, na: (be[i], 0, 0)),
            ],
            out_specs=pl.BlockSpec((rb * ROW_SLABS, LANES), lambda i, be, na: (i, 0)),
            scratch_shapes=[pltpu.VMEM((2, d, 2 * D_EXPERT), F32), pltpu.VMEM((2, D_EXPERT, d), F32),
                            pltpu.VMEM((d, 2 * D_EXPERT), BF16), pltpu.VMEM((D_EXPERT, d), BF16),
                            pltpu.SemaphoreType.DMA((2, 2))]),
        out_shape=jax.ShapeDtypeStruct((n_rows * ROW_SLABS, LANES), F32),
        compiler_params=params,
        name="experts",
    )(blk_e, nact, xr, w_gate_up.astype(F32),
      b_gate_up.reshape(N_EXPERTS, 1, -1).astype(F32), w_down.astype(F32),
      b_down.reshape(N_EXPERTS, 1, -1).astype(F32))

    ftok = lambda width: pl.BlockSpec((TOKEN_TILE, width), lambda i, ps: (i, 0))
    fwhole = lambda arr: pl.BlockSpec(arr.shape, lambda i, ps: (0,) * arr.ndim)
    consts4 = [_row(g_ple), w_ple_gate.astype(BF16), w_ple_proj.astype(BF16)]
    out = pl.pallas_call(
        _final_kernel,
        grid_spec=pltpu.PrefetchScalarGridSpec(
            num_scalar_prefetch=1,
            grid=(n_tiles,),
            in_specs=[meta_spec(0), meta_spec(1), ftok(d), ftok(LANES), ftok(PLE_DIM)]
            + [fwhole(a) for a in consts4] + [pl.BlockSpec(memory_space=pl.ANY)],
            out_specs=ftok(d),
            scratch_shapes=[pltpu.VMEM((2 * RUN_SLOTS * ROW_SLABS, LANES), F32),
                            pltpu.SemaphoreType.DMA((2,))]),
        out_shape=jax.ShapeDtypeStruct((n, d), F32),
        compiler_params=params,
        name="combine_ple",
    )(pstarts, meta_i, meta_i, x1, route, p_i.reshape(n, PLE_DIM), *consts4, yr)
    return out.reshape(b, s, d)


def kernel(x, p, g_mix, w_in, moba_q_norm, moba_k_norm, mla_q_lat_norm, w_uq, mla_kv_lat_norm, w_ukv, mla_q_norm, mla_k_norm, w_branch_a, w_branch_b, w_out, g_ffn, w_router, b_router, w_gate_up, b_gate_up, w_down, b_down, g_ple, w_ple_gate, w_ple_proj):
    for i in range(p.shape[0]):
        x = _layer(x, p[i], g_mix[i], w_in[i], moba_q_norm[i], moba_k_norm[i], mla_q_lat_norm[i],
                   w_uq[i], mla_kv_lat_norm[i], w_ukv[i], mla_q_norm[i], mla_k_norm[i],
                   w_branch_a[i], w_branch_b[i], w_out[i], g_ffn[i], w_router[i], b_router[i],
                   w_gate_up[i], b_gate_up[i], w_down[i], b_down[i], g_ple[i], w_ple_gate[i],
                   w_ple_proj[i])
    return x
```

```python
import functools

import numpy as np
import jax
import jax.numpy as jnp
from jax import lax
from jax.experimental import pallas as pl
from jax.experimental.pallas import tpu as pltpu

F32 = jnp.float32
BF16 = jnp.bfloat16

D_MODEL = 1024
PLE_DIM = 256
EPS = 1e-6
ROPE_THETA = 10000.0
MOBA_HEADS = 8
MOBA_HEAD_DIM = 64
MOBA_BLOCK = 256
MOBA_TOPK = 3
MOBA_WIDTH = MOBA_HEADS * MOBA_HEAD_DIM
MLA_HEADS = 8
MLA_Q_LORA = 256
MLA_KV_LORA = 128
MLA_NOPE_DIM = 64
MLA_ROPE_DIM = 32
MLA_V_DIM = 64
MLA_QK_DIM = MLA_NOPE_DIM + MLA_ROPE_DIM
MLA_WIDTH = MLA_HEADS * MLA_V_DIM
N_EXPERTS = 32
TOP_K = 4
D_EXPERT = 1024
SWIGLU_LIMIT = 7.0
SWIGLU_ALPHA = 1.702

LANES = 128
SUBLANES = 8
ROW_SLABS = D_MODEL // LANES
VMEM_LIMIT = 56 * 1024 * 1024

TOKEN_TILE = 256
ATTN_TILE = 256
ATTN_GROUPS = 4
EXPERT_ROWS = 256
RUN_CHUNK = 8
RUN_SLOTS = -(-(TOKEN_TILE * TOP_K + N_EXPERTS * (RUN_CHUNK - 1)) // 256) * 256

LOG2_E = 1.4426950408889634
NEG = -1e30
MASK_BIAS = -1e9

C_QA, C_KA, C_VA = 0, 512, 1024
C_CQ, C_CKV, C_KPE = 1536, 1792, 1920
C_GA, C_GB = 2048, 3072
D_IN_PACKED = 4096


def _rms(x, gain):
    return x * lax.rsqrt(jnp.mean(x * x, axis=-1, keepdims=True) + EPS) * gain


def _rope(t, cos, sin):
    return t * cos + pltpu.roll(t, LANES // 2, 1) * sin


def _moba_even_head(lane):
    return (lane & (MOBA_HEAD_DIM // 2)) == 0


def _inproj_kernel(x_ref, gmix_ref, win_ref, gqa_ref, gka_ref, cosa_ref, sina_ref,
                   gql_ref, wuq_ref, gkvl_ref, wuk_ref, wuv_ref, gqb_ref, gkb_ref,
                   cosb_ref, sinb_ref,
                   qa_ref, ka_ref, vat_ref, kmean_ref, qb_ref, kb_ref, vbt_ref, ga_ref, gb_ref):
    hn = _rms(x_ref[...], gmix_ref[...]).astype(BF16)

    def proj(c0, width):
        return jnp.dot(hn, win_ref[:, c0:c0 + width], preferred_element_type=F32)

    first = _moba_even_head(lax.broadcasted_iota(jnp.int32, (TOKEN_TILE, LANES), 1))
    cosa, sina = cosa_ref[...], sina_ref[...]

    def moba_norm_rope(t, gain):
        sq = t * t
        ss0 = jnp.sum(jnp.where(first, sq, 0.0), axis=-1, keepdims=True)
        ss1 = jnp.sum(jnp.where(first, 0.0, sq), axis=-1, keepdims=True)
        ms = jnp.where(first, ss0, ss1) * (1.0 / MOBA_HEAD_DIM)
        t = t * lax.rsqrt(ms + EPS) * gain
        return _rope(t, cosa, sina)

    qa = proj(C_QA, MOBA_WIDTH)
    ka = proj(C_KA, MOBA_WIDTH)
    for c in range(MOBA_WIDTH // LANES):
        sl = slice(c * LANES, (c + 1) * LANES)
        qa_ref[:, sl] = moba_norm_rope(qa[:, sl], gqa_ref[...]).astype(BF16)
        kc = moba_norm_rope(ka[:, sl], gka_ref[...])
        ka_ref[c] = kc.astype(BF16)
        kmean_ref[0, :, sl] = jnp.mean(kc, axis=0, keepdims=True)
    vat_ref[0] = proj(C_VA, MOBA_WIDTH).T.astype(BF16)

    cosb, sinb = cosb_ref[...], sinb_ref[...]

    def mla_norm_rope(t, gain):
        ms = jnp.sum(t * t, axis=-1, keepdims=True) * (1.0 / MLA_QK_DIM)
        t = t * lax.rsqrt(ms + EPS) * gain
        return _rope(t, cosb, sinb)

    cq = _rms(proj(C_CQ, MLA_Q_LORA), gql_ref[...]).astype(BF16)
    qb = jnp.dot(cq, wuq_ref[...], preferred_element_type=F32)
    ckv = _rms(proj(C_CKV, MLA_KV_LORA), gkvl_ref[...]).astype(BF16)
    kn = jnp.dot(ckv, wuk_ref[...], preferred_element_type=F32)
    kpe = proj(C_KPE, LANES)
    for h in range(MLA_HEADS):
        sl = slice(h * LANES, (h + 1) * LANES)
        qb_ref[:, sl] = mla_norm_rope(qb[:, sl], gqb_ref[...]).astype(BF16)
        kb_ref[h] = mla_norm_rope(kn[:, sl] + kpe, gkb_ref[...]).astype(BF16)
    vbt_ref[0] = jnp.dot(ckv, wuv_ref[...], preferred_element_type=F32).T.astype(BF16)

    ga_ref[...] = jax.nn.sigmoid(proj(C_GA, D_MODEL)).astype(BF16)
    gb_ref[...] = jax.nn.sigmoid(proj(C_GB, D_MODEL)).astype(BF16)


_NT = (((1,), (1,)), ((), ()))


GATE_ROWS = 16


def _attn_kernel(*refs, moba):
    if moba:
        q_ref, k_ref, vt_ref, kmean_ref, o_ref = refs
    else:
        q_ref, k_ref, vt_ref, o_ref = refs
    t = ATTN_TILE
    hd = MOBA_HEAD_DIM
    n_heads = 2 * ATTN_GROUPS
    qi = pl.program_id(2)
    key_i = lax.broadcasted_iota(jnp.int32, (t, t), 0)
    qry_i = lax.broadcasted_iota(jnp.int32, (t, t), 1)

    blk = lax.broadcasted_iota(jnp.int32, (GATE_ROWS, t), 0)
    heads, biases = [], []
    for hh in range(n_heads):
        if moba:
            lane = lax.broadcasted_iota(jnp.int32, (t, LANES), 1)
            even = _moba_even_head(lane)
            head_lanes = even if hh % 2 == 0 else jnp.logical_not(even)
            k_tile = hh // 2
            kcols = slice(k_tile * LANES, (k_tile + 1) * LANES)
            q = jnp.where(head_lanes, q_ref[0, :, kcols], jnp.zeros((), BF16))
            gate = lax.dot_general(kmean_ref[0, :, kcols], q, _NT, preferred_element_type=F32)
            g = jnp.where(blk < qi, gate, -jnp.inf)
            keep = jnp.zeros((GATE_ROWS, t), F32)
            for _ in range(MOBA_TOPK):
                gmax = jnp.max(g, axis=0, keepdims=True)
                pick = jnp.min(jnp.where(g == gmax, blk, GATE_ROWS), axis=0, keepdims=True)
                hit = blk == jnp.where(gmax > -jnp.inf, pick, GATE_ROWS)
                keep = jnp.where(hit, 1.0, keep)
                g = jnp.where(hit, -jnp.inf, g)
            biases.append(jnp.where(keep > 0.0, 0.0, MASK_BIAS))
        else:
            k_tile = hh
            q = q_ref[0, :, hh * LANES:(hh + 1) * LANES]
        heads.append((q, k_tile))

    def update(s, vt_blk, state):
        m_prev, l_prev, acc = state
        m_new = jnp.maximum(m_prev, jnp.max(s, axis=0, keepdims=True))
        alpha = jnp.exp2(m_prev - m_new)
        p = jnp.exp2(s - m_new)
        l_new = alpha * l_prev + jnp.sum(p, axis=0, keepdims=True)
        acc = alpha * acc + jnp.dot(vt_blk, p.astype(BF16), preferred_element_type=F32)
        return m_new, l_new, acc

    def past_block(j, states):
        start = pl.multiple_of(j * t, t)
        scores = [lax.dot_general(k_ref[k_tile, pl.ds(start, t), :], q, _NT,
                                  preferred_element_type=F32) for q, k_tile in heads]
        out = []
        for hh, s in enumerate(scores):
            if moba:
                s = jnp.sum(jnp.where(blk == j, biases[hh], 0.0), axis=0, keepdims=True) + s
            out.append(update(s, vt_ref[0, j, hh * hd:(hh + 1) * hd, :], states[hh]))
        return tuple(out)

    init = (jnp.full((1, t), NEG, F32), jnp.zeros((1, t), F32), jnp.zeros((hd, t), F32))
    states = lax.fori_loop(0, qi, past_block, (init,) * n_heads)

    diag_start = pl.multiple_of(qi * t, t)
    scores = [lax.dot_general(k_ref[k_tile, pl.ds(diag_start, t), :], q, _NT,
                              preferred_element_type=F32) for q, k_tile in heads]
    outs = []
    for hh, s in enumerate(scores):
        s = jnp.where(key_i <= qry_i, s, NEG)
        _, l_fin, acc = update(s, vt_ref[0, qi, hh * hd:(hh + 1) * hd, :], states[hh])
        outs.append(acc / l_fin)
    o_ref[0] = jnp.concatenate(outs, axis=0).T.astype(BF16)


def _attention(q, k, vt, kmean, *, moba):
    b, s, _ = q.shape
    v_cols = ATTN_GROUPS * LANES
    steps = vt.shape[2] // v_cols
    nblk = s // ATTN_TILE
    qk_cols = v_cols if moba else 2 * v_cols
    k_tiles = qk_cols // LANES
    in_specs = [
        pl.BlockSpec((1, ATTN_TILE, qk_cols), lambda bi, gi, qi: (bi, qi, gi)),
        pl.BlockSpec((k_tiles, s, LANES), lambda bi, gi, qi: (gi, bi, 0)),
        pl.BlockSpec((1, nblk, v_cols, ATTN_TILE), lambda bi, gi, qi: (bi, 0, gi, 0)),
    ]
    args = [q, k, vt]
    if moba:
        in_specs.append(pl.BlockSpec((1, GATE_ROWS, v_cols), lambda bi, gi, qi: (bi, 0, gi)))
        args.append(kmean)
    return pl.pallas_call(
        functools.partial(_attn_kernel, moba=moba),
        grid=(b, steps, nblk),
        in_specs=in_specs,
        out_specs=pl.BlockSpec((1, ATTN_TILE, v_cols), lambda bi, gi, qi: (bi, qi, gi)),
        out_shape=jax.ShapeDtypeStruct((b, s, steps * v_cols), BF16),
        compiler_params=pltpu.CompilerParams(
            dimension_semantics=("arbitrary", "arbitrary", "arbitrary"),
            vmem_limit_bytes=VMEM_LIMIT),
        name="moba_attention" if moba else "mla_attention",
    )(*args)


def _merge_kernel(x_ref, ya_ref, yb_ref, ga_ref, gb_ref, wa_ref, wb_ref, wo_ref, gffn_ref,
                  wr_ref, br_ref, x1_ref, h_ref, route_ref, meta_ref, cnt_ref, run_scr):
    t = TOKEN_TILE

    @pl.when(pl.program_id(0) == 0)
    def _():
        run_scr[...] = jnp.zeros_like(run_scr)

    merged = (ga_ref[...].astype(F32) * jnp.dot(ya_ref[...], wa_ref[...], preferred_element_type=F32)
              + gb_ref[...].astype(F32) * jnp.dot(yb_ref[...], wb_ref[...], preferred_element_type=F32))
    x1 = x_ref[...] + jnp.dot(merged.astype(BF16), wo_ref[...], preferred_element_type=F32)
    x1_ref[...] = x1
    h = _rms(x1, gffn_ref[...]).astype(BF16)
    h_ref[...] = h

    logits = jnp.dot(h, wr_ref[...], preferred_element_type=F32) + br_ref[...]
    lane = lax.broadcasted_iota(jnp.int32, (t, LANES), 1)
    lg = logits
    hits = []
    top = None
    for r in range(TOP_K):
        gmax = jnp.max(lg, axis=-1, keepdims=True)
        pick = jnp.min(jnp.where(lg == gmax, lane, LANES), axis=-1, keepdims=True)
        hit = lane == pick
        if r == 0:
            top = gmax
        hits.append(hit)
        lg = jnp.where(hit, -jnp.inf, lg)
    sel = jnp.where(lg == -jnp.inf, 1.0, 0.0)
    wgt = sel * jnp.exp(logits - top)
    wgt = wgt / jnp.sum(wgt, axis=-1, keepdims=True)

    r_i = lax.broadcasted_iota(jnp.int32, (t, t), 0)
    c_i = lax.broadcasted_iota(jnp.int32, (t, t), 1)
    lower = jnp.where(c_i < r_i, 1.0, 0.0).astype(BF16)
    rank_in_tile = jnp.dot(lower, sel.astype(BF16), preferred_element_type=F32)
    tcnt = jnp.sum(sel, axis=0, keepdims=True)
    tpad = jnp.floor((tcnt + (RUN_CHUNK - 1)) * (1.0 / RUN_CHUNK)) * RUN_CHUNK
    e_r = lax.broadcasted_iota(jnp.int32, (LANES, LANES), 0)
    e_c = lax.broadcasted_iota(jnp.int32, (LANES, LANES), 1)
    before = jnp.where(e_r < e_c, 1.0, 0.0).astype(BF16)
    tbase = jnp.dot(jnp.broadcast_to(tpad, (SUBLANES, LANES)).astype(BF16), before,
                    preferred_element_type=F32)[0:1, :]
    run = run_scr[0:1, :]
    run_new = run + tcnt
    run_scr[...] = jnp.broadcast_to(run_new, run_scr.shape)
    cnt_ref[...] = jnp.broadcast_to(run_new, cnt_ref.shape)
    row = lax.broadcasted_iota(jnp.int32, (SUBLANES, LANES), 0)
    n_chunks = jnp.sum(tpad, axis=-1, keepdims=True) * (1.0 / RUN_CHUNK)
    meta_ref[0] = jnp.where(row == 0, tcnt, jnp.where(row == 1, run, jnp.where(
        row == 2, tbase, jnp.where(row == 3, n_chunks, 0.0))))

    slot_of = rank_in_tile + tbase
    route = jnp.zeros((t, LANES), F32)
    for r in range(TOP_K):
        w_r = jnp.sum(jnp.where(hits[r], wgt, 0.0), axis=-1, keepdims=True)
        slot_r = jnp.sum(jnp.where(hits[r], slot_of, 0.0), axis=-1, keepdims=True)
        route = jnp.where(lane == r, w_r, route)
        route = jnp.where(lane == TOP_K + r, slot_r, route)
    route_ref[...] = route


BIG_COPY = 4


def _for_each_run_copy(meta_ref, pstart_ref, fn):
    big_rows = BIG_COPY * RUN_CHUNK

    def per_expert(e, carry):
        n_chunks = lax.shift_right_logical(meta_ref[e] + (RUN_CHUNK - 1), RUN_CHUNK.bit_length() - 1)
        n_big = lax.shift_right_logical(n_chunks, BIG_COPY.bit_length() - 1)
        slot0 = meta_ref[2 * LANES + e]
        row0 = pstart_ref[e] + meta_ref[LANES + e]

        def big(b, cc):
            fn(slot0 + b * big_rows, row0 + b * big_rows, big_rows)
            return cc

        def small(c, cc):
            fn(slot0 + c * RUN_CHUNK, row0 + c * RUN_CHUNK, RUN_CHUNK)
            return cc

        lax.fori_loop(0, n_big, big, 0)
        lax.fori_loop(n_big * BIG_COPY, n_chunks, small, 0)
        return carry

    lax.fori_loop(0, N_EXPERTS, per_expert, 0)


WAIT_BATCH = 16


def _wait_run_chunks(meta_ref, make_copy):
    n_chunks = meta_ref[3 * LANES]
    n_batches = lax.shift_right_logical(n_chunks, WAIT_BATCH.bit_length() - 1)
    lax.fori_loop(0, n_batches, lambda b, c: (make_copy(WAIT_BATCH * RUN_CHUNK).wait(), c)[1], 0)
    lax.fori_loop(0, n_chunks - n_batches * WAIT_BATCH,
                  lambda b, c: (make_copy(RUN_CHUNK).wait(), c)[1], 0)


def _slab_rows(first_row, n_rows):
    return pl.ds(pl.multiple_of(first_row * ROW_SLABS, ROW_SLABS), n_rows * ROW_SLABS)


def _dispatch_kernel(pstart_ref, pend_ref, nact_ref, meta_ref, meta_prev_ref, route_ref, h_ref,
                     xr_hbm, stage, zero_scr, sems, zsem):
    t = TOKEN_TILE
    i = pl.program_id(0)
    half = lax.rem(i, 2)

    def run_copy(buf_half, slot, row, n_rows=RUN_CHUNK):
        src = stage.at[_slab_rows(buf_half * RUN_SLOTS + slot, n_rows), :]
        return pltpu.make_async_copy(src, xr_hbm.at[_slab_rows(row, n_rows), :], sems.at[buf_half])

    @pl.when(i == 0)
    def _():
        zero_scr[...] = jnp.zeros_like(zero_scr)
        n_blocks = xr_hbm.shape[0] // (EXPERT_ROWS * ROW_SLABS)

        def block_copy(first_row):
            return pltpu.make_async_copy(zero_scr, xr_hbm.at[_slab_rows(first_row, EXPERT_ROWS), :], zsem)

        def pad_blocks(e):
            seg_blocks = lax.shift_right_logical(pend_ref[e] - pstart_ref[e],
                                                 EXPERT_ROWS.bit_length() - 1)
            return jnp.minimum(seg_blocks, 2)

        def zero_pad(e, c):
            lax.fori_loop(1, pad_blocks(e) + 1,
                          lambda b, cc: (block_copy(pend_ref[e] - b * EXPERT_ROWS).start(), cc)[1], 0)
            return c

        def wait_pad(e, c):
            lax.fori_loop(0, pad_blocks(e), lambda b, cc: (block_copy(0).wait(), cc)[1], 0)
            return c

        def zero_tail(blk, c):
            block_copy(blk * EXPERT_ROWS).start()
            return c

        lax.fori_loop(0, N_EXPERTS, zero_pad, 0)
        lax.fori_loop(nact_ref[0], n_blocks, zero_tail, 0)
        lax.fori_loop(0, N_EXPERTS, wait_pad, 0)
        lax.fori_loop(0, n_blocks - nact_ref[0], lambda r, c: (block_copy(0).wait(), c)[1], 0)

    slot_rows = route_ref[...].T[TOP_K:2 * TOP_K, :]
    s_i = lax.broadcasted_iota(jnp.int32, (RUN_SLOTS, t), 0).astype(F32)
    pick = jnp.zeros((RUN_SLOTS, t), F32)
    for k in range(TOP_K):
        pick = jnp.where(s_i == slot_rows[k:k + 1, :], 1.0, pick)
    rows = jnp.dot(pick.astype(BF16), h_ref[...], preferred_element_type=F32)
    base = half * (RUN_SLOTS * ROW_SLABS)
    for c in range(ROW_SLABS):
        stage[pl.ds(base + c, RUN_SLOTS, stride=ROW_SLABS), :] = rows[:, c * LANES:(c + 1) * LANES]

    @pl.when(i > 0)
    def _():
        _wait_run_chunks(meta_prev_ref, lambda n_rows: run_copy(1 - half, 0, 0, n_rows))

    _for_each_run_copy(meta_ref, pstart_ref, lambda s, r, n: run_copy(half, s, r, n).start())

    @pl.when(i == pl.num_programs(0) - 1)
    def _():
        _wait_run_chunks(meta_ref, lambda n_rows: run_copy(half, 0, 0, n_rows))


def _expert_kernel(blk_e_ref, nact_ref, xr_ref, wgu_hbm, bgu_ref, wdn_hbm, bdn_ref, yr_ref,
                   wgu_f32, wdn_f32, wgu_bf, wdn_bf, sems):
    r = EXPERT_ROWS
    i = pl.program_id(0)
    e = blk_e_ref[i]
    active = i < nact_ref[0]
    new_expert = jnp.logical_or(i == 0, e != blk_e_ref[jnp.maximum(i - 1, 0)])

    def weight_copies(expert):
        half = lax.rem(expert, 2)
        return (pltpu.make_async_copy(wgu_hbm.at[expert], wgu_f32.at[half], sems.at[0, half]),
                pltpu.make_async_copy(wdn_hbm.at[expert], wdn_f32.at[half], sems.at[1, half]))

    @pl.when(i == 0)
    def _():
        for cp in weight_copies(e):
            cp.start()

    @pl.when(jnp.logical_and(active, new_expert))
    def _():
        for cp in weight_copies(e):
            cp.wait()

        @pl.when(e + 1 < N_EXPERTS)
        def _():
            for cp in weight_copies(e + 1):
                cp.start()

        half = lax.rem(e, 2)

        def cast_rows(c, carry):
            rows = pl.ds(pl.multiple_of(c * LANES, LANES), LANES)
            wgu_bf[rows, :] = wgu_f32[half, rows, :].astype(BF16)
            wdn_bf[rows, :] = wdn_f32[half, rows, :].astype(BF16)
            return carry

        lax.fori_loop(0, D_MODEL // LANES, cast_rows, 0)

    @pl.when(active)
    def _():
        x = jnp.concatenate(
            [xr_ref[pl.ds(c, r, stride=ROW_SLABS), :] for c in range(ROW_SLABS)], axis=1)
        gu = jnp.dot(x.astype(BF16), wgu_bf[...], preferred_element_type=F32) + bgu_ref[0]
        g = jnp.minimum(gu[:, :D_EXPERT], SWIGLU_LIMIT)
        u = jnp.clip(gu[:, D_EXPERT:], -SWIGLU_LIMIT, SWIGLU_LIMIT)
        act = (u + 1.0) * (g * jax.nn.sigmoid(SWIGLU_ALPHA * g))
        y = jnp.dot(act.astype(BF16), wdn_bf[...], preferred_element_type=F32) + bdn_ref[0]
        for c in range(ROW_SLABS):
            yr_ref[pl.ds(c, r, stride=ROW_SLABS), :] = y[:, c * LANES:(c + 1) * LANES]

    @pl.when(jnp.logical_not(active))
    def _():
        yr_ref[...] = jnp.zeros_like(yr_ref)


def _final_kernel(pstart_ref, meta_ref, meta_next_ref, x1_ref, route_ref, p_ref, gple_ref,
                  wpg_ref, wpp_ref, yr_hbm, o_ref, gstage, sems):
    t = TOKEN_TILE
    i = pl.program_id(0)
    half = lax.rem(i, 2)

    def run_copy(buf_half, slot, row, n_rows=RUN_CHUNK):
        dst = gstage.at[_slab_rows(buf_half * RUN_SLOTS + slot, n_rows), :]
        return pltpu.make_async_copy(yr_hbm.at[_slab_rows(row, n_rows), :], dst, sems.at[buf_half])

    @pl.when(i == 0)
    def _():
        gstage[...] = jnp.zeros_like(gstage)
        _for_each_run_copy(meta_ref, pstart_ref, lambda s, r, n: run_copy(0, s, r, n).start())

    @pl.when(i + 1 < pl.num_programs(0))
    def _():
        _for_each_run_copy(meta_next_ref, pstart_ref,
                           lambda s, r, n: run_copy(1 - half, s, r, n).start())

    _wait_run_chunks(meta_ref, lambda n_rows: run_copy(half, 0, 0, n_rows))

    base = half * (RUN_SLOTS * ROW_SLABS)
    rows = jnp.concatenate(
        [gstage[pl.ds(base + c, RUN_SLOTS, stride=ROW_SLABS), :] for c in range(ROW_SLABS)], axis=1)
    route = route_ref[...]
    s_i = lax.broadcasted_iota(jnp.int32, (t, RUN_SLOTS), 1).astype(F32)
    wmat = jnp.zeros((t, RUN_SLOTS), F32)
    for k in range(TOP_K):
        wmat = jnp.where(s_i == route[:, TOP_K + k:TOP_K + k + 1], route[:, k:k + 1], wmat)
    rows_hi = rows.astype(BF16)
    rows_lo = (rows - rows_hi.astype(F32)).astype(BF16)
    w_hi = wmat.astype(BF16)
    w_lo = (wmat - w_hi.astype(F32)).astype(BF16)
    y = (jnp.dot(w_hi, rows_hi, preferred_element_type=F32)
         + jnp.dot(w_lo, rows_hi, preferred_element_type=F32)
         + jnp.dot(w_hi, rows_lo, preferred_element_type=F32))
    x2 = x1_ref[...] + y
    hp = _rms(x2, gple_ref[...]).astype(BF16)
    gate = jax.nn.sigmoid(jnp.dot(hp, wpg_ref[...], preferred_element_type=F32))
    emb = jnp.dot(p_ref[...].astype(BF16), wpp_ref[...], preferred_element_type=F32)
    o_ref[...] = x2 + gate * emb


def _rope_tables(s, half, x1_starts, x2_starts):
    inv_freq = ROPE_THETA ** (-(np.arange(half, dtype=np.float64) / half))
    ang = np.arange(s, dtype=np.float64)[:, None] * inv_freq[None, :]
    cos, sin = np.cos(ang), np.sin(ang)
    cos_t, sin_t = np.ones((s, LANES), np.float32), np.zeros((s, LANES), np.float32)
    for st in x1_starts:
        cos_t[:, st:st + half] = cos
        sin_t[:, st:st + half] = -sin
    for st in x2_starts:
        cos_t[:, st:st + half] = cos
        sin_t[:, st:st + half] = sin
    return jnp.asarray(cos_t), jnp.asarray(sin_t)


_MOBA_LANE_COLS = tuple(list(range(0, 32)) + list(range(64, 96)) + list(range(32, 64))
                        + list(range(96, 128)))
_MLA_LANE_DIMS = tuple(list(range(80, 96)) + list(range(0, 48)) + list(range(64, 80))
                       + list(range(48, 64)) + [MLA_QK_DIM] * 32)


def _moba_lanes(w):
    k, width = w.shape
    cols = jnp.asarray(_MOBA_LANE_COLS, jnp.int32)
    return w.reshape(k, width // LANES, LANES)[:, :, cols].reshape(k, width)


def _mla_lanes(w, heads):
    k = w.shape[0]
    w = jnp.pad(w.reshape(k, heads, MLA_QK_DIM), ((0, 0), (0, 0), (0, 1)))
    return w[:, :, jnp.asarray(_MLA_LANE_DIMS, jnp.int32)].reshape(k, heads * LANES)


def _row(v):
    return v.reshape(1, -1).astype(F32)


def _layer(x, p_i, g_mix, w_in, moba_q_norm, moba_k_norm, mla_q_lat_norm, w_uq, mla_kv_lat_norm,
           w_ukv, mla_q_norm, mla_k_norm, w_branch_a, w_branch_b, w_out, g_ffn, w_router, b_router,
           w_gate_up, b_gate_up, w_down, b_down, g_ple, w_ple_gate, w_ple_proj):
    b, s, d = x.shape
    n = b * s
    assert d == D_MODEL and s % ATTN_TILE == 0
    assert s // MOBA_BLOCK <= GATE_ROWS and TOKEN_TILE == ATTN_TILE == MOBA_BLOCK
    assert D_EXPERT == D_MODEL and RUN_CHUNK == SUBLANES
    n_tiles = n // TOKEN_TILE
    tiles_per_seq = s // TOKEN_TILE
    xf = x.reshape(n, d)

    off = [0]
    for wdt in (MOBA_WIDTH, MOBA_WIDTH, MOBA_WIDTH, MLA_Q_LORA, MLA_KV_LORA, MLA_ROPE_DIM, D_MODEL, D_MODEL):
        off.append(off[-1] + wdt)
    seg = [w_in[:, off[i]:off[i + 1]] for i in range(8)]
    kpe_cols = _mla_lanes(jnp.pad(seg[5], ((0, 0), (MLA_NOPE_DIM, 0))), 1)
    w_in_p = jnp.concatenate([_moba_lanes(seg[0]), _moba_lanes(seg[1])] + seg[2:5] + [kpe_cols]
                             + seg[6:], axis=1).astype(BF16)
    assert w_in_p.shape[1] == D_IN_PACKED
    w_uq_p = _mla_lanes(w_uq, MLA_HEADS).astype(BF16)
    w_ukv_h = w_ukv.reshape(MLA_KV_LORA, MLA_HEADS, MLA_NOPE_DIM + MLA_V_DIM)
    w_uk_p = _mla_lanes(jnp.pad(w_ukv_h[:, :, :MLA_NOPE_DIM], ((0, 0), (0, 0), (0, MLA_ROPE_DIM)))
                        .reshape(MLA_KV_LORA, -1), MLA_HEADS).astype(BF16)
    w_uv = w_ukv_h[:, :, MLA_NOPE_DIM:].reshape(MLA_KV_LORA, MLA_WIDTH).astype(BF16)
    gqa = _moba_lanes(_row(jnp.tile(moba_q_norm, 2))) * (MOBA_HEAD_DIM ** -0.5 * LOG2_E)
    gka = _moba_lanes(_row(jnp.tile(moba_k_norm, 2)))
    gqb = _mla_lanes(_row(mla_q_norm), 1) * (MLA_QK_DIM ** -0.5 * LOG2_E)
    gkb = _mla_lanes(_row(mla_k_norm), 1)
    half_a, half_b = MOBA_HEAD_DIM // 2, MLA_ROPE_DIM // 2
    cosa, sina = _rope_tables(s, half_a, (0, half_a), (LANES // 2, LANES // 2 + half_a))
    cosb, sinb = _rope_tables(s, half_b, (LANES // 2,), (0,))

    tok = lambda width: pl.BlockSpec((TOKEN_TILE, width), lambda i: (i, 0))
    whole = lambda arr: pl.BlockSpec(arr.shape, lambda i: (0,) * arr.ndim)
    seq_tab = pl.BlockSpec((TOKEN_TILE, LANES), lambda i: (i % tiles_per_seq, 0))
    vt_spec = pl.BlockSpec((1, MOBA_WIDTH, TOKEN_TILE), lambda i: (i, 0, 0))
    k_tiles_spec = lambda tiles: pl.BlockSpec((tiles, TOKEN_TILE, LANES), lambda i: (0, i, 0))
    params = pltpu.CompilerParams(dimension_semantics=("arbitrary",), vmem_limit_bytes=VMEM_LIMIT)

    consts1 = [_row(g_mix), w_in_p, gqa, gka]
    consts2 = [_row(mla_q_lat_norm), w_uq_p, _row(mla_kv_lat_norm), w_uk_p, w_uv, gqb, gkb]
    qa, ka, va, kmean, qb, kb, vb, ga, gb = pl.pallas_call(
        _inproj_kernel,
        grid=(n_tiles,),
        in_specs=([tok(d)] + [whole(a) for a in consts1] + [seq_tab] * 2
                  + [whole(a) for a in consts2] + [seq_tab] * 2),
        out_specs=[tok(MOBA_WIDTH), k_tiles_spec(MOBA_WIDTH // LANES), vt_spec,
                   pl.BlockSpec((1, 1, MOBA_WIDTH), lambda i: (i, 0, 0)),
                   tok(MLA_HEADS * LANES), k_tiles_spec(MLA_HEADS), vt_spec,
                   tok(d), tok(d)],
        out_shape=[jax.ShapeDtypeStruct((n, MOBA_WIDTH), BF16),
                   jax.ShapeDtypeStruct((MOBA_WIDTH // LANES, n, LANES), BF16),
                   jax.ShapeDtypeStruct((n_tiles, MOBA_WIDTH, TOKEN_TILE), BF16),
                   jax.ShapeDtypeStruct((n_tiles, 1, MOBA_WIDTH), F32),
                   jax.ShapeDtypeStruct((n, MLA_HEADS * LANES), BF16),
                   jax.ShapeDtypeStruct((MLA_HEADS, n, LANES), BF16),
                   jax.ShapeDtypeStruct((n_tiles, MLA_WIDTH, TOKEN_TILE), BF16)]
        + [jax.ShapeDtypeStruct((n, d), BF16)] * 2,
        compiler_params=params,
        name="in_projection",
    )(xf, *consts1, cosa, sina, *consts2, cosb, sinb)

    kmean = kmean.reshape(b, tiles_per_seq, MOBA_WIDTH)
    kmean = jnp.pad(kmean, ((0, 0), (0, GATE_ROWS - tiles_per_seq), (0, 0))).astype(BF16)

    r3 = lambda a: a.reshape(b, s, a.shape[-1])
    r4 = lambda a: a.reshape(b, tiles_per_seq, a.shape[1], TOKEN_TILE)
    ya = _attention(r3(qa), ka, r4(va), kmean, moba=True).reshape(n, MOBA_WIDTH)
    yb = _attention(r3(qb), kb, r4(vb), None, moba=False).reshape(n, MLA_WIDTH)

    wr_p = jnp.pad(w_router, ((0, 0), (0, LANES - N_EXPERTS))).astype(BF16)
    br_p = jnp.pad(b_router.astype(F32), (0, LANES - N_EXPERTS), constant_values=NEG).reshape(1, LANES)
    consts3 = [w_branch_a.astype(BF16), w_branch_b.astype(BF16), w_out.astype(BF16), _row(g_ffn),
               wr_p, br_p]
    x1, h_ffn, route, meta, cnt = pl.pallas_call(
        _merge_kernel,
        grid=(n_tiles,),
        in_specs=[tok(d), tok(MOBA_WIDTH), tok(MLA_WIDTH), tok(d), tok(d)]
        + [whole(a) for a in consts3],
        out_specs=[tok(d), tok(d), tok(LANES),
                   pl.BlockSpec((1, SUBLANES, LANES), lambda i: (i, 0, 0)),
                   pl.BlockSpec((SUBLANES, LANES), lambda i: (0, 0))],
        out_shape=[jax.ShapeDtypeStruct((n, d), F32),
                   jax.ShapeDtypeStruct((n, d), BF16),
                   jax.ShapeDtypeStruct((n, LANES), F32),
                   jax.ShapeDtypeStruct((n_tiles, SUBLANES, LANES), F32),
                   jax.ShapeDtypeStruct((SUBLANES, LANES), F32)],
        scratch_shapes=[pltpu.VMEM((SUBLANES, LANES), F32)],
        compiler_params=params,
        name="merge_router",
    )(xf, ya, yb, ga, gb, *consts3)

    rb = EXPERT_ROWS
    n_blocks = -(-(n * TOP_K + N_EXPERTS * (RUN_CHUNK - 1)) // rb) + N_EXPERTS
    n_rows = n_blocks * rb
    counts = cnt[0, :N_EXPERTS].astype(jnp.int32)
    pcounts = ((counts + (RUN_CHUNK - 1) + rb - 1) // rb) * rb
    pends = jnp.cumsum(pcounts).astype(jnp.int32)
    pstarts = (pends - pcounts).astype(jnp.int32)
    nact = (pends[-1] // rb).astype(jnp.int32).reshape(1)
    blk = jnp.minimum(jnp.arange(n_blocks, dtype=jnp.int32), nact[0] - 1)
    blk_e = jnp.sum((pends[None, :] <= (blk * rb)[:, None]).astype(jnp.int32), axis=1)
    blk_e = jnp.minimum(blk_e, N_EXPERTS - 1)
    meta_i = meta[:, :4, :].astype(jnp.int32).reshape(n_tiles * 4 * LANES)
    meta_spec = lambda shift: pl.BlockSpec(
        (4 * LANES,), lambda i, *_: (jnp.clip(i + shift, 0, n_tiles - 1),), memory_space=pltpu.SMEM)

    xr = pl.pallas_call(
        _dispatch_kernel,
        grid_spec=pltpu.PrefetchScalarGridSpec(
            num_scalar_prefetch=3,
            grid=(n_tiles,),
            in_specs=[meta_spec(0), meta_spec(-1),
                      pl.BlockSpec((TOKEN_TILE, LANES), lambda i, *_: (i, 0)),
                      pl.BlockSpec((TOKEN_TILE, d), lambda i, *_: (i, 0))],
            out_specs=pl.BlockSpec(memory_space=pl.ANY),
            scratch_shapes=[pltpu.VMEM((2 * RUN_SLOTS * ROW_SLABS, LANES), F32),
                            pltpu.VMEM((EXPERT_ROWS * ROW_SLABS, LANES), F32),
                            pltpu.SemaphoreType.DMA((2,)), pltpu.SemaphoreType.DMA(())]),
        out_shape=jax.ShapeDtypeStruct((n_rows * ROW_SLABS, LANES), F32),
        compiler_params=params,
        name="dispatch_rows",
    )(pstarts, pends, nact, meta_i, meta_i, route, h_ffn)

    act_blk = lambda i, be, na: jnp.minimum(i, na[0] - 1)
    yr = pl.pallas_call(
        _expert_kernel,
        grid_spec=pltpu.PrefetchScalarGridSpec(
            num_scalar_prefetch=2,
            grid=(n_blocks,),
            in_specs=[
                pl.BlockSpec((rb * ROW_SLABS, LANES), lambda i, be, na: (act_blk(i, be, na), 0)),
                pl.BlockSpec(memory_space=pl.ANY),
                pl.BlockSpec((1, 1, 2 * D_EXPERT), lambda i, be, na: (be[i], 0, 0)),
                pl.BlockSpec(memory_space=pl.ANY),
                pl.BlockSpec((1, 1, d), lambda i, be, na: (be[i], 0, 0)),
            ],
            out_specs=pl.BlockSpec((rb * ROW_SLABS, LANES), lambda i, be, na: (i, 0)),
            scratch_shapes=[pltpu.VMEM((2, d, 2 * D_EXPERT), F32), pltpu.VMEM((2, D_EXPERT, d), F32),
                            pltpu.VMEM((d, 2 * D_EXPERT), BF16), pltpu.VMEM((D_EXPERT, d), BF16),
                            pltpu.SemaphoreType.DMA((2, 2))]),
        out_shape=jax.ShapeDtypeStruct((n_rows * ROW_SLABS, LANES), F32),
        compiler_params=params,
        name="experts",
    )(blk_e, nact, xr, w_gate_up.astype(F32),
      b_gate_up.reshape(N_EXPERTS, 1, -1).astype(F32), w_down.astype(F32),
      b_down.reshape(N_EXPERTS, 1, -1).astype(F32))

    ftok = lambda width: pl.BlockSpec((TOKEN_TILE, width), lambda i, ps: (i, 0))
    fwhole = lambda arr: pl.BlockSpec(arr.shape, lambda i, ps: (0,) * arr.ndim)
    consts4 = [_row(g_ple), w_ple_gate.astype(BF16), w_ple_proj.astype(BF16)]
    out = pl.pallas_call(
        _final_kernel,
        grid_spec=pltpu.PrefetchScalarGridSpec(
            num_scalar_prefetch=1,
            grid=(n_tiles,),
            in_specs=[meta_spec(0), meta_spec(1), ftok(d), ftok(LANES), ftok(PLE_DIM)]
            + [fwhole(a) for a in consts4] + [pl.BlockSpec(memory_space=pl.ANY)],
            out_specs=ftok(d),
            scratch_shapes=[pltpu.VMEM((2 * RUN_SLOTS * ROW_SLABS, LANES), F32),
                            pltpu.SemaphoreType.DMA((2,))]),
        out_shape=jax.ShapeDtypeStruct((n, d), F32),
        compiler_params=params,
        name="combine_ple",
    )(pstarts, meta_i, meta_i, x1, route, p_i.reshape(n, PLE_DIM), *consts4, yr)
    return out.reshape(b, s, d)


def kernel(x, p, g_mix, w_in, moba_q_norm, moba_k_norm, mla_q_lat_norm, w_uq, mla_kv_lat_norm, w_ukv, mla_q_norm, mla_k_norm, w_branch_a, w_branch_b, w_out, g_ffn, w_router, b_router, w_gate_up, b_gate_up, w_down, b_down, g_ple, w_ple_gate, w_ple_proj):
    for i in range(p.shape[0]):
        x = _layer(x, p[i], g_mix[i], w_in[i], moba_q_norm[i], moba_k_norm[i], mla_q_lat_norm[i],
                   w_uq[i], mla_kv_lat_norm[i], w_ukv[i], mla_q_norm[i], mla_k_norm[i],
                   w_branch_a[i], w_branch_b[i], w_out[i], g_ffn[i], w_router[i], b_router[i],
                   w_gate_up[i], b_gate_up[i], w_down[i], b_down[i], g_ple[i], w_ple_gate[i],
                   w_ple_proj[i])
    return x
```

```python
import functools

import numpy as np
import jax
import jax.numpy as jnp
from jax import lax
from jax.experimental import pallas as pl
from jax.experimental.pallas import tpu as pltpu

F32 = jnp.float32
BF16 = jnp.bfloat16

D_MODEL = 1024
PLE_DIM = 256
EPS = 1e-6
ROPE_THETA = 10000.0
MOBA_HEADS = 8
MOBA_HEAD_DIM = 64
MOBA_BLOCK = 256
MOBA_TOPK = 3
MOBA_WIDTH = MOBA_HEADS * MOBA_HEAD_DIM
MLA_HEADS = 8
MLA_Q_LORA = 256
MLA_KV_LORA = 128
MLA_NOPE_DIM = 64
MLA_ROPE_DIM = 32
MLA_V_DIM = 64
MLA_QK_DIM = MLA_NOPE_DIM + MLA_ROPE_DIM
MLA_WIDTH = MLA_HEADS * MLA_V_DIM
N_EXPERTS = 32
TOP_K = 4
D_EXPERT = 1024
SWIGLU_LIMIT = 7.0
SWIGLU_ALPHA = 1.702

LANES = 128
SUBLANES = 8
ROW_SLABS = D_MODEL // LANES
VMEM_LIMIT = 56 * 1024 * 1024

TOKEN_TILE = 256
ATTN_TILE = 256
ATTN_GROUPS = 4
EXPERT_ROWS = 256
RUN_CHUNK = 8
RUN_SLOTS = -(-(TOKEN_TILE * TOP_K + N_EXPERTS * (RUN_CHUNK - 1)) // 256) * 256

LOG2_E = 1.4426950408889634
NEG = -1e30
MASK_BIAS = -1e9

C_QA, C_KA = 0, 512
C_CQ, C_CKV, C_KPE = 1024, 1280, 1408
C_GA, C_GB = 1536, 2560
D_IN_PACKED = 3584


def _rms(x, gain):
    return x * lax.rsqrt(jnp.mean(x * x, axis=-1, keepdims=True) + EPS) * gain


def _rope(t, cos, sin):
    return t * cos + pltpu.roll(t, LANES // 2, 1) * sin


def _moba_even_head(lane):
    return (lane & (MOBA_HEAD_DIM // 2)) == 0


def _inproj_kernel(x_ref, gmix_ref, win_ref, wvat_ref, gqa_ref, gka_ref, cosa_ref, sina_ref,
                   gql_ref, wuq_ref, gkvl_ref, wuk_ref, wuvt_ref, gqb_ref, gkb_ref,
                   cosb_ref, sinb_ref,
                   qa_ref, ka_ref, vat_ref, kmean_ref, qb_ref, kb_ref, vbt_ref, ga_ref, gb_ref):
    hn = _rms(x_ref[...], gmix_ref[...]).astype(BF16)

    def proj(c0, width):
        return jnp.dot(hn, win_ref[:, c0:c0 + width], preferred_element_type=F32)

    first = _moba_even_head(lax.broadcasted_iota(jnp.int32, (TOKEN_TILE, LANES), 1))
    cosa, sina = cosa_ref[...], sina_ref[...]

    def moba_norm_rope(t, gain):
        sq = t * t
        ss0 = jnp.sum(jnp.where(first, sq, 0.0), axis=-1, keepdims=True)
        ss1 = jnp.sum(jnp.where(first, 0.0, sq), axis=-1, keepdims=True)
        ms = jnp.where(first, ss0, ss1) * (1.0 / MOBA_HEAD_DIM)
        t = t * lax.rsqrt(ms + EPS) * gain
        return _rope(t, cosa, sina)

    qa = proj(C_QA, MOBA_WIDTH)
    ka = proj(C_KA, MOBA_WIDTH)
    for c in range(MOBA_WIDTH // LANES):
        sl = slice(c * LANES, (c + 1) * LANES)
        qa_ref[:, sl] = moba_norm_rope(qa[:, sl], gqa_ref[...]).astype(BF16)
        kc = moba_norm_rope(ka[:, sl], gka_ref[...])
        ka_ref[c] = kc.astype(BF16)
        kmean_ref[0, :, sl] = jnp.mean(kc, axis=0, keepdims=True)
    vat_ref[0] = lax.dot_general(wvat_ref[...], hn, _NT, preferred_element_type=F32).astype(BF16)

    cosb, sinb = cosb_ref[...], sinb_ref[...]

    def mla_norm_rope(t, gain):
        ms = jnp.sum(t * t, axis=-1, keepdims=True) * (1.0 / MLA_QK_DIM)
        t = t * lax.rsqrt(ms + EPS) * gain
        return _rope(t, cosb, sinb)

    cq = _rms(proj(C_CQ, MLA_Q_LORA), gql_ref[...]).astype(BF16)
    qb = jnp.dot(cq, wuq_ref[...], preferred_element_type=F32)
    ckv = _rms(proj(C_CKV, MLA_KV_LORA), gkvl_ref[...]).astype(BF16)
    kn = jnp.dot(ckv, wuk_ref[...], preferred_element_type=F32)
    kpe = proj(C_KPE, LANES)
    for h in range(MLA_HEADS):
        sl = slice(h * LANES, (h + 1) * LANES)
        qb_ref[:, sl] = mla_norm_rope(qb[:, sl], gqb_ref[...]).astype(BF16)
        kb_ref[h] = mla_norm_rope(kn[:, sl] + kpe, gkb_ref[...]).astype(BF16)
    vbt_ref[0] = lax.dot_general(wuvt_ref[...], ckv, _NT, preferred_element_type=F32).astype(BF16)

    ga_ref[...] = jax.nn.sigmoid(proj(C_GA, D_MODEL)).astype(BF16)
    gb_ref[...] = jax.nn.sigmoid(proj(C_GB, D_MODEL)).astype(BF16)


_NT = (((1,), (1,)), ((), ()))


GATE_ROWS = 16


def _attn_kernel(*refs, moba):
    if moba:
        q_ref, k_ref, vt_ref, kmean_ref, o_ref = refs
    else:
        q_ref, k_ref, vt_ref, o_ref = refs
    t = ATTN_TILE
    hd = MOBA_HEAD_DIM
    n_heads = 2 * ATTN_GROUPS
    qi = pl.program_id(2)
    key_i = lax.broadcasted_iota(jnp.int32, (t, t), 0)
    qry_i = lax.broadcasted_iota(jnp.int32, (t, t), 1)

    blk = lax.broadcasted_iota(jnp.int32, (GATE_ROWS, t), 0)
    heads, biases = [], []
    for hh in range(n_heads):
        if moba:
            lane = lax.broadcasted_iota(jnp.int32, (t, LANES), 1)
            even = _moba_even_head(lane)
            head_lanes = even if hh % 2 == 0 else jnp.logical_not(even)
            k_tile = hh // 2
            kcols = slice(k_tile * LANES, (k_tile + 1) * LANES)
            q = jnp.where(head_lanes, q_ref[0, :, kcols], jnp.zeros((), BF16))
            gate = lax.dot_general(kmean_ref[0, :, kcols], q, _NT, preferred_element_type=F32)
            g = jnp.where(blk < qi, gate, -jnp.inf)
            keep = jnp.zeros((GATE_ROWS, t), F32)
            for _ in range(MOBA_TOPK):
                gmax = jnp.max(g, axis=0, keepdims=True)
                pick = jnp.min(jnp.where(g == gmax, blk, GATE_ROWS), axis=0, keepdims=True)
                hit = blk == jnp.where(gmax > -jnp.inf, pick, GATE_ROWS)
                keep = jnp.where(hit, 1.0, keep)
                g = jnp.where(hit, -jnp.inf, g)
            biases.append(jnp.where(keep > 0.0, 0.0, MASK_BIAS))
        else:
            k_tile = hh
            q = q_ref[0, :, hh * LANES:(hh + 1) * LANES]
        heads.append((q, k_tile))

    def update(s, vt_blk, state):
        m_prev, l_prev, acc = state
        m_new = jnp.maximum(m_prev, jnp.max(s, axis=0, keepdims=True))
        alpha = jnp.exp2(m_prev - m_new)
        p = jnp.exp2(s - m_new)
        l_new = alpha * l_prev + jnp.sum(p, axis=0, keepdims=True)
        acc = alpha * acc + jnp.dot(vt_blk, p.astype(BF16), preferred_element_type=F32)
        return m_new, l_new, acc

    def past_block(j, states):
        start = pl.multiple_of(j * t, t)
        scores = [lax.dot_general(k_ref[k_tile, pl.ds(start, t), :], q, _NT,
                                  preferred_element_type=F32) for q, k_tile in heads]
        out = []
        for hh, s in enumerate(scores):
            if moba:
                s = jnp.sum(jnp.where(blk == j, biases[hh], 0.0), axis=0, keepdims=True) + s
            out.append(update(s, vt_ref[0, j, hh * hd:(hh + 1) * hd, :], states[hh]))
        return tuple(out)

    init = (jnp.full((1, t), NEG, F32), jnp.zeros((1, t), F32), jnp.zeros((hd, t), F32))
    states = lax.fori_loop(0, qi, past_block, (init,) * n_heads)

    diag_start = pl.multiple_of(qi * t, t)
    scores = [lax.dot_general(k_ref[k_tile, pl.ds(diag_start, t), :], q, _NT,
                              preferred_element_type=F32) for q, k_tile in heads]
    outs = []
    for hh, s in enumerate(scores):
        s = jnp.where(key_i <= qry_i, s, NEG)
        _, l_fin, acc = update(s, vt_ref[0, qi, hh * hd:(hh + 1) * hd, :], states[hh])
        outs.append(acc / l_fin)
    o_ref[0] = jnp.concatenate(outs, axis=0).T.astype(BF16)


def _attention(q, k, vt, kmean, *, moba):
    b, s, _ = q.shape
    v_cols = ATTN_GROUPS * LANES
    steps = vt.shape[2] // v_cols
    nblk = s // ATTN_TILE
    qk_cols = v_cols if moba else 2 * v_cols
    k_tiles = qk_cols // LANES
    in_specs = [
        pl.BlockSpec((1, ATTN_TILE, qk_cols), lambda bi, gi, qi: (bi, qi, gi)),
        pl.BlockSpec((k_tiles, s, LANES), lambda bi, gi, qi: (gi, bi, 0)),
        pl.BlockSpec((1, nblk, v_cols, ATTN_TILE), lambda bi, gi, qi: (bi, 0, gi, 0)),
    ]
    args = [q, k, vt]
    if moba:
        in_specs.append(pl.BlockSpec((1, GATE_ROWS, v_cols), lambda bi, gi, qi: (bi, 0, gi)))
        args.append(kmean)
    return pl.pallas_call(
        functools.partial(_attn_kernel, moba=moba),
        grid=(b, steps, nblk),
        in_specs=in_specs,
        out_specs=pl.BlockSpec((1, ATTN_TILE, v_cols), lambda bi, gi, qi: (bi, qi, gi)),
        out_shape=jax.ShapeDtypeStruct((b, s, steps * v_cols), BF16),
        compiler_params=pltpu.CompilerParams(
            dimension_semantics=("arbitrary", "arbitrary", "arbitrary"),
            vmem_limit_bytes=VMEM_LIMIT),
        name="moba_attention" if moba else "mla_attention",
    )(*args)


def _merge_kernel(x_ref, ya_ref, yb_ref, ga_ref, gb_ref, wa_ref, wb_ref, wo_ref, gffn_ref,
                  wr_ref, br_ref, x1_ref, h_ref, route_ref, meta_ref, cnt_ref, run_scr):
    t = TOKEN_TILE

    @pl.when(pl.program_id(0) == 0)
    def _():
        run_scr[...] = jnp.zeros_like(run_scr)

    merged = (ga_ref[...].astype(F32) * jnp.dot(ya_ref[...], wa_ref[...], preferred_element_type=F32)
              + gb_ref[...].astype(F32) * jnp.dot(yb_ref[...], wb_ref[...], preferred_element_type=F32))
    x1 = x_ref[...] + jnp.dot(merged.astype(BF16), wo_ref[...], preferred_element_type=F32)
    x1_ref[...] = x1
    h = _rms(x1, gffn_ref[...]).astype(BF16)
    h_ref[...] = h

    logits = lax.dot_general(wr_ref[...], h, _NT, preferred_element_type=F32) + br_ref[...]
    e_i = lax.broadcasted_iota(jnp.int32, (N_EXPERTS, t), 0)
    lg = logits
    hits = []
    top = None
    for r in range(TOP_K):
        gmax = jnp.max(lg, axis=0, keepdims=True)
        pick = jnp.min(jnp.where(lg == gmax, e_i, N_EXPERTS), axis=0, keepdims=True)
        hit = e_i == pick
        if r == 0:
            top = gmax
        hits.append(hit)
        lg = jnp.where(hit, -jnp.inf, lg)
    sel = jnp.where(lg == -jnp.inf, 1.0, 0.0)
    wgt = sel * jnp.exp(logits - top)
    wgt = wgt / jnp.sum(wgt, axis=0, keepdims=True)

    r_i = lax.broadcasted_iota(jnp.int32, (t, t), 0)
    c_i = lax.broadcasted_iota(jnp.int32, (t, t), 1)
    earlier = jnp.where(r_i < c_i, 1.0, 0.0).astype(BF16)
    rank_in_tile = jnp.dot(sel.astype(BF16), earlier, preferred_element_type=F32)
    tcnt = jnp.sum(sel, axis=1, keepdims=True)
    tpad = jnp.floor((tcnt + (RUN_CHUNK - 1)) * (1.0 / RUN_CHUNK)) * RUN_CHUNK
    e_r = lax.broadcasted_iota(jnp.int32, (N_EXPERTS, N_EXPERTS), 0)
    e_c = lax.broadcasted_iota(jnp.int32, (N_EXPERTS, N_EXPERTS), 1)
    before = jnp.where(e_c < e_r, 1.0, 0.0).astype(BF16)
    tbase = jnp.dot(before, jnp.broadcast_to(tpad, (N_EXPERTS, LANES)).astype(BF16),
                    preferred_element_type=F32)
    n_chunks = jnp.sum(tpad, axis=0, keepdims=True) * (1.0 / RUN_CHUNK)
    run = run_scr[...]
    run_new = run + tcnt
    run_scr[...] = run_new
    cnt_ref[...] = run_new
    lane = lax.broadcasted_iota(jnp.int32, (N_EXPERTS, LANES), 1)
    meta_ref[0] = jnp.where(lane == 0, tcnt, jnp.where(lane == 1, run, jnp.where(
        lane == 2, tbase, jnp.where(lane == 3, n_chunks, 0.0))))

    slot_of = rank_in_tile + tbase[:, 0:1]
    row = lax.broadcasted_iota(jnp.int32, (2 * TOP_K, t), 0)
    route = jnp.zeros((2 * TOP_K, t), F32)
    for r in range(TOP_K):
        w_r = jnp.sum(jnp.where(hits[r], wgt, 0.0), axis=0, keepdims=True)
        slot_r = jnp.sum(jnp.where(hits[r], slot_of, 0.0), axis=0, keepdims=True)
        route = jnp.where(row == r, w_r, route)
        route = jnp.where(row == TOP_K + r, slot_r, route)
    route_ref[0] = route


BIG_COPY = 4


def _for_each_run_copy(meta_ref, pstart_ref, fn):
    big_rows = BIG_COPY * RUN_CHUNK

    def per_expert(e, carry):
        n_chunks = lax.shift_right_logical(meta_ref[e] + (RUN_CHUNK - 1), RUN_CHUNK.bit_length() - 1)
        n_big = lax.shift_right_logical(n_chunks, BIG_COPY.bit_length() - 1)
        slot0 = meta_ref[2 * LANES + e]
        row0 = pstart_ref[e] + meta_ref[LANES + e]

        def big(b, cc):
            fn(slot0 + b * big_rows, row0 + b * big_rows, big_rows)
            return cc

        def small(c, cc):
            fn(slot0 + c * RUN_CHUNK, row0 + c * RUN_CHUNK, RUN_CHUNK)
            return cc

        lax.fori_loop(0, n_big, big, 0)
        lax.fori_loop(n_big * BIG_COPY, n_chunks, small, 0)
        return carry

    lax.fori_loop(0, N_EXPERTS, per_expert, 0)


WAIT_BATCH = 16


def _wait_run_chunks(meta_ref, make_copy):
    n_chunks = meta_ref[3 * LANES]
    n_batches = lax.shift_right_logical(n_chunks, WAIT_BATCH.bit_length() - 1)
    lax.fori_loop(0, n_batches, lambda b, c: (make_copy(WAIT_BATCH * RUN_CHUNK).wait(), c)[1], 0)
    lax.fori_loop(0, n_chunks - n_batches * WAIT_BATCH,
                  lambda b, c: (make_copy(RUN_CHUNK).wait(), c)[1], 0)


def _slab_rows(first_row, n_rows):
    return pl.ds(pl.multiple_of(first_row * ROW_SLABS, ROW_SLABS), n_rows * ROW_SLABS)


def _dispatch_kernel(pstart_ref, pend_ref, nact_ref, meta_ref, meta_prev_ref, route_ref, h_ref,
                     xr_hbm, stage, zero_scr, sems, zsem):
    t = TOKEN_TILE
    i = pl.program_id(0)
    half = lax.rem(i, 2)

    def run_copy(buf_half, slot, row, n_rows=RUN_CHUNK):
        src = stage.at[_slab_rows(buf_half * RUN_SLOTS + slot, n_rows), :]
        return pltpu.make_async_copy(src, xr_hbm.at[_slab_rows(row, n_rows), :], sems.at[buf_half])

    @pl.when(i == 0)
    def _():
        zero_scr[...] = jnp.zeros_like(zero_scr)
        n_blocks = xr_hbm.shape[0] // (EXPERT_ROWS * ROW_SLABS)

        def block_copy(first_row):
            return pltpu.make_async_copy(zero_scr, xr_hbm.at[_slab_rows(first_row, EXPERT_ROWS), :], zsem)

        def pad_blocks(e):
            seg_blocks = lax.shift_right_logical(pend_ref[e] - pstart_ref[e],
                                                 EXPERT_ROWS.bit_length() - 1)
            return jnp.minimum(seg_blocks, 2)

        def zero_pad(e, c):
            lax.fori_loop(1, pad_blocks(e) + 1,
                          lambda b, cc: (block_copy(pend_ref[e] - b * EXPERT_ROWS).start(), cc)[1], 0)
            return c

        def wait_pad(e, c):
            lax.fori_loop(0, pad_blocks(e), lambda b, cc: (block_copy(0).wait(), cc)[1], 0)
            return c

        def zero_tail(blk, c):
            block_copy(blk * EXPERT_ROWS).start()
            return c

        lax.fori_loop(0, N_EXPERTS, zero_pad, 0)
        lax.fori_loop(nact_ref[0], n_blocks, zero_tail, 0)
        lax.fori_loop(0, N_EXPERTS, wait_pad, 0)
        lax.fori_loop(0, n_blocks - nact_ref[0], lambda r, c: (block_copy(0).wait(), c)[1], 0)

    slot_rows = route_ref[0, TOP_K:2 * TOP_K, :]
    s_i = lax.broadcasted_iota(jnp.int32, (RUN_SLOTS, t), 0).astype(F32)
    pick = jnp.zeros((RUN_SLOTS, t), F32)
    for k in range(TOP_K):
        pick = jnp.where(s_i == slot_rows[k:k + 1, :], 1.0, pick)
    rows = jnp.dot(pick.astype(BF16), h_ref[...], preferred_element_type=F32)
    base = half * (RUN_SLOTS * ROW_SLABS)
    for c in range(ROW_SLABS):
        stage[pl.ds(base + c, RUN_SLOTS, stride=ROW_SLABS), :] = rows[:, c * LANES:(c + 1) * LANES]

    @pl.when(i > 0)
    def _():
        _wait_run_chunks(meta_prev_ref, lambda n_rows: run_copy(1 - half, 0, 0, n_rows))

    _for_each_run_copy(meta_ref, pstart_ref, lambda s, r, n: run_copy(half, s, r, n).start())

    @pl.when(i == pl.num_programs(0) - 1)
    def _():
        _wait_run_chunks(meta_ref, lambda n_rows: run_copy(half, 0, 0, n_rows))


def _expert_kernel(blk_e_ref, nact_ref, xr_ref, wgu_hbm, bgu_ref, wdn_hbm, bdn_ref, yr_ref,
                   wgu_f32, wdn_f32, wgu_bf, wdn_bf, sems):
    r = EXPERT_ROWS
    i = pl.program_id(0)
    e = blk_e_ref[i]
    active = i < nact_ref[0]
    new_expert = jnp.logical_or(i == 0, e != blk_e_ref[jnp.maximum(i - 1, 0)])

    def weight_copies(expert):
        half = lax.rem(expert, 2)
        return (pltpu.make_async_copy(wgu_hbm.at[expert], wgu_f32.at[half], sems.at[0, half]),
                pltpu.make_async_copy(wdn_hbm.at[expert], wdn_f32.at[half], sems.at[1, half]))

    @pl.when(i == 0)
    def _():
        for cp in weight_copies(e):
            cp.start()

    @pl.when(jnp.logical_and(active, new_expert))
    def _():
        for cp in weight_copies(e):
            cp.wait()

        @pl.when(e + 1 < N_EXPERTS)
        def _():
            for cp in weight_copies(e + 1):
                cp.start()

        half = lax.rem(e, 2)

        def cast_rows(c, carry):
            rows = pl.ds(pl.multiple_of(c * LANES, LANES), LANES)
            wgu_bf[rows, :] = wgu_f32[half, rows, :].astype(BF16)
            wdn_bf[rows, :] = wdn_f32[half, rows, :].astype(BF16)
            return carry

        lax.fori_loop(0, D_MODEL // LANES, cast_rows, 0)

    @pl.when(active)
    def _():
        x = jnp.concatenate(
            [xr_ref[pl.ds(c, r, stride=ROW_SLABS), :] for c in range(ROW_SLABS)], axis=1)
        gu = jnp.dot(x.astype(BF16), wgu_bf[...], preferred_element_type=F32) + bgu_ref[0]
        g = jnp.minimum(gu[:, :D_EXPERT], SWIGLU_LIMIT)
        u = jnp.clip(gu[:, D_EXPERT:], -SWIGLU_LIMIT, SWIGLU_LIMIT)
        act = (u + 1.0) * (g * jax.nn.sigmoid(SWIGLU_ALPHA * g))
        y = jnp.dot(act.astype(BF16), wdn_bf[...], preferred_element_type=F32) + bdn_ref[0]
        for c in range(ROW_SLABS):
            yr_ref[pl.ds(c, r, stride=ROW_SLABS), :] = y[:, c * LANES:(c + 1) * LANES]

    @pl.when(jnp.logical_not(active))
    def _():
        yr_ref[...] = jnp.zeros_like(yr_ref)


def _final_kernel(pstart_ref, meta_ref, meta_next_ref, x1_ref, route_ref, p_ref, gple_ref,
                  wpg_ref, wpp_ref, yr_hbm, o_ref, gstage, sems):
    t = TOKEN_TILE
    i = pl.program_id(0)
    half = lax.rem(i, 2)

    def run_copy(buf_half, slot, row, n_rows=RUN_CHUNK):
        dst = gstage.at[_slab_rows(buf_half * RUN_SLOTS + slot, n_rows), :]
        return pltpu.make_async_copy(yr_hbm.at[_slab_rows(row, n_rows), :], dst, sems.at[buf_half])

    @pl.when(i == 0)
    def _():
        gstage[...] = jnp.zeros_like(gstage)
        _for_each_run_copy(meta_ref, pstart_ref, lambda s, r, n: run_copy(0, s, r, n).start())

    @pl.when(i + 1 < pl.num_programs(0))
    def _():
        _for_each_run_copy(meta_next_ref, pstart_ref,
                           lambda s, r, n: run_copy(1 - half, s, r, n).start())

    _wait_run_chunks(meta_ref, lambda n_rows: run_copy(half, 0, 0, n_rows))

    base = half * (RUN_SLOTS * ROW_SLABS)
    rows = jnp.concatenate(
        [gstage[pl.ds(base + c, RUN_SLOTS, stride=ROW_SLABS), :] for c in range(ROW_SLABS)], axis=1)
    route = route_ref[...]
    s_i = lax.broadcasted_iota(jnp.int32, (t, RUN_SLOTS), 1).astype(F32)
    wmat = jnp.zeros((t, RUN_SLOTS), F32)
    for k in range(TOP_K):
        wmat = jnp.where(s_i == route[:, TOP_K + k:TOP_K + k + 1], route[:, k:k + 1], wmat)
    rows_hi = rows.astype(BF16)
    rows_lo = (rows - rows_hi.astype(F32)).astype(BF16)
    w_hi = wmat.astype(BF16)
    w_lo = (wmat - w_hi.astype(F32)).astype(BF16)
    y = (jnp.dot(w_hi, rows_hi, preferred_element_type=F32)
         + jnp.dot(w_lo, rows_hi, preferred_element_type=F32)
         + jnp.dot(w_hi, rows_lo, preferred_element_type=F32))
    x2 = x1_ref[...] + y
    hp = _rms(x2, gple_ref[...]).astype(BF16)
    gate = jax.nn.sigmoid(jnp.dot(hp, wpg_ref[...], preferred_element_type=F32))
    emb = jnp.dot(p_ref[...].astype(BF16), wpp_ref[...], preferred_element_type=F32)
    o_ref[...] = x2 + gate * emb


def _rope_tables(s, half, x1_starts, x2_starts):
    inv_freq = ROPE_THETA ** (-(np.arange(half, dtype=np.float64) / half))
    ang = np.arange(s, dtype=np.float64)[:, None] * inv_freq[None, :]
    cos, sin = np.cos(ang), np.sin(ang)
    cos_t, sin_t = np.ones((s, LANES), np.float32), np.zeros((s, LANES), np.float32)
    for st in x1_starts:
        cos_t[:, st:st + half] = cos
        sin_t[:, st:st + half] = -sin
    for st in x2_starts:
        cos_t[:, st:st + half] = cos
        sin_t[:, st:st + half] = sin
    return jnp.asarray(cos_t), jnp.asarray(sin_t)


_MOBA_LANE_COLS = tuple(list(range(0, 32)) + list(range(64, 96)) + list(range(32, 64))
                        + list(range(96, 128)))
_MLA_LANE_DIMS = tuple(list(range(80, 96)) + list(range(0, 48)) + list(range(64, 80))
                       + list(range(48, 64)) + [MLA_QK_DIM] * 32)


def _moba_lanes(w):
    k, width = w.shape
    cols = jnp.asarray(_MOBA_LANE_COLS, jnp.int32)
    return w.reshape(k, width // LANES, LANES)[:, :, cols].reshape(k, width)


def _mla_lanes(w, heads):
    k = w.shape[0]
    w = jnp.pad(w.reshape(k, heads, MLA_QK_DIM), ((0, 0), (0, 0), (0, 1)))
    return w[:, :, jnp.asarray(_MLA_LANE_DIMS, jnp.int32)].reshape(k, heads * LANES)


def _row(v):
    return v.reshape(1, -1).astype(F32)


def _layer(x, p_i, g_mix, w_in, moba_q_norm, moba_k_norm, mla_q_lat_norm, w_uq, mla_kv_lat_norm,
           w_ukv, mla_q_norm, mla_k_norm, w_branch_a, w_branch_b, w_out, g_ffn, w_router, b_router,
           w_gate_up, b_gate_up, w_down, b_down, g_ple, w_ple_gate, w_ple_proj):
    b, s, d = x.shape
    n = b * s
    assert d == D_MODEL and s % ATTN_TILE == 0
    assert s // MOBA_BLOCK <= GATE_ROWS and TOKEN_TILE == ATTN_TILE == MOBA_BLOCK
    assert D_EXPERT == D_MODEL and RUN_CHUNK == SUBLANES
    n_tiles = n // TOKEN_TILE
    tiles_per_seq = s // TOKEN_TILE
    xf = x.reshape(n, d)

    off = [0]
    for wdt in (MOBA_WIDTH, MOBA_WIDTH, MOBA_WIDTH, MLA_Q_LORA, MLA_KV_LORA, MLA_ROPE_DIM, D_MODEL, D_MODEL):
        off.append(off[-1] + wdt)
    seg = [w_in[:, off[i]:off[i + 1]] for i in range(8)]
    kpe_cols = _mla_lanes(jnp.pad(seg[5], ((0, 0), (MLA_NOPE_DIM, 0))), 1)
    w_in_p = jnp.concatenate([_moba_lanes(seg[0]), _moba_lanes(seg[1])] + seg[3:5] + [kpe_cols]
                             + seg[6:], axis=1).astype(BF16)
    assert w_in_p.shape[1] == D_IN_PACKED
    w_va_t = seg[2].T.astype(BF16)
    w_uq_p = _mla_lanes(w_uq, MLA_HEADS).astype(BF16)
    w_ukv_h = w_ukv.reshape(MLA_KV_LORA, MLA_HEADS, MLA_NOPE_DIM + MLA_V_DIM)
    w_uk_p = _mla_lanes(jnp.pad(w_ukv_h[:, :, :MLA_NOPE_DIM], ((0, 0), (0, 0), (0, MLA_ROPE_DIM)))
                        .reshape(MLA_KV_LORA, -1), MLA_HEADS).astype(BF16)
    w_uv_t = w_ukv_h[:, :, MLA_NOPE_DIM:].reshape(MLA_KV_LORA, MLA_WIDTH).T.astype(BF16)
    gqa = _moba_lanes(_row(jnp.tile(moba_q_norm, 2))) * (MOBA_HEAD_DIM ** -0.5 * LOG2_E)
    gka = _moba_lanes(_row(jnp.tile(moba_k_norm, 2)))
    gqb = _mla_lanes(_row(mla_q_norm), 1) * (MLA_QK_DIM ** -0.5 * LOG2_E)
    gkb = _mla_lanes(_row(mla_k_norm), 1)
    half_a, half_b = MOBA_HEAD_DIM // 2, MLA_ROPE_DIM // 2
    cosa, sina = _rope_tables(s, half_a, (0, half_a), (LANES // 2, LANES // 2 + half_a))
    cosb, sinb = _rope_tables(s, half_b, (LANES // 2,), (0,))

    tok = lambda width: pl.BlockSpec((TOKEN_TILE, width), lambda i: (i, 0))
    whole = lambda arr: pl.BlockSpec(arr.shape, lambda i: (0,) * arr.ndim)
    seq_tab = pl.BlockSpec((TOKEN_TILE, LANES), lambda i: (i % tiles_per_seq, 0))
    vt_spec = pl.BlockSpec((1, MOBA_WIDTH, TOKEN_TILE), lambda i: (i, 0, 0))
    k_tiles_spec = lambda tiles: pl.BlockSpec((tiles, TOKEN_TILE, LANES), lambda i: (0, i, 0))
    params = pltpu.CompilerParams(dimension_semantics=("arbitrary",), vmem_limit_bytes=VMEM_LIMIT)

    consts1 = [_row(g_mix), w_in_p, w_va_t, gqa, gka]
    consts2 = [_row(mla_q_lat_norm), w_uq_p, _row(mla_kv_lat_norm), w_uk_p, w_uv_t, gqb, gkb]
    qa, ka, va, kmean, qb, kb, vb, ga, gb = pl.pallas_call(
        _inproj_kernel,
        grid=(n_tiles,),
        in_specs=([tok(d)] + [whole(a) for a in consts1] + [seq_tab] * 2
                  + [whole(a) for a in consts2] + [seq_tab] * 2),
        out_specs=[tok(MOBA_WIDTH), k_tiles_spec(MOBA_WIDTH // LANES), vt_spec,
                   pl.BlockSpec((1, 1, MOBA_WIDTH), lambda i: (i, 0, 0)),
                   tok(MLA_HEADS * LANES), k_tiles_spec(MLA_HEADS), vt_spec,
                   tok(d), tok(d)],
        out_shape=[jax.ShapeDtypeStruct((n, MOBA_WIDTH), BF16),
                   jax.ShapeDtypeStruct((MOBA_WIDTH // LANES, n, LANES), BF16),
                   jax.ShapeDtypeStruct((n_tiles, MOBA_WIDTH, TOKEN_TILE), BF16),
                   jax.ShapeDtypeStruct((n_tiles, 1, MOBA_WIDTH), F32),
                   jax.ShapeDtypeStruct((n, MLA_HEADS * LANES), BF16),
                   jax.ShapeDtypeStruct((MLA_HEADS, n, LANES), BF16),
                   jax.ShapeDtypeStruct((n_tiles, MLA_WIDTH, TOKEN_TILE), BF16)]
        + [jax.ShapeDtypeStruct((n, d), BF16)] * 2,
        compiler_params=params,
        name="in_projection",
    )(xf, *consts1, cosa, sina, *consts2, cosb, sinb)

    kmean = kmean.reshape(b, tiles_per_seq, MOBA_WIDTH)
    kmean = jnp.pad(kmean, ((0, 0), (0, GATE_ROWS - tiles_per_seq), (0, 0))).astype(BF16)

    r3 = lambda a: a.reshape(b, s, a.shape[-1])
    r4 = lambda a: a.reshape(b, tiles_per_seq, a.shape[1], TOKEN_TILE)
    ya = _attention(r3(qa), ka, r4(va), kmean, moba=True).reshape(n, MOBA_WIDTH)
    yb = _attention(r3(qb), kb, r4(vb), None, moba=False).reshape(n, MLA_WIDTH)

    consts3 = [w_branch_a.astype(BF16), w_branch_b.astype(BF16), w_out.astype(BF16), _row(g_ffn),
               w_router.T.astype(BF16), b_router.reshape(N_EXPERTS, 1).astype(F32)]
    x1, h_ffn, route_t, meta, cnt = pl.pallas_call(
        _merge_kernel,
        grid=(n_tiles,),
        in_specs=[tok(d), tok(MOBA_WIDTH), tok(MLA_WIDTH), tok(d), tok(d)]
        + [whole(a) for a in consts3],
        out_specs=[tok(d), tok(d),
                   pl.BlockSpec((1, 2 * TOP_K, TOKEN_TILE), lambda i: (i, 0, 0)),
                   pl.BlockSpec((1, N_EXPERTS, LANES), lambda i: (i, 0, 0)),
                   pl.BlockSpec((N_EXPERTS, LANES), lambda i: (0, 0))],
        out_shape=[jax.ShapeDtypeStruct((n, d), F32),
                   jax.ShapeDtypeStruct((n, d), BF16),
                   jax.ShapeDtypeStruct((n_tiles, 2 * TOP_K, TOKEN_TILE), F32),
                   jax.ShapeDtypeStruct((n_tiles, N_EXPERTS, LANES), F32),
                   jax.ShapeDtypeStruct((N_EXPERTS, LANES), F32)],
        scratch_shapes=[pltpu.VMEM((N_EXPERTS, LANES), F32)],
        compiler_params=params,
        name="merge_router",
    )(xf, ya, yb, ga, gb, *consts3)
    route = jnp.pad(route_t.transpose(0, 2, 1).reshape(n, 2 * TOP_K),
                    ((0, 0), (0, LANES - 2 * TOP_K)))

    rb = EXPERT_ROWS
    n_blocks = -(-(n * TOP_K + N_EXPERTS * (RUN_CHUNK - 1)) // rb) + N_EXPERTS
    n_rows = n_blocks * rb
    counts = cnt[:, 0].astype(jnp.int32)
    pcounts = ((counts + (RUN_CHUNK - 1) + rb - 1) // rb) * rb
    pends = jnp.cumsum(pcounts).astype(jnp.int32)
    pstarts = (pends - pcounts).astype(jnp.int32)
    nact = (pends[-1] // rb).astype(jnp.int32).reshape(1)
    blk = jnp.minimum(jnp.arange(n_blocks, dtype=jnp.int32), nact[0] - 1)
    blk_e = jnp.sum((pends[None, :] <= (blk * rb)[:, None]).astype(jnp.int32), axis=1)
    blk_e = jnp.minimum(blk_e, N_EXPERTS - 1)
    meta_i = jnp.pad(meta[:, :, :4].transpose(0, 2, 1), ((0, 0), (0, 0), (0, LANES - N_EXPERTS)))
    meta_i = meta_i.astype(jnp.int32).reshape(n_tiles * 4 * LANES)
    meta_spec = lambda shift: pl.BlockSpec(
        (4 * LANES,), lambda i, *_: (jnp.clip(i + shift, 0, n_tiles - 1),), memory_space=pltpu.SMEM)

    xr = pl.pallas_call(
        _dispatch_kernel,
        grid_spec=pltpu.PrefetchScalarGridSpec(
            num_scalar_prefetch=3,
            grid=(n_tiles,),
            in_specs=[meta_spec(0), meta_spec(-1),
                      pl.BlockSpec((1, 2 * TOP_K, TOKEN_TILE), lambda i, *_: (i, 0, 0)),
                      pl.BlockSpec((TOKEN_TILE, d), lambda i, *_: (i, 0))],
            out_specs=pl.BlockSpec(memory_space=pl.ANY),
            scratch_shapes=[pltpu.VMEM((2 * RUN_SLOTS * ROW_SLABS, LANES), F32),
                            pltpu.VMEM((EXPERT_ROWS * ROW_SLABS, LANES), F32),
                            pltpu.SemaphoreType.DMA((2,)), pltpu.SemaphoreType.DMA(())]),
        out_shape=jax.ShapeDtypeStruct((n_rows * ROW_SLABS, LANES), F32),
        compiler_params=params,
        name="dispatch_rows",
    )(pstarts, pends, nact, meta_i, meta_i, route_t, h_ffn)

    act_blk = lambda i, be, na: jnp.minimum(i, na[0] - 1)
    yr = pl.pallas_call(
        _expert_kernel,
        grid_spec=pltpu.PrefetchScalarGridSpec(
            num_scalar_prefetch=2,
            grid=(n_blocks,),
            in_specs=[
                pl.BlockSpec((rb * ROW_SLABS, LANES), lambda i, be, na: (act_blk(i, be, na), 0)),
                pl.BlockSpec(memory_space=pl.ANY),
                pl.BlockSpec((1, 1, 2 * D_EXPERT), lambda i, be, na: (be[i], 0, 0)),
                pl.BlockSpec(memory_space=pl.ANY),
                pl.BlockSpec((1, 1, d), lambda i, be, na: (be[i], 0, 0)),
            ],
            out_specs=pl.BlockSpec((rb * ROW_SLABS, LANES), lambda i, be, na: (i, 0)),
            scratch_shapes=[pltpu.VMEM((2, d, 2 * D_EXPERT), F32), pltpu.VMEM((2, D_EXPERT, d), F32),
                            pltpu.VMEM((d, 2 * D_EXPERT), BF16), pltpu.VMEM((D_EXPERT, d), BF16),
                            pltpu.SemaphoreType.DMA((2, 2))]),
        out_shape=jax.ShapeDtypeStruct((n_rows * ROW_SLABS, LANES), F32),
        compiler_params=params,
        name="experts",
    )(blk_e, nact, xr, w_gate_up.astype(F32),
      b_gate_up.reshape(N_EXPERTS, 1, -1).astype(F32), w_down.astype(F32),
      b_down.reshape(N_EXPERTS, 1, -1).astype(F32))

    ftok = lambda width: pl.BlockSpec((TOKEN_TILE, width), lambda i, ps: (i, 0))
    fwhole = lambda arr: pl.BlockSpec(arr.shape, lambda i, ps: (0,) * arr.ndim)
    consts4 = [_row(g_ple), w_ple_gate.astype(BF16), w_ple_proj.astype(BF16)]
    out = pl.pallas_call(
        _final_kernel,
        grid_spec=pltpu.PrefetchScalarGridSpec(
            num_scalar_prefetch=1,
            grid=(n_tiles,),
            in_specs=[meta_spec(0), meta_spec(1), ftok(d), ftok(LANES), ftok(PLE_DIM)]
            + [fwhole(a) for a in consts4] + [pl.BlockSpec(memory_space=pl.ANY)],
            out_specs=ftok(d),
            scratch_shapes=[pltpu.VMEM((2 * RUN_SLOTS * ROW_SLABS, LANES), F32),
                            pltpu.SemaphoreType.DMA((2,))]),
        out_shape=jax.ShapeDtypeStruct((n, d), F32),
        compiler_params=params,
        name="combine_ple",
    )(pstarts, meta_i, meta_i, x1, route, p_i.reshape(n, PLE_DIM), *consts4, yr)
    return out.reshape(b, s, d)


def kernel(x, p, g_mix, w_in, moba_q_norm, moba_k_norm, mla_q_lat_norm, w_uq, mla_kv_lat_norm, w_ukv, mla_q_norm, mla_k_norm, w_branch_a, w_branch_b, w_out, g_ffn, w_router, b_router, w_gate_up, b_gate_up, w_down, b_down, g_ple, w_ple_gate, w_ple_proj):
    for i in range(p.shape[0]):
        x = _layer(x, p[i], g_mix[i], w_in[i], moba_q_norm[i], moba_k_norm[i], mla_q_lat_norm[i],
                   w_uq[i], mla_kv_lat_norm[i], w_ukv[i], mla_q_norm[i], mla_k_norm[i],
                   w_branch_a[i], w_branch_b[i], w_out[i], g_ffn[i], w_router[i], b_router[i],
                   w_gate_up[i], b_gate_up[i], w_down[i], b_down[i], g_ple[i], w_ple_gate[i],
                   w_ple_proj[i])
    return x
```

```python
import functools

import numpy as np
import jax
import jax.numpy as jnp
from jax import lax
from jax.experimental import pallas as pl
from jax.experimental.pallas import tpu as pltpu

F32 = jnp.float32
BF16 = jnp.bfloat16

D_MODEL = 1024
PLE_DIM = 256
EPS = 1e-6
ROPE_THETA = 10000.0
MOBA_HEADS = 8
MOBA_HEAD_DIM = 64
MOBA_BLOCK = 256
MOBA_TOPK = 3
MOBA_WIDTH = MOBA_HEADS * MOBA_HEAD_DIM
MLA_HEADS = 8
MLA_Q_LORA = 256
MLA_KV_LORA = 128
MLA_NOPE_DIM = 64
MLA_ROPE_DIM = 32
MLA_V_DIM = 64
MLA_QK_DIM = MLA_NOPE_DIM + MLA_ROPE_DIM
MLA_WIDTH = MLA_HEADS * MLA_V_DIM
N_EXPERTS = 32
TOP_K = 4
D_EXPERT = 1024
SWIGLU_LIMIT = 7.0
SWIGLU_ALPHA = 1.702

LANES = 128
SUBLANES = 8
ROW_SLABS = D_MODEL // LANES
VMEM_LIMIT = 56 * 1024 * 1024

TOKEN_TILE = 256
ATTN_TILE = 256
ATTN_GROUPS = 4
EXPERT_ROWS = 256
RUN_CHUNK = 8
RUN_SLOTS = -(-(TOKEN_TILE * TOP_K + N_EXPERTS * (RUN_CHUNK - 1)) // 256) * 256

LOG2_E = 1.4426950408889634
LOGIT_LIMIT = 60.0
NEG = -1e30
MASK_BIAS = -1e9

C_QA, C_KA = 0, 512
C_CQ, C_CKV, C_KPE = 1024, 1280, 1408
C_GA, C_GB = 1536, 2560
D_IN_PACKED = 3584


def _rms(x, gain):
    return x * lax.rsqrt(jnp.mean(x * x, axis=-1, keepdims=True) + EPS) * gain


def _rope(t, cos, sin):
    return t * cos + pltpu.roll(t, LANES // 2, 1) * sin


def _moba_even_head(lane):
    return (lane & (MOBA_HEAD_DIM // 2)) == 0


def _inproj_kernel(x_ref, gmix_ref, win_ref, wvat_ref, gqa_ref, gka_ref, cosa_ref, sina_ref,
                   gql_ref, wuq_ref, gkvl_ref, wuk_ref, wuvt_ref, gqb_ref, gkb_ref,
                   cosb_ref, sinb_ref,
                   qa_ref, ka_ref, vat_ref, kmean_ref, qb_ref, kb_ref, vbt_ref, ga_ref, gb_ref):
    hn = _rms(x_ref[...], gmix_ref[...]).astype(BF16)

    def proj(c0, width):
        return jnp.dot(hn, win_ref[:, c0:c0 + width], preferred_element_type=F32)

    first = _moba_even_head(lax.broadcasted_iota(jnp.int32, (TOKEN_TILE, LANES), 1))
    cosa, sina = cosa_ref[...], sina_ref[...]

    def moba_norm_rope(t, gain):
        sq = t * t
        ss0 = jnp.sum(jnp.where(first, sq, 0.0), axis=-1, keepdims=True)
        ss1 = jnp.sum(jnp.where(first, 0.0, sq), axis=-1, keepdims=True)
        ms = jnp.where(first, ss0, ss1) * (1.0 / MOBA_HEAD_DIM)
        t = t * lax.rsqrt(ms + EPS) * gain
        return _rope(t, cosa, sina)

    qa = proj(C_QA, MOBA_WIDTH)
    ka = proj(C_KA, MOBA_WIDTH)
    for c in range(MOBA_WIDTH // LANES):
        sl = slice(c * LANES, (c + 1) * LANES)
        qa_ref[:, sl] = moba_norm_rope(qa[:, sl], gqa_ref[...]).astype(BF16)
        kc = moba_norm_rope(ka[:, sl], gka_ref[...])
        ka_ref[c] = kc.astype(BF16)
        kmean_ref[0, :, sl] = jnp.mean(kc, axis=0, keepdims=True)
    vat_ref[0] = lax.dot_general(wvat_ref[...], hn, _NT, preferred_element_type=F32).astype(BF16)

    cosb, sinb = cosb_ref[...], sinb_ref[...]

    def mla_norm_rope(t, gain):
        ms = jnp.sum(t * t, axis=-1, keepdims=True) * (1.0 / MLA_QK_DIM)
        t = t * lax.rsqrt(ms + EPS) * gain
        return _rope(t, cosb, sinb)

    cq = _rms(proj(C_CQ, MLA_Q_LORA), gql_ref[...]).astype(BF16)
    qb = jnp.dot(cq, wuq_ref[...], preferred_element_type=F32)
    ckv = _rms(proj(C_CKV, MLA_KV_LORA), gkvl_ref[...]).astype(BF16)
    kn = jnp.dot(ckv, wuk_ref[...], preferred_element_type=F32)
    kpe = proj(C_KPE, LANES)
    for h in range(MLA_HEADS):
        sl = slice(h * LANES, (h + 1) * LANES)
        qb_ref[:, sl] = mla_norm_rope(qb[:, sl], gqb_ref[...]).astype(BF16)
        kb_ref[h] = mla_norm_rope(kn[:, sl] + kpe, gkb_ref[...]).astype(BF16)
    vbt_ref[0] = lax.dot_general(wuvt_ref[...], ckv, _NT, preferred_element_type=F32).astype(BF16)

    ga_ref[...] = jax.nn.sigmoid(proj(C_GA, D_MODEL)).astype(BF16)
    gb_ref[...] = jax.nn.sigmoid(proj(C_GB, D_MODEL)).astype(BF16)


_NT = (((1,), (1,)), ((), ()))


GATE_ROWS = 16


def _attn_kernel(*refs, moba):
    if moba:
        bounded_ref, q_ref, k_ref, vt_ref, kmean_ref, o_ref = refs
    else:
        bounded_ref, q_ref, k_ref, vt_ref, o_ref = refs
    t = ATTN_TILE
    hd = MOBA_HEAD_DIM
    n_heads = 2 * ATTN_GROUPS
    qi = pl.program_id(2)
    key_i = lax.broadcasted_iota(jnp.int32, (t, t), 0)
    qry_i = lax.broadcasted_iota(jnp.int32, (t, t), 1)

    blk = lax.broadcasted_iota(jnp.int32, (GATE_ROWS, t), 0)
    heads, biases = [], []
    for hh in range(n_heads):
        if moba:
            lane = lax.broadcasted_iota(jnp.int32, (t, LANES), 1)
            even = _moba_even_head(lane)
            head_lanes = even if hh % 2 == 0 else jnp.logical_not(even)
            k_tile = hh // 2
            kcols = slice(k_tile * LANES, (k_tile + 1) * LANES)
            q = jnp.where(head_lanes, q_ref[0, :, kcols], jnp.zeros((), BF16))
            gate = lax.dot_general(kmean_ref[0, :, kcols], q, _NT, preferred_element_type=F32)
            g = jnp.where(blk < qi, gate, -jnp.inf)
            keep = jnp.zeros((GATE_ROWS, t), F32)
            for _ in range(MOBA_TOPK):
                gmax = jnp.max(g, axis=0, keepdims=True)
                pick = jnp.min(jnp.where(g == gmax, blk, GATE_ROWS), axis=0, keepdims=True)
                hit = blk == jnp.where(gmax > -jnp.inf, pick, GATE_ROWS)
                keep = jnp.where(hit, 1.0, keep)
                g = jnp.where(hit, -jnp.inf, g)
            biases.append(jnp.where(keep > 0.0, 0.0, MASK_BIAS))
        else:
            k_tile = hh
            q = q_ref[0, :, hh * LANES:(hh + 1) * LANES]
        heads.append((q, k_tile))

    def softmax_attend(shift_free):
        def update(s, vt_blk, state):
            m_prev, l_prev, acc = state
            if shift_free:
                p = jnp.exp2(s)
                l_new = l_prev + jnp.sum(p, axis=0, keepdims=True)
                acc = acc + jnp.dot(vt_blk, p.astype(BF16), preferred_element_type=F32)
                return m_prev, l_new, acc
            m_new = jnp.maximum(m_prev, jnp.max(s, axis=0, keepdims=True))
            alpha = jnp.exp2(m_prev - m_new)
            p = jnp.exp2(s - m_new)
            l_new = alpha * l_prev + jnp.sum(p, axis=0, keepdims=True)
            acc = alpha * acc + jnp.dot(vt_blk, p.astype(BF16), preferred_element_type=F32)
            return m_new, l_new, acc

        def past_block(j, states):
            start = pl.multiple_of(j * t, t)
            scores = [lax.dot_general(k_ref[k_tile, pl.ds(start, t), :], q, _NT,
                                      preferred_element_type=F32) for q, k_tile in heads]
            out = []
            for hh, s in enumerate(scores):
                if moba:
                    s = jnp.sum(jnp.where(blk == j, biases[hh], 0.0), axis=0, keepdims=True) + s
                out.append(update(s, vt_ref[0, j, hh * hd:(hh + 1) * hd, :], states[hh]))
            return tuple(out)

        init = (jnp.full((1, t), NEG, F32), jnp.zeros((1, t), F32), jnp.zeros((hd, t), F32))
        states = lax.fori_loop(0, qi, past_block, (init,) * n_heads)

        diag_start = pl.multiple_of(qi * t, t)
        scores = [lax.dot_general(k_ref[k_tile, pl.ds(diag_start, t), :], q, _NT,
                                  preferred_element_type=F32) for q, k_tile in heads]
        outs = []
        for hh, s in enumerate(scores):
            s = jnp.where(key_i <= qry_i, s, NEG)
            _, l_fin, acc = update(s, vt_ref[0, qi, hh * hd:(hh + 1) * hd, :], states[hh])
            outs.append(acc / l_fin)
        o_ref[0] = jnp.concatenate(outs, axis=0).T.astype(BF16)

    bounded = bounded_ref[0] == 1

    @pl.when(bounded)
    def _():
        softmax_attend(True)

    @pl.when(jnp.logical_not(bounded))
    def _():
        softmax_attend(False)


def _logits_bounded(q_gain, k_gain, dims):
    bound = 1.02 * dims * jnp.max(jnp.abs(q_gain)) * jnp.max(jnp.abs(k_gain))
    return jnp.where(bound <= LOGIT_LIMIT, 1, 0).astype(jnp.int32).reshape(1)


def _attention(q, k, vt, kmean, bounded, *, moba):
    b, s, _ = q.shape
    v_cols = ATTN_GROUPS * LANES
    steps = vt.shape[2] // v_cols
    nblk = s // ATTN_TILE
    qk_cols = v_cols if moba else 2 * v_cols
    k_tiles = qk_cols // LANES
    in_specs = [
        pl.BlockSpec((1, ATTN_TILE, qk_cols), lambda bi, gi, qi, fl: (bi, qi, gi)),
        pl.BlockSpec((k_tiles, s, LANES), lambda bi, gi, qi, fl: (gi, bi, 0)),
        pl.BlockSpec((1, nblk, v_cols, ATTN_TILE), lambda bi, gi, qi, fl: (bi, 0, gi, 0)),
    ]
    args = [q, k, vt]
    if moba:
        in_specs.append(pl.BlockSpec((1, GATE_ROWS, v_cols), lambda bi, gi, qi, fl: (bi, 0, gi)))
        args.append(kmean)
    return pl.pallas_call(
        functools.partial(_attn_kernel, moba=moba),
        grid_spec=pltpu.PrefetchScalarGridSpec(
            num_scalar_prefetch=1,
            grid=(b, steps, nblk),
            in_specs=in_specs,
            out_specs=pl.BlockSpec((1, ATTN_TILE, v_cols), lambda bi, gi, qi, fl: (bi, qi, gi))),
        out_shape=jax.ShapeDtypeStruct((b, s, steps * v_cols), BF16),
        compiler_params=pltpu.CompilerParams(
            dimension_semantics=("arbitrary", "arbitrary", "arbitrary"),
            vmem_limit_bytes=VMEM_LIMIT),
        name="moba_attention" if moba else "mla_attention",
    )(bounded, *args)


def _merge_kernel(x_ref, ya_ref, yb_ref, ga_ref, gb_ref, wa_ref, wb_ref, wo_ref, gffn_ref,
                  wr_ref, br_ref, x1_ref, h_ref, route_ref, meta_ref, cnt_ref, run_scr):
    t = TOKEN_TILE

    @pl.when(pl.program_id(0) == 0)
    def _():
        run_scr[...] = jnp.zeros_like(run_scr)

    merged = (ga_ref[...].astype(F32) * jnp.dot(ya_ref[...], wa_ref[...], preferred_element_type=F32)
              + gb_ref[...].astype(F32) * jnp.dot(yb_ref[...], wb_ref[...], preferred_element_type=F32))
    x1 = x_ref[...] + jnp.dot(merged.astype(BF16), wo_ref[...], preferred_element_type=F32)
    x1_ref[...] = x1
    h = _rms(x1, gffn_ref[...]).astype(BF16)
    h_ref[...] = h

    logits = lax.dot_general(wr_ref[...], h, _NT, preferred_element_type=F32) + br_ref[...]
    e_i = lax.broadcasted_iota(jnp.int32, (N_EXPERTS, t), 0)
    lg = logits
    hits = []
    top = None
    for r in range(TOP_K):
        gmax = jnp.max(lg, axis=0, keepdims=True)
        pick = jnp.min(jnp.where(lg == gmax, e_i, N_EXPERTS), axis=0, keepdims=True)
        hit = e_i == pick
        if r == 0:
            top = gmax
        hits.append(hit)
        lg = jnp.where(hit, -jnp.inf, lg)
    sel = jnp.where(lg == -jnp.inf, 1.0, 0.0)
    wgt = sel * jnp.exp(logits - top)
    wgt = wgt / jnp.sum(wgt, axis=0, keepdims=True)

    r_i = lax.broadcasted_iota(jnp.int32, (t, t), 0)
    c_i = lax.broadcasted_iota(jnp.int32, (t, t), 1)
    earlier = jnp.where(r_i < c_i, 1.0, 0.0).astype(BF16)
    rank_in_tile = jnp.dot(sel.astype(BF16), earlier, preferred_element_type=F32)
    tcnt = jnp.sum(sel, axis=1, keepdims=True)
    tpad = jnp.floor((tcnt + (RUN_CHUNK - 1)) * (1.0 / RUN_CHUNK)) * RUN_CHUNK
    e_r = lax.broadcasted_iota(jnp.int32, (N_EXPERTS, N_EXPERTS), 0)
    e_c = lax.broadcasted_iota(jnp.int32, (N_EXPERTS, N_EXPERTS), 1)
    before = jnp.where(e_c < e_r, 1.0, 0.0).astype(BF16)
    tbase = jnp.dot(before, jnp.broadcast_to(tpad, (N_EXPERTS, LANES)).astype(BF16),
                    preferred_element_type=F32)
    n_chunks = jnp.sum(tpad, axis=0, keepdims=True) * (1.0 / RUN_CHUNK)
    run = run_scr[...]
    run_new = run + tcnt
    run_scr[...] = run_new
    cnt_ref[...] = run_new
    lane = lax.broadcasted_iota(jnp.int32, (N_EXPERTS, LANES), 1)
    meta_ref[0] = jnp.where(lane == 0, tcnt, jnp.where(lane == 1, run, jnp.where(
        lane == 2, tbase, jnp.where(lane == 3, n_chunks, 0.0))))

    slot_of = rank_in_tile + tbase[:, 0:1]
    row = lax.broadcasted_iota(jnp.int32, (2 * TOP_K, t), 0)
    route = jnp.zeros((2 * TOP_K, t), F32)
    for r in range(TOP_K):
        w_r = jnp.sum(jnp.where(hits[r], wgt, 0.0), axis=0, keepdims=True)
        slot_r = jnp.sum(jnp.where(hits[r], slot_of, 0.0), axis=0, keepdims=True)
        route = jnp.where(row == r, w_r, route)
        route = jnp.where(row == TOP_K + r, slot_r, route)
    route_ref[0] = route


BIG_COPY = 4


def _for_each_run_copy(meta_ref, pstart_ref, fn):
    big_rows = BIG_COPY * RUN_CHUNK

    def per_expert(e, carry):
        n_chunks = lax.shift_right_logical(meta_ref[e] + (RUN_CHUNK - 1), RUN_CHUNK.bit_length() - 1)
        n_big = lax.shift_right_logical(n_chunks, BIG_COPY.bit_length() - 1)
        slot0 = meta_ref[2 * LANES + e]
        row0 = pstart_ref[e] + meta_ref[LANES + e]

        def big(b, cc):
            fn(slot0 + b * big_rows, row0 + b * big_rows, big_rows)
            return cc

        def small(c, cc):
            fn(slot0 + c * RUN_CHUNK, row0 + c * RUN_CHUNK, RUN_CHUNK)
            return cc

        lax.fori_loop(0, n_big, big, 0)
        lax.fori_loop(n_big * BIG_COPY, n_chunks, small, 0)
        return carry

    lax.fori_loop(0, N_EXPERTS, per_expert, 0)


WAIT_BATCH = 16


def _wait_run_chunks(meta_ref, make_copy):
    n_chunks = meta_ref[3 * LANES]
    n_batches = lax.shift_right_logical(n_chunks, WAIT_BATCH.bit_length() - 1)
    lax.fori_loop(0, n_batches, lambda b, c: (make_copy(WAIT_BATCH * RUN_CHUNK).wait(), c)[1], 0)
    lax.fori_loop(0, n_chunks - n_batches * WAIT_BATCH,
                  lambda b, c: (make_copy(RUN_CHUNK).wait(), c)[1], 0)


def _slab_rows(first_row, n_rows):
    return pl.ds(pl.multiple_of(first_row * ROW_SLABS, ROW_SLABS), n_rows * ROW_SLABS)


def _dispatch_kernel(pstart_ref, pend_ref, nact_ref, meta_ref, meta_prev_ref, route_ref, h_ref,
                     xr_hbm, stage, zero_scr, sems, zsem):
    t = TOKEN_TILE
    i = pl.program_id(0)
    half = lax.rem(i, 2)

    def run_copy(buf_half, slot, row, n_rows=RUN_CHUNK):
        src = stage.at[_slab_rows(buf_half * RUN_SLOTS + slot, n_rows), :]
        return pltpu.make_async_copy(src, xr_hbm.at[_slab_rows(row, n_rows), :], sems.at[buf_half])

    @pl.when(i == 0)
    def _():
        zero_scr[...] = jnp.zeros_like(zero_scr)
        n_blocks = xr_hbm.shape[0] // (EXPERT_ROWS * ROW_SLABS)

        def block_copy(first_row):
            return pltpu.make_async_copy(zero_scr, xr_hbm.at[_slab_rows(first_row, EXPERT_ROWS), :], zsem)

        def pad_blocks(e):
            seg_blocks = lax.shift_right_logical(pend_ref[e] - pstart_ref[e],
                                                 EXPERT_ROWS.bit_length() - 1)
            return jnp.minimum(seg_blocks, 2)

        def zero_pad(e, c):
            lax.fori_loop(1, pad_blocks(e) + 1,
                          lambda b, cc: (block_copy(pend_ref[e] - b * EXPERT_ROWS).start(), cc)[1], 0)
            return c

        def wait_pad(e, c):
            lax.fori_loop(0, pad_blocks(e), lambda b, cc: (block_copy(0).wait(), cc)[1], 0)
            return c

        def zero_tail(blk, c):
            block_copy(blk * EXPERT_ROWS).start()
            return c

        lax.fori_loop(0, N_EXPERTS, zero_pad, 0)
        lax.fori_loop(nact_ref[0], n_blocks, zero_tail, 0)
        lax.fori_loop(0, N_EXPERTS, wait_pad, 0)
        lax.fori_loop(0, n_blocks - nact_ref[0], lambda r, c: (block_copy(0).wait(), c)[1], 0)

    slot_rows = route_ref[0, TOP_K:2 * TOP_K, :]
    s_i = lax.broadcasted_iota(jnp.int32, (RUN_SLOTS, t), 0).astype(F32)
    pick = jnp.zeros((RUN_SLOTS, t), F32)
    for k in range(TOP_K):
        pick = jnp.where(s_i == slot_rows[k:k + 1, :], 1.0, pick)
    rows = jnp.dot(pick.astype(BF16), h_ref[...], preferred_element_type=F32)
    base = half * (RUN_SLOTS * ROW_SLABS)
    for c in range(ROW_SLABS):
        stage[pl.ds(base + c, RUN_SLOTS, stride=ROW_SLABS), :] = rows[:, c * LANES:(c + 1) * LANES]

    @pl.when(i > 0)
    def _():
        _wait_run_chunks(meta_prev_ref, lambda n_rows: run_copy(1 - half, 0, 0, n_rows))

    _for_each_run_copy(meta_ref, pstart_ref, lambda s, r, n: run_copy(half, s, r, n).start())

    @pl.when(i == pl.num_programs(0) - 1)
    def _():
        _wait_run_chunks(meta_ref, lambda n_rows: run_copy(half, 0, 0, n_rows))


def _expert_kernel(blk_e_ref, nact_ref, xr_ref, wgu_hbm, bgu_ref, wdn_hbm, bdn_ref, yr_ref,
                   wgu_f32, wdn_f32, wgu_bf, wdn_bf, sems):
    r = EXPERT_ROWS
    i = pl.program_id(0)
    e = blk_e_ref[i]
    active = i < nact_ref[0]
    new_expert = jnp.logical_or(i == 0, e != blk_e_ref[jnp.maximum(i - 1, 0)])

    def weight_copies(expert):
        half = lax.rem(expert, 2)
        return (pltpu.make_async_copy(wgu_hbm.at[expert], wgu_f32.at[half], sems.at[0, half]),
                pltpu.make_async_copy(wdn_hbm.at[expert], wdn_f32.at[half], sems.at[1, half]))

    @pl.when(i == 0)
    def _():
        for cp in weight_copies(e):
            cp.start()

    @pl.when(jnp.logical_and(active, new_expert))
    def _():
        for cp in weight_copies(e):
            cp.wait()

        @pl.when(e + 1 < N_EXPERTS)
        def _():
            for cp in weight_copies(e + 1):
                cp.start()

        half = lax.rem(e, 2)

        def cast_rows(c, carry):
            rows = pl.ds(pl.multiple_of(c * LANES, LANES), LANES)
            wgu_bf[rows, :] = wgu_f32[half, rows, :].astype(BF16)
            wdn_bf[rows, :] = wdn_f32[half, rows, :].astype(BF16)
            return carry

        lax.fori_loop(0, D_MODEL // LANES, cast_rows, 0)

    @pl.when(active)
    def _():
        x = jnp.concatenate(
            [xr_ref[pl.ds(c, r, stride=ROW_SLABS), :] for c in range(ROW_SLABS)], axis=1)
        gu = jnp.dot(x.astype(BF16), wgu_bf[...], preferred_element_type=F32) + bgu_ref[0]
        g = jnp.minimum(gu[:, :D_EXPERT], SWIGLU_LIMIT)
        u = jnp.clip(gu[:, D_EXPERT:], -SWIGLU_LIMIT, SWIGLU_LIMIT)
        act = (u + 1.0) * (g * jax.nn.sigmoid(SWIGLU_ALPHA * g))
        y = jnp.dot(act.astype(BF16), wdn_bf[...], preferred_element_type=F32) + bdn_ref[0]
        for c in range(ROW_SLABS):
            yr_ref[pl.ds(c, r, stride=ROW_SLABS), :] = y[:, c * LANES:(c + 1) * LANES]

    @pl.when(jnp.logical_not(active))
    def _():
        yr_ref[...] = jnp.zeros_like(yr_ref)


def _final_kernel(pstart_ref, meta_ref, meta_next_ref, x1_ref, route_ref, p_ref, gple_ref,
                  wpg_ref, wpp_ref, yr_hbm, o_ref, gstage, sems):
    t = TOKEN_TILE
    i = pl.program_id(0)
    half = lax.rem(i, 2)

    def run_copy(buf_half, slot, row, n_rows=RUN_CHUNK):
        dst = gstage.at[_slab_rows(buf_half * RUN_SLOTS + slot, n_rows), :]
        return pltpu.make_async_copy(yr_hbm.at[_slab_rows(row, n_rows), :], dst, sems.at[buf_half])

    @pl.when(i == 0)
    def _():
        gstage[...] = jnp.zeros_like(gstage)
        _for_each_run_copy(meta_ref, pstart_ref, lambda s, r, n: run_copy(0, s, r, n).start())

    @pl.when(i + 1 < pl.num_programs(0))
    def _():
        _for_each_run_copy(meta_next_ref, pstart_ref,
                           lambda s, r, n: run_copy(1 - half, s, r, n).start())

    _wait_run_chunks(meta_ref, lambda n_rows: run_copy(half, 0, 0, n_rows))

    base = half * (RUN_SLOTS * ROW_SLABS)
    rows = jnp.concatenate(
        [gstage[pl.ds(base + c, RUN_SLOTS, stride=ROW_SLABS), :] for c in range(ROW_SLABS)], axis=1)
    route = route_ref[...]
    s_i = lax.broadcasted_iota(jnp.int32, (t, RUN_SLOTS), 1).astype(F32)
    wmat = jnp.zeros((t, RUN_SLOTS), F32)
    for k in range(TOP_K):
        wmat = jnp.where(s_i == route[:, TOP_K + k:TOP_K + k + 1], route[:, k:k + 1], wmat)
    rows_hi = rows.astype(BF16)
    rows_lo = (rows - rows_hi.astype(F32)).astype(BF16)
    w_hi = wmat.astype(BF16)
    w_lo = (wmat - w_hi.astype(F32)).astype(BF16)
    y = (jnp.dot(w_hi, rows_hi, preferred_element_type=F32)
         + jnp.dot(w_lo, rows_hi, preferred_element_type=F32)
         + jnp.dot(w_hi, rows_lo, preferred_element_type=F32))
    x2 = x1_ref[...] + y
    hp = _rms(x2, gple_ref[...]).astype(BF16)
    gate = jax.nn.sigmoid(jnp.dot(hp, wpg_ref[...], preferred_element_type=F32))
    emb = jnp.dot(p_ref[...].astype(BF16), wpp_ref[...], preferred_element_type=F32)
    o_ref[...] = x2 + gate * emb


def _rope_tables(s, half, x1_starts, x2_starts):
    inv_freq = ROPE_THETA ** (-(np.arange(half, dtype=np.float64) / half))
    ang = np.arange(s, dtype=np.float64)[:, None] * inv_freq[None, :]
    cos, sin = np.cos(ang), np.sin(ang)
    cos_t, sin_t = np.ones((s, LANES), np.float32), np.zeros((s, LANES), np.float32)
    for st in x1_starts:
        cos_t[:, st:st + half] = cos
        sin_t[:, st:st + half] = -sin
    for st in x2_starts:
        cos_t[:, st:st + half] = cos
        sin_t[:, st:st + half] = sin
    return jnp.asarray(cos_t), jnp.asarray(sin_t)


_MOBA_LANE_COLS = tuple(list(range(0, 32)) + list(range(64, 96)) + list(range(32, 64))
                        + list(range(96, 128)))
_MLA_LANE_DIMS = tuple(list(range(80, 96)) + list(range(0, 48)) + list(range(64, 80))
                       + list(range(48, 64)) + [MLA_QK_DIM] * 32)


def _moba_lanes(w):
    k, width = w.shape
    cols = jnp.asarray(_MOBA_LANE_COLS, jnp.int32)
    return w.reshape(k, width // LANES, LANES)[:, :, cols].reshape(k, width)


def _mla_lanes(w, heads):
    k = w.shape[0]
    w = jnp.pad(w.reshape(k, heads, MLA_QK_DIM), ((0, 0), (0, 0), (0, 1)))
    return w[:, :, jnp.asarray(_MLA_LANE_DIMS, jnp.int32)].reshape(k, heads * LANES)


def _row(v):
    return v.reshape(1, -1).astype(F32)


def _layer(x, p_i, g_mix, w_in, moba_q_norm, moba_k_norm, mla_q_lat_norm, w_uq, mla_kv_lat_norm,
           w_ukv, mla_q_norm, mla_k_norm, w_branch_a, w_branch_b, w_out, g_ffn, w_router, b_router,
           w_gate_up, b_gate_up, w_down, b_down, g_ple, w_ple_gate, w_ple_proj):
    b, s, d = x.shape
    n = b * s
    assert d == D_MODEL and s % ATTN_TILE == 0
    assert s // MOBA_BLOCK <= GATE_ROWS and TOKEN_TILE == ATTN_TILE == MOBA_BLOCK
    assert D_EXPERT == D_MODEL and RUN_CHUNK == SUBLANES
    n_tiles = n // TOKEN_TILE
    tiles_per_seq = s // TOKEN_TILE
    xf = x.reshape(n, d)

    off = [0]
    for wdt in (MOBA_WIDTH, MOBA_WIDTH, MOBA_WIDTH, MLA_Q_LORA, MLA_KV_LORA, MLA_ROPE_DIM, D_MODEL, D_MODEL):
        off.append(off[-1] + wdt)
    seg = [w_in[:, off[i]:off[i + 1]] for i in range(8)]
    kpe_cols = _mla_lanes(jnp.pad(seg[5], ((0, 0), (MLA_NOPE_DIM, 0))), 1)
    w_in_p = jnp.concatenate([_moba_lanes(seg[0]), _moba_lanes(seg[1])] + seg[3:5] + [kpe_cols]
                             + seg[6:], axis=1).astype(BF16)
    assert w_in_p.shape[1] == D_IN_PACKED
    w_va_t = seg[2].T.astype(BF16)
    w_uq_p = _mla_lanes(w_uq, MLA_HEADS).astype(BF16)
    w_ukv_h = w_ukv.reshape(MLA_KV_LORA, MLA_HEADS, MLA_NOPE_DIM + MLA_V_DIM)
    w_uk_p = _mla_lanes(jnp.pad(w_ukv_h[:, :, :MLA_NOPE_DIM], ((0, 0), (0, 0), (0, MLA_ROPE_DIM)))
                        .reshape(MLA_KV_LORA, -1), MLA_HEADS).astype(BF16)
    w_uv_t = w_ukv_h[:, :, MLA_NOPE_DIM:].reshape(MLA_KV_LORA, MLA_WIDTH).T.astype(BF16)
    gqa = _moba_lanes(_row(jnp.tile(moba_q_norm, 2))) * (MOBA_HEAD_DIM ** -0.5 * LOG2_E)
    gka = _moba_lanes(_row(jnp.tile(moba_k_norm, 2)))
    gqb = _mla_lanes(_row(mla_q_norm), 1) * (MLA_QK_DIM ** -0.5 * LOG2_E)
    gkb = _mla_lanes(_row(mla_k_norm), 1)
    half_a, half_b = MOBA_HEAD_DIM // 2, MLA_ROPE_DIM // 2
    cosa, sina = _rope_tables(s, half_a, (0, half_a), (LANES // 2, LANES // 2 + half_a))
    cosb, sinb = _rope_tables(s, half_b, (LANES // 2,), (0,))

    tok = lambda width: pl.BlockSpec((TOKEN_TILE, width), lambda i: (i, 0))
    whole = lambda arr: pl.BlockSpec(arr.shape, lambda i: (0,) * arr.ndim)
    seq_tab = pl.BlockSpec((TOKEN_TILE, LANES), lambda i: (i % tiles_per_seq, 0))
    vt_spec = pl.BlockSpec((1, MOBA_WIDTH, TOKEN_TILE), lambda i: (i, 0, 0))
    k_tiles_spec = lambda tiles: pl.BlockSpec((tiles, TOKEN_TILE, LANES), lambda i: (0, i, 0))
    params = pltpu.CompilerParams(dimension_semantics=("arbitrary",), vmem_limit_bytes=VMEM_LIMIT)

    consts1 = [_row(g_mix), w_in_p, w_va_t, gqa, gka]
    consts2 = [_row(mla_q_lat_norm), w_uq_p, _row(mla_kv_lat_norm), w_uk_p, w_uv_t, gqb, gkb]
    qa, ka, va, kmean, qb, kb, vb, ga, gb = pl.pallas_call(
        _inproj_kernel,
        grid=(n_tiles,),
        in_specs=([tok(d)] + [whole(a) for a in consts1] + [seq_tab] * 2
                  + [whole(a) for a in consts2] + [seq_tab] * 2),
        out_specs=[tok(MOBA_WIDTH), k_tiles_spec(MOBA_WIDTH // LANES), vt_spec,
                   pl.BlockSpec((1, 1, MOBA_WIDTH), lambda i: (i, 0, 0)),
                   tok(MLA_HEADS * LANES), k_tiles_spec(MLA_HEADS), vt_spec,
                   tok(d), tok(d)],
        out_shape=[jax.ShapeDtypeStruct((n, MOBA_WIDTH), BF16),
                   jax.ShapeDtypeStruct((MOBA_WIDTH // LANES, n, LANES), BF16),
                   jax.ShapeDtypeStruct((n_tiles, MOBA_WIDTH, TOKEN_TILE), BF16),
                   jax.ShapeDtypeStruct((n_tiles, 1, MOBA_WIDTH), F32),
                   jax.ShapeDtypeStruct((n, MLA_HEADS * LANES), BF16),
                   jax.ShapeDtypeStruct((MLA_HEADS, n, LANES), BF16),
                   jax.ShapeDtypeStruct((n_tiles, MLA_WIDTH, TOKEN_TILE), BF16)]
        + [jax.ShapeDtypeStruct((n, d), BF16)] * 2,
        compiler_params=params,
        name="in_projection",
    )(xf, *consts1, cosa, sina, *consts2, cosb, sinb)

    kmean = kmean.reshape(b, tiles_per_seq, MOBA_WIDTH)
    kmean = jnp.pad(kmean, ((0, 0), (0, GATE_ROWS - tiles_per_seq), (0, 0))).astype(BF16)

    r3 = lambda a: a.reshape(b, s, a.shape[-1])
    r4 = lambda a: a.reshape(b, tiles_per_seq, a.shape[1], TOKEN_TILE)
    ya = _attention(r3(qa), ka, r4(va), kmean, _logits_bounded(gqa, gka, MOBA_HEAD_DIM),
                    moba=True).reshape(n, MOBA_WIDTH)
    yb = _attention(r3(qb), kb, r4(vb), None, _logits_bounded(gqb, gkb, MLA_QK_DIM),
                    moba=False).reshape(n, MLA_WIDTH)

    consts3 = [w_branch_a.astype(BF16), w_branch_b.astype(BF16), w_out.astype(BF16), _row(g_ffn),
               w_router.T.astype(BF16), b_router.reshape(N_EXPERTS, 1).astype(F32)]
    x1, h_ffn, route_t, meta, cnt = pl.pallas_call(
        _merge_kernel,
        grid=(n_tiles,),
        in_specs=[tok(d), tok(MOBA_WIDTH), tok(MLA_WIDTH), tok(d), tok(d)]
        + [whole(a) for a in consts3],
        out_specs=[tok(d), tok(d),
                   pl.BlockSpec((1, 2 * TOP_K, TOKEN_TILE), lambda i: (i, 0, 0)),
                   pl.BlockSpec((1, N_EXPERTS, LANES), lambda i: (i, 0, 0)),
                   pl.BlockSpec((N_EXPERTS, LANES), lambda i: (0, 0))],
        out_shape=[jax.ShapeDtypeStruct((n, d), F32),
                   jax.ShapeDtypeStruct((n, d), BF16),
                   jax.ShapeDtypeStruct((n_tiles, 2 * TOP_K, TOKEN_TILE), F32),
                   jax.ShapeDtypeStruct((n_tiles, N_EXPERTS, LANES), F32),
                   jax.ShapeDtypeStruct((N_EXPERTS, LANES), F32)],
        scratch_shapes=[pltpu.VMEM((N_EXPERTS, LANES), F32)],
        compiler_params=params,
        name="merge_router",
    )(xf, ya, yb, ga, gb, *consts3)
    route = jnp.pad(route_t.transpose(0, 2, 1).reshape(n, 2 * TOP_K),
                    ((0, 0), (0, LANES - 2 * TOP_K)))

    rb = EXPERT_ROWS
    n_blocks = -(-(n * TOP_K + N_EXPERTS * (RUN_CHUNK - 1)) // rb) + N_EXPERTS
    n_rows = n_blocks * rb
    counts = cnt[:, 0].astype(jnp.int32)
    pcounts = ((counts + (RUN_CHUNK - 1) + rb - 1) // rb) * rb
    pends = jnp.cumsum(pcounts).astype(jnp.int32)
    pstarts = (pends - pcounts).astype(jnp.int32)
    nact = (pends[-1] // rb).astype(jnp.int32).reshape(1)
    blk = jnp.minimum(jnp.arange(n_blocks, dtype=jnp.int32), nact[0] - 1)
    blk_e = jnp.sum((pends[None, :] <= (blk * rb)[:, None]).astype(jnp.int32), axis=1)
    blk_e = jnp.minimum(blk_e, N_EXPERTS - 1)
    meta_i = jnp.pad(meta[:, :, :4].transpose(0, 2, 1), ((0, 0), (0, 0), (0, LANES - N_EXPERTS)))
    meta_i = meta_i.astype(jnp.int32).reshape(n_tiles * 4 * LANES)
    meta_spec = lambda shift: pl.BlockSpec(
        (4 * LANES,), lambda i, *_: (jnp.clip(i + shift, 0, n_tiles - 1),), memory_space=pltpu.SMEM)

    xr = pl.pallas_call(
        _dispatch_kernel,
        grid_spec=pltpu.PrefetchScalarGridSpec(
            num_scalar_prefetch=3,
            grid=(n_tiles,),
            in_specs=[meta_spec(0), meta_spec(-1),
                      pl.BlockSpec((1, 2 * TOP_K, TOKEN_TILE), lambda i, *_: (i, 0, 0)),
                      pl.BlockSpec((TOKEN_TILE, d), lambda i, *_: (i, 0))],
            out_specs=pl.BlockSpec(memory_space=pl.ANY),
            scratch_shapes=[pltpu.VMEM((2 * RUN_SLOTS * ROW_SLABS, LANES), F32),
                            pltpu.VMEM((EXPERT_ROWS * ROW_SLABS, LANES), F32),
                            pltpu.SemaphoreType.DMA((2,)), pltpu.SemaphoreType.DMA(())]),
        out_shape=jax.ShapeDtypeStruct((n_rows * ROW_SLABS, LANES), F32),
        compiler_params=params,
        name="dispatch_rows",
    )(pstarts, pends, nact, meta_i, meta_i, route_t, h_ffn)

    act_blk = lambda i, be, na: jnp.minimum(i, na[0] - 1)
    yr = pl.pallas_call(
        _expert_kernel,
        grid_spec=pltpu.PrefetchScalarGridSpec(
            num_scalar_prefetch=2,
            grid=(n_blocks,),
            in_specs=[
                pl.BlockSpec((rb * ROW_SLABS, LANES), lambda i, be, na: (act_blk(i, be, na), 0)),
                pl.BlockSpec(memory_space=pl.ANY),
                pl.BlockSpec((1, 1, 2 * D_EXPERT), lambda i, be, na: (be[i], 0, 0)),
                pl.BlockSpec(memory_space=pl.ANY),
                pl.BlockSpec((1, 1, d), lambda i, be, na: (be[i], 0, 0)),
            ],
            out_specs=pl.BlockSpec((rb * ROW_SLABS, LANES), lambda i, be, na: (i, 0)),
            scratch_shapes=[pltpu.VMEM((2, d, 2 * D_EXPERT), F32), pltpu.VMEM((2, D_EXPERT, d), F32),
                            pltpu.VMEM((d, 2 * D_EXPERT), BF16), pltpu.VMEM((D_EXPERT, d), BF16),
                            pltpu.SemaphoreType.DMA((2, 2))]),
        out_shape=jax.ShapeDtypeStruct((n_rows * ROW_SLABS, LANES), F32),
        compiler_params=params,
        name="experts",
    )(blk_e, nact, xr, w_gate_up.astype(F32),
      b_gate_up.reshape(N_EXPERTS, 1, -1).astype(F32), w_down.astype(F32),
      b_down.reshape(N_EXPERTS, 1, -1).astype(F32))

    ftok = lambda width: pl.BlockSpec((TOKEN_TILE, width), lambda i, ps: (i, 0))
    fwhole = lambda arr: pl.BlockSpec(arr.shape, lambda i, ps: (0,) * arr.ndim)
    consts4 = [_row(g_ple), w_ple_gate.astype(BF16), w_ple_proj.astype(BF16)]
    out = pl.pallas_call(
        _final_kernel,
        grid_spec=pltpu.PrefetchScalarGridSpec(
            num_scalar_prefetch=1,
            grid=(n_tiles,),
            in_specs=[meta_spec(0), meta_spec(1), ftok(d), ftok(LANES), ftok(PLE_DIM)]
            + [fwhole(a) for a in consts4] + [pl.BlockSpec(memory_space=pl.ANY)],
            out_specs=ftok(d),
            scratch_shapes=[pltpu.VMEM((2 * RUN_SLOTS * ROW_SLABS, LANES), F32),
                            pltpu.SemaphoreType.DMA((2,))]),
        out_shape=jax.ShapeDtypeStruct((n, d), F32),
        compiler_params=params,
        name="combine_ple",
    )(pstarts, meta_i, meta_i, x1, route, p_i.reshape(n, PLE_DIM), *consts4, yr)
    return out.reshape(b, s, d)


def kernel(x, p, g_mix, w_in, moba_q_norm, moba_k_norm, mla_q_lat_norm, w_uq, mla_kv_lat_norm, w_ukv, mla_q_norm, mla_k_norm, w_branch_a, w_branch_b, w_out, g_ffn, w_router, b_router, w_gate_up, b_gate_up, w_down, b_down, g_ple, w_ple_gate, w_ple_proj):
    for i in range(p.shape[0]):
        x = _layer(x, p[i], g_mix[i], w_in[i], moba_q_norm[i], moba_k_norm[i], mla_q_lat_norm[i],
                   w_uq[i], mla_kv_lat_norm[i], w_ukv[i], mla_q_norm[i], mla_k_norm[i],
                   w_branch_a[i], w_branch_b[i], w_out[i], g_ffn[i], w_router[i], b_router[i],
                   w_gate_up[i], b_gate_up[i], w_down[i], b_down[i], g_ple[i], w_ple_gate[i],
                   w_ple_proj[i])
    return x
```

```python
import functools

import numpy as np
import jax
import jax.numpy as jnp
from jax import lax
from jax.experimental import pallas as pl
from jax.experimental.pallas import tpu as pltpu

F32 = jnp.float32
BF16 = jnp.bfloat16

D_MODEL = 1024
PLE_DIM = 256
EPS = 1e-6
ROPE_THETA = 10000.0
MOBA_HEADS = 8
MOBA_HEAD_DIM = 64
MOBA_BLOCK = 256
MOBA_TOPK = 3
MOBA_WIDTH = MOBA_HEADS * MOBA_HEAD_DIM
MLA_HEADS = 8
MLA_Q_LORA = 256
MLA_KV_LORA = 128
MLA_NOPE_DIM = 64
MLA_ROPE_DIM = 32
MLA_V_DIM = 64
MLA_QK_DIM = MLA_NOPE_DIM + MLA_ROPE_DIM
MLA_WIDTH = MLA_HEADS * MLA_V_DIM
N_EXPERTS = 32
TOP_K = 4
D_EXPERT = 1024
SWIGLU_LIMIT = 7.0
SWIGLU_ALPHA = 1.702

LANES = 128
SUBLANES = 8
ROW_SLABS = D_MODEL // LANES
VMEM_LIMIT = 56 * 1024 * 1024

TOKEN_TILE = 256
ATTN_TILE = 256
ATTN_GROUPS = 4
EXPERT_ROWS = 512
RUN_CHUNK = 8
RUN_SLOTS = -(-(TOKEN_TILE * TOP_K + N_EXPERTS * (RUN_CHUNK - 1)) // 256) * 256

LOG2_E = 1.4426950408889634
LOGIT_LIMIT = 60.0
NEG = -1e30
MASK_BIAS = -1e9

C_QA, C_KA = 0, 512
C_CQ, C_CKV, C_KPE = 1024, 1280, 1408
C_GA, C_GB = 1536, 2560
D_IN_PACKED = 3584


def _rms(x, gain):
    return x * lax.rsqrt(jnp.mean(x * x, axis=-1, keepdims=True) + EPS) * gain


def _rope(t, cos, sin):
    return t * cos + pltpu.roll(t, LANES // 2, 1) * sin


def _moba_even_head(lane):
    return (lane & (MOBA_HEAD_DIM // 2)) == 0


def _inproj_kernel(x_ref, gmix_ref, win_ref, wvat_ref, gqa_ref, gka_ref, cosa_ref, sina_ref,
                   gql_ref, wuq_ref, gkvl_ref, wuk_ref, wuvt_ref, gqb_ref, gkb_ref,
                   cosb_ref, sinb_ref,
                   qa_ref, ka_ref, vat_ref, kmean_ref, qb_ref, kb_ref, vbt_ref, ga_ref, gb_ref):
    hn = _rms(x_ref[...], gmix_ref[...]).astype(BF16)

    def proj(c0, width):
        return jnp.dot(hn, win_ref[:, c0:c0 + width], preferred_element_type=F32)

    first = _moba_even_head(lax.broadcasted_iota(jnp.int32, (TOKEN_TILE, LANES), 1))
    cosa, sina = cosa_ref[...], sina_ref[...]

    def moba_norm_rope(t, gain):
        sq = t * t
        ss0 = jnp.sum(jnp.where(first, sq, 0.0), axis=-1, keepdims=True)
        ss1 = jnp.sum(jnp.where(first, 0.0, sq), axis=-1, keepdims=True)
        ms = jnp.where(first, ss0, ss1) * (1.0 / MOBA_HEAD_DIM)
        t = t * lax.rsqrt(ms + EPS) * gain
        return _rope(t, cosa, sina)

    qa = proj(C_QA, MOBA_WIDTH)
    ka = proj(C_KA, MOBA_WIDTH)
    for c in range(MOBA_WIDTH // LANES):
        sl = slice(c * LANES, (c + 1) * LANES)
        qa_ref[:, sl] = moba_norm_rope(qa[:, sl], gqa_ref[...]).astype(BF16)
        kc = moba_norm_rope(ka[:, sl], gka_ref[...])
        ka_ref[c] = kc.astype(BF16)
        kmean_ref[0, :, sl] = jnp.mean(kc, axis=0, keepdims=True)
    vat_ref[0] = lax.dot_general(wvat_ref[...], hn, _NT, preferred_element_type=F32).astype(BF16)

    cosb, sinb = cosb_ref[...], sinb_ref[...]

    def mla_norm_rope(t, gain):
        ms = jnp.sum(t * t, axis=-1, keepdims=True) * (1.0 / MLA_QK_DIM)
        t = t * lax.rsqrt(ms + EPS) * gain
        return _rope(t, cosb, sinb)

    cq = _rms(proj(C_CQ, MLA_Q_LORA), gql_ref[...]).astype(BF16)
    qb = jnp.dot(cq, wuq_ref[...], preferred_element_type=F32)
    ckv = _rms(proj(C_CKV, MLA_KV_LORA), gkvl_ref[...]).astype(BF16)
    kn = jnp.dot(ckv, wuk_ref[...], preferred_element_type=F32)
    kpe = proj(C_KPE, LANES)
    for h in range(MLA_HEADS):
        sl = slice(h * LANES, (h + 1) * LANES)
        qb_ref[:, sl] = mla_norm_rope(qb[:, sl], gqb_ref[...]).astype(BF16)
        kb_ref[h] = mla_norm_rope(kn[:, sl] + kpe, gkb_ref[...]).astype(BF16)
    vbt_ref[0] = lax.dot_general(wuvt_ref[...], ckv, _NT, preferred_element_type=F32).astype(BF16)

    ga_ref[...] = jax.nn.sigmoid(proj(C_GA, D_MODEL)).astype(BF16)
    gb_ref[...] = jax.nn.sigmoid(proj(C_GB, D_MODEL)).astype(BF16)


_NT = (((1,), (1,)), ((), ()))


GATE_ROWS = 16


def _attn_kernel(*refs, moba):
    if moba:
        bounded_ref, q_ref, k_ref, vt_ref, kmean_ref, o_ref = refs
    else:
        bounded_ref, q_ref, k_ref, vt_ref, o_ref = refs
    t = ATTN_TILE
    hd = MOBA_HEAD_DIM
    n_heads = 2 * ATTN_GROUPS
    qi = pl.program_id(2)
    key_i = lax.broadcasted_iota(jnp.int32, (t, t), 0)
    qry_i = lax.broadcasted_iota(jnp.int32, (t, t), 1)

    blk = lax.broadcasted_iota(jnp.int32, (GATE_ROWS, t), 0)
    heads, biases = [], []
    for hh in range(n_heads):
        if moba:
            lane = lax.broadcasted_iota(jnp.int32, (t, LANES), 1)
            even = _moba_even_head(lane)
            head_lanes = even if hh % 2 == 0 else jnp.logical_not(even)
            k_tile = hh // 2
            kcols = slice(k_tile * LANES, (k_tile + 1) * LANES)
            q = jnp.where(head_lanes, q_ref[0, :, kcols], jnp.zeros((), BF16))
            gate = lax.dot_general(kmean_ref[0, :, kcols], q, _NT, preferred_element_type=F32)
            g = jnp.where(blk < qi, gate, -jnp.inf)
            keep = jnp.zeros((GATE_ROWS, t), F32)
            for _ in range(MOBA_TOPK):
                gmax = jnp.max(g, axis=0, keepdims=True)
                pick = jnp.min(jnp.where(g == gmax, blk, GATE_ROWS), axis=0, keepdims=True)
                hit = blk == jnp.where(gmax > -jnp.inf, pick, GATE_ROWS)
                keep = jnp.where(hit, 1.0, keep)
                g = jnp.where(hit, -jnp.inf, g)
            biases.append(jnp.where(keep > 0.0, 0.0, MASK_BIAS))
        else:
            k_tile = hh
            q = q_ref[0, :, hh * LANES:(hh + 1) * LANES]
        heads.append((q, k_tile))

    def softmax_attend(shift_free):
        def update(s, vt_blk, state):
            m_prev, l_prev, acc = state
            if shift_free:
                p = jnp.exp2(s)
                l_new = l_prev + jnp.sum(p, axis=0, keepdims=True)
                acc = acc + jnp.dot(vt_blk, p.astype(BF16), preferred_element_type=F32)
                return m_prev, l_new, acc
            m_new = jnp.maximum(m_prev, jnp.max(s, axis=0, keepdims=True))
            alpha = jnp.exp2(m_prev - m_new)
            p = jnp.exp2(s - m_new)
            l_new = alpha * l_prev + jnp.sum(p, axis=0, keepdims=True)
            acc = alpha * acc + jnp.dot(vt_blk, p.astype(BF16), preferred_element_type=F32)
            return m_new, l_new, acc

        def past_block(j, states):
            start = pl.multiple_of(j * t, t)
            scores = [lax.dot_general(k_ref[k_tile, pl.ds(start, t), :], q, _NT,
                                      preferred_element_type=F32) for q, k_tile in heads]
            out = []
            for hh, s in enumerate(scores):
                if moba:
                    s = jnp.sum(jnp.where(blk == j, biases[hh], 0.0), axis=0, keepdims=True) + s
                out.append(update(s, vt_ref[0, j, hh * hd:(hh + 1) * hd, :], states[hh]))
            return tuple(out)

        init = (jnp.full((1, t), NEG, F32), jnp.zeros((1, t), F32), jnp.zeros((hd, t), F32))
        states = lax.fori_loop(0, qi, past_block, (init,) * n_heads)

        diag_start = pl.multiple_of(qi * t, t)
        scores = [lax.dot_general(k_ref[k_tile, pl.ds(diag_start, t), :], q, _NT,
                                  preferred_element_type=F32) for q, k_tile in heads]
        outs = []
        for hh, s in enumerate(scores):
            s = jnp.where(key_i <= qry_i, s, NEG)
            _, l_fin, acc = update(s, vt_ref[0, qi, hh * hd:(hh + 1) * hd, :], states[hh])
            outs.append(acc / l_fin)
        o_ref[0] = jnp.concatenate(outs, axis=0).T.astype(BF16)

    bounded = bounded_ref[0] == 1

    @pl.when(bounded)
    def _():
        softmax_attend(True)

    @pl.when(jnp.logical_not(bounded))
    def _():
        softmax_attend(False)


def _logits_bounded(q_gain, k_gain, dims):
    bound = 1.02 * dims * jnp.max(jnp.abs(q_gain)) * jnp.max(jnp.abs(k_gain))
    return jnp.where(bound <= LOGIT_LIMIT, 1, 0).astype(jnp.int32).reshape(1)


def _attention(q, k, vt, kmean, bounded, *, moba):
    b, s, _ = q.shape
    v_cols = ATTN_GROUPS * LANES
    steps = vt.shape[2] // v_cols
    nblk = s // ATTN_TILE
    qk_cols = v_cols if moba else 2 * v_cols
    k_tiles = qk_cols // LANES
    in_specs = [
        pl.BlockSpec((1, ATTN_TILE, qk_cols), lambda bi, gi, qi, fl: (bi, qi, gi)),
        pl.BlockSpec((k_tiles, s, LANES), lambda bi, gi, qi, fl: (gi, bi, 0)),
        pl.BlockSpec((1, nblk, v_cols, ATTN_TILE), lambda bi, gi, qi, fl: (bi, 0, gi, 0)),
    ]
    args = [q, k, vt]
    if moba:
        in_specs.append(pl.BlockSpec((1, GATE_ROWS, v_cols), lambda bi, gi, qi, fl: (bi, 0, gi)))
        args.append(kmean)
    return pl.pallas_call(
        functools.partial(_attn_kernel, moba=moba),
        grid_spec=pltpu.PrefetchScalarGridSpec(
            num_scalar_prefetch=1,
            grid=(b, steps, nblk),
            in_specs=in_specs,
            out_specs=pl.BlockSpec((1, ATTN_TILE, v_cols), lambda bi, gi, qi, fl: (bi, qi, gi))),
        out_shape=jax.ShapeDtypeStruct((b, s, steps * v_cols), BF16),
        compiler_params=pltpu.CompilerParams(
            dimension_semantics=("arbitrary", "arbitrary", "arbitrary"),
            vmem_limit_bytes=VMEM_LIMIT),
        name="moba_attention" if moba else "mla_attention",
    )(bounded, *args)


def _merge_kernel(x_ref, ya_ref, yb_ref, ga_ref, gb_ref, wa_ref, wb_ref, wo_ref, gffn_ref,
                  wr_ref, br_ref, x1_ref, h_ref, route_ref, meta_ref, cnt_ref, run_scr):
    t = TOKEN_TILE

    @pl.when(pl.program_id(0) == 0)
    def _():
        run_scr[...] = jnp.zeros_like(run_scr)

    merged = (ga_ref[...].astype(F32) * jnp.dot(ya_ref[...], wa_ref[...], preferred_element_type=F32)
              + gb_ref[...].astype(F32) * jnp.dot(yb_ref[...], wb_ref[...], preferred_element_type=F32))
    x1 = x_ref[...] + jnp.dot(merged.astype(BF16), wo_ref[...], preferred_element_type=F32)
    x1_ref[...] = x1
    h = _rms(x1, gffn_ref[...]).astype(BF16)
    h_ref[...] = h

    logits = lax.dot_general(wr_ref[...], h, _NT, preferred_element_type=F32) + br_ref[...]
    e_i = lax.broadcasted_iota(jnp.int32, (N_EXPERTS, t), 0)
    lg = logits
    hits = []
    top = None
    for r in range(TOP_K):
        gmax = jnp.max(lg, axis=0, keepdims=True)
        pick = jnp.min(jnp.where(lg == gmax, e_i, N_EXPERTS), axis=0, keepdims=True)
        hit = e_i == pick
        if r == 0:
            top = gmax
        hits.append(hit)
        lg = jnp.where(hit, -jnp.inf, lg)
    sel = jnp.where(lg == -jnp.inf, 1.0, 0.0)
    wgt = sel * jnp.exp(logits - top)
    wgt = wgt / jnp.sum(wgt, axis=0, keepdims=True)

    r_i = lax.broadcasted_iota(jnp.int32, (t, t), 0)
    c_i = lax.broadcasted_iota(jnp.int32, (t, t), 1)
    earlier = jnp.where(r_i < c_i, 1.0, 0.0).astype(BF16)
    rank_in_tile = jnp.dot(sel.astype(BF16), earlier, preferred_element_type=F32)
    tcnt = jnp.sum(sel, axis=1, keepdims=True)
    tpad = jnp.floor((tcnt + (RUN_CHUNK - 1)) * (1.0 / RUN_CHUNK)) * RUN_CHUNK
    e_r = lax.broadcasted_iota(jnp.int32, (N_EXPERTS, N_EXPERTS), 0)
    e_c = lax.broadcasted_iota(jnp.int32, (N_EXPERTS, N_EXPERTS), 1)
    before = jnp.where(e_c < e_r, 1.0, 0.0).astype(BF16)
    tbase = jnp.dot(before, jnp.broadcast_to(tpad, (N_EXPERTS, LANES)).astype(BF16),
                    preferred_element_type=F32)
    n_chunks = jnp.sum(tpad, axis=0, keepdims=True) * (1.0 / RUN_CHUNK)
    run = run_scr[...]
    run_new = run + tcnt
    run_scr[...] = run_new
    cnt_ref[...] = run_new
    lane = lax.broadcasted_iota(jnp.int32, (N_EXPERTS, LANES), 1)
    meta_ref[0] = jnp.where(lane == 0, tcnt, jnp.where(lane == 1, run, jnp.where(
        lane == 2, tbase, jnp.where(lane == 3, n_chunks, 0.0))))

    slot_of = rank_in_tile + tbase[:, 0:1]
    row = lax.broadcasted_iota(jnp.int32, (2 * TOP_K, t), 0)
    route = jnp.zeros((2 * TOP_K, t), F32)
    for r in range(TOP_K):
        w_r = jnp.sum(jnp.where(hits[r], wgt, 0.0), axis=0, keepdims=True)
        slot_r = jnp.sum(jnp.where(hits[r], slot_of, 0.0), axis=0, keepdims=True)
        route = jnp.where(row == r, w_r, route)
        route = jnp.where(row == TOP_K + r, slot_r, route)
    route_ref[0] = route


BIG_COPY = 4


def _for_each_run_copy(meta_ref, pstart_ref, fn):
    big_rows = BIG_COPY * RUN_CHUNK

    def per_expert(e, carry):
        n_chunks = lax.shift_right_logical(meta_ref[e] + (RUN_CHUNK - 1), RUN_CHUNK.bit_length() - 1)
        n_big = lax.shift_right_logical(n_chunks, BIG_COPY.bit_length() - 1)
        slot0 = meta_ref[2 * LANES + e]
        row0 = pstart_ref[e] + meta_ref[LANES + e]

        def big(b, cc):
            fn(slot0 + b * big_rows, row0 + b * big_rows, big_rows)
            return cc

        def small(c, cc):
            fn(slot0 + c * RUN_CHUNK, row0 + c * RUN_CHUNK, RUN_CHUNK)
            return cc

        lax.fori_loop(0, n_big, big, 0)
        lax.fori_loop(n_big * BIG_COPY, n_chunks, small, 0)
        return carry

    lax.fori_loop(0, N_EXPERTS, per_expert, 0)


WAIT_BATCH = 16


def _wait_run_chunks(meta_ref, make_copy):
    n_chunks = meta_ref[3 * LANES]
    n_batches = lax.shift_right_logical(n_chunks, WAIT_BATCH.bit_length() - 1)
    lax.fori_loop(0, n_batches, lambda b, c: (make_copy(WAIT_BATCH * RUN_CHUNK).wait(), c)[1], 0)
    lax.fori_loop(0, n_chunks - n_batches * WAIT_BATCH,
                  lambda b, c: (make_copy(RUN_CHUNK).wait(), c)[1], 0)


def _slab_rows(first_row, n_rows):
    return pl.ds(pl.multiple_of(first_row * ROW_SLABS, ROW_SLABS), n_rows * ROW_SLABS)


def _dispatch_kernel(pstart_ref, pend_ref, nact_ref, meta_ref, meta_prev_ref, route_ref, h_ref,
                     xr_hbm, stage, zero_scr, sems, zsem):
    t = TOKEN_TILE
    i = pl.program_id(0)
    half = lax.rem(i, 2)

    def run_copy(buf_half, slot, row, n_rows=RUN_CHUNK):
        src = stage.at[_slab_rows(buf_half * RUN_SLOTS + slot, n_rows), :]
        return pltpu.make_async_copy(src, xr_hbm.at[_slab_rows(row, n_rows), :], sems.at[buf_half])

    @pl.when(i == 0)
    def _():
        zero_scr[...] = jnp.zeros_like(zero_scr)
        n_blocks = xr_hbm.shape[0] // (EXPERT_ROWS * ROW_SLABS)

        def block_copy(first_row):
            return pltpu.make_async_copy(zero_scr, xr_hbm.at[_slab_rows(first_row, EXPERT_ROWS), :], zsem)

        def pad_blocks(e):
            seg_blocks = lax.shift_right_logical(pend_ref[e] - pstart_ref[e],
                                                 EXPERT_ROWS.bit_length() - 1)
            return jnp.minimum(seg_blocks, 2)

        def zero_pad(e, c):
            lax.fori_loop(1, pad_blocks(e) + 1,
                          lambda b, cc: (block_copy(pend_ref[e] - b * EXPERT_ROWS).start(), cc)[1], 0)
            return c

        def wait_pad(e, c):
            lax.fori_loop(0, pad_blocks(e), lambda b, cc: (block_copy(0).wait(), cc)[1], 0)
            return c

        def zero_tail(blk, c):
            block_copy(blk * EXPERT_ROWS).start()
            return c

        lax.fori_loop(0, N_EXPERTS, zero_pad, 0)
        lax.fori_loop(nact_ref[0], n_blocks, zero_tail, 0)
        lax.fori_loop(0, N_EXPERTS, wait_pad, 0)
        lax.fori_loop(0, n_blocks - nact_ref[0], lambda r, c: (block_copy(0).wait(), c)[1], 0)

    slot_rows = route_ref[0, TOP_K:2 * TOP_K, :]
    s_i = lax.broadcasted_iota(jnp.int32, (RUN_SLOTS, t), 0).astype(F32)
    pick = jnp.zeros((RUN_SLOTS, t), F32)
    for k in range(TOP_K):
        pick = jnp.where(s_i == slot_rows[k:k + 1, :], 1.0, pick)
    rows = jnp.dot(pick.astype(BF16), h_ref[...], preferred_element_type=F32)
    base = half * (RUN_SLOTS * ROW_SLABS)
    for c in range(ROW_SLABS):
        stage[pl.ds(base + c, RUN_SLOTS, stride=ROW_SLABS), :] = rows[:, c * LANES:(c + 1) * LANES]

    @pl.when(i > 0)
    def _():
        _wait_run_chunks(meta_prev_ref, lambda n_rows: run_copy(1 - half, 0, 0, n_rows))

    _for_each_run_copy(meta_ref, pstart_ref, lambda s, r, n: run_copy(half, s, r, n).start())

    @pl.when(i == pl.num_programs(0) - 1)
    def _():
        _wait_run_chunks(meta_ref, lambda n_rows: run_copy(half, 0, 0, n_rows))


def _expert_kernel(blk_e_ref, nact_ref, xr_ref, wgu_hbm, bgu_ref, wdn_hbm, bdn_ref, yr_ref,
                   wgu_f32, wdn_f32, wgu_bf, wdn_bf, sems):
    r = EXPERT_ROWS
    i = pl.program_id(0)
    e = blk_e_ref[i]
    active = i < nact_ref[0]
    new_expert = jnp.logical_or(i == 0, e != blk_e_ref[jnp.maximum(i - 1, 0)])

    def weight_copies(expert):
        half = lax.rem(expert, 2)
        return (pltpu.make_async_copy(wgu_hbm.at[expert], wgu_f32.at[half], sems.at[0, half]),
                pltpu.make_async_copy(wdn_hbm.at[expert], wdn_f32.at[half], sems.at[1, half]))

    @pl.when(i == 0)
    def _():
        for cp in weight_copies(e):
            cp.start()

    @pl.when(jnp.logical_and(active, new_expert))
    def _():
        for cp in weight_copies(e):
            cp.wait()

        @pl.when(e + 1 < N_EXPERTS)
        def _():
            for cp in weight_copies(e + 1):
                cp.start()

        half = lax.rem(e, 2)

        def cast_rows(c, carry):
            rows = pl.ds(pl.multiple_of(c * LANES, LANES), LANES)
            wgu_bf[rows, :] = wgu_f32[half, rows, :].astype(BF16)
            wdn_bf[rows, :] = wdn_f32[half, rows, :].astype(BF16)
            return carry

        lax.fori_loop(0, D_MODEL // LANES, cast_rows, 0)

    @pl.when(active)
    def _():
        x = jnp.concatenate(
            [xr_ref[pl.ds(c, r, stride=ROW_SLABS), :] for c in range(ROW_SLABS)], axis=1)
        gu = jnp.dot(x.astype(BF16), wgu_bf[...], preferred_element_type=F32) + bgu_ref[0]
        g = jnp.minimum(gu[:, :D_EXPERT], SWIGLU_LIMIT)
        u = jnp.clip(gu[:, D_EXPERT:], -SWIGLU_LIMIT, SWIGLU_LIMIT)
        act = (u + 1.0) * (g * jax.nn.sigmoid(SWIGLU_ALPHA * g))
        y = jnp.dot(act.astype(BF16), wdn_bf[...], preferred_element_type=F32) + bdn_ref[0]
        for c in range(ROW_SLABS):
            yr_ref[pl.ds(c, r, stride=ROW_SLABS), :] = y[:, c * LANES:(c + 1) * LANES]

    @pl.when(jnp.logical_not(active))
    def _():
        yr_ref[...] = jnp.zeros_like(yr_ref)


def _final_kernel(pstart_ref, meta_ref, meta_next_ref, x1_ref, route_ref, p_ref, gple_ref,
                  wpg_ref, wpp_ref, yr_hbm, o_ref, gstage, sems):
    t = TOKEN_TILE
    i = pl.program_id(0)
    half = lax.rem(i, 2)

    def run_copy(buf_half, slot, row, n_rows=RUN_CHUNK):
        dst = gstage.at[_slab_rows(buf_half * RUN_SLOTS + slot, n_rows), :]
        return pltpu.make_async_copy(yr_hbm.at[_slab_rows(row, n_rows), :], dst, sems.at[buf_half])

    @pl.when(i == 0)
    def _():
        gstage[...] = jnp.zeros_like(gstage)
        _for_each_run_copy(meta_ref, pstart_ref, lambda s, r, n: run_copy(0, s, r, n).start())

    @pl.when(i + 1 < pl.num_programs(0))
    def _():
        _for_each_run_copy(meta_next_ref, pstart_ref,
                           lambda s, r, n: run_copy(1 - half, s, r, n).start())

    _wait_run_chunks(meta_ref, lambda n_rows: run_copy(half, 0, 0, n_rows))

    base = half * (RUN_SLOTS * ROW_SLABS)
    rows = jnp.concatenate(
        [gstage[pl.ds(base + c, RUN_SLOTS, stride=ROW_SLABS), :] for c in range(ROW_SLABS)], axis=1)
    route = route_ref[...]
    s_i = lax.broadcasted_iota(jnp.int32, (t, RUN_SLOTS), 1).astype(F32)
    wmat = jnp.zeros((t, RUN_SLOTS), F32)
    for k in range(TOP_K):
        wmat = jnp.where(s_i == route[:, TOP_K + k:TOP_K + k + 1], route[:, k:k + 1], wmat)
    rows_hi = rows.astype(BF16)
    rows_lo = (rows - rows_hi.astype(F32)).astype(BF16)
    w_hi = wmat.astype(BF16)
    w_lo = (wmat - w_hi.astype(F32)).astype(BF16)
    y = (jnp.dot(w_hi, rows_hi, preferred_element_type=F32)
         + jnp.dot(w_lo, rows_hi, preferred_element_type=F32)
         + jnp.dot(w_hi, rows_lo, preferred_element_type=F32))
    x2 = x1_ref[...] + y
    hp = _rms(x2, gple_ref[...]).astype(BF16)
    gate = jax.nn.sigmoid(jnp.dot(hp, wpg_ref[...], preferred_element_type=F32))
    emb = jnp.dot(p_ref[...].astype(BF16), wpp_ref[...], preferred_element_type=F32)
    o_ref[...] = x2 + gate * emb


def _rope_tables(s, half, x1_starts, x2_starts):
    inv_freq = ROPE_THETA ** (-(np.arange(half, dtype=np.float64) / half))
    ang = np.arange(s, dtype=np.float64)[:, None] * inv_freq[None, :]
    cos, sin = np.cos(ang), np.sin(ang)
    cos_t, sin_t = np.ones((s, LANES), np.float32), np.zeros((s, LANES), np.float32)
    for st in x1_starts:
        cos_t[:, st:st + half] = cos
        sin_t[:, st:st + half] = -sin
    for st in x2_starts:
        cos_t[:, st:st + half] = cos
        sin_t[:, st:st + half] = sin
    return jnp.asarray(cos_t), jnp.asarray(sin_t)


_MOBA_LANE_COLS = tuple(list(range(0, 32)) + list(range(64, 96)) + list(range(32, 64))
                        + list(range(96, 128)))
_MLA_LANE_DIMS = tuple(list(range(80, 96)) + list(range(0, 48)) + list(range(64, 80))
                       + list(range(48, 64)) + [MLA_QK_DIM] * 32)


def _moba_lanes(w):
    k, width = w.shape
    cols = jnp.asarray(_MOBA_LANE_COLS, jnp.int32)
    return w.reshape(k, width // LANES, LANES)[:, :, cols].reshape(k, width)


def _mla_lanes(w, heads):
    k = w.shape[0]
    w = jnp.pad(w.reshape(k, heads, MLA_QK_DIM), ((0, 0), (0, 0), (0, 1)))
    return w[:, :, jnp.asarray(_MLA_LANE_DIMS, jnp.int32)].reshape(k, heads * LANES)


def _row(v):
    return v.reshape(1, -1).astype(F32)


def _layer(x, p_i, g_mix, w_in, moba_q_norm, moba_k_norm, mla_q_lat_norm, w_uq, mla_kv_lat_norm,
           w_ukv, mla_q_norm, mla_k_norm, w_branch_a, w_branch_b, w_out, g_ffn, w_router, b_router,
           w_gate_up, b_gate_up, w_down, b_down, g_ple, w_ple_gate, w_ple_proj):
    b, s, d = x.shape
    n = b * s
    assert d == D_MODEL and s % ATTN_TILE == 0
    assert s // MOBA_BLOCK <= GATE_ROWS and TOKEN_TILE == ATTN_TILE == MOBA_BLOCK
    assert D_EXPERT == D_MODEL and RUN_CHUNK == SUBLANES
    n_tiles = n // TOKEN_TILE
    tiles_per_seq = s // TOKEN_TILE
    xf = x.reshape(n, d)

    off = [0]
    for wdt in (MOBA_WIDTH, MOBA_WIDTH, MOBA_WIDTH, MLA_Q_LORA, MLA_KV_LORA, MLA_ROPE_DIM, D_MODEL, D_MODEL):
        off.append(off[-1] + wdt)
    seg = [w_in[:, off[i]:off[i + 1]] for i in range(8)]
    kpe_cols = _mla_lanes(jnp.pad(seg[5], ((0, 0), (MLA_NOPE_DIM, 0))), 1)
    w_in_p = jnp.concatenate([_moba_lanes(seg[0]), _moba_lanes(seg[1])] + seg[3:5] + [kpe_cols]
                             + seg[6:], axis=1).astype(BF16)
    assert w_in_p.shape[1] == D_IN_PACKED
    w_va_t = seg[2].T.astype(BF16)
    w_uq_p = _mla_lanes(w_uq, MLA_HEADS).astype(BF16)
    w_ukv_h = w_ukv.reshape(MLA_KV_LORA, MLA_HEADS, MLA_NOPE_DIM + MLA_V_DIM)
    w_uk_p = _mla_lanes(jnp.pad(w_ukv_h[:, :, :MLA_NOPE_DIM], ((0, 0), (0, 0), (0, MLA_ROPE_DIM)))
                        .reshape(MLA_KV_LORA, -1), MLA_HEADS).astype(BF16)
    w_uv_t = w_ukv_h[:, :, MLA_NOPE_DIM:].reshape(MLA_KV_LORA, MLA_WIDTH).T.astype(BF16)
    gqa = _moba_lanes(_row(jnp.tile(moba_q_norm, 2))) * (MOBA_HEAD_DIM ** -0.5 * LOG2_E)
    gka = _moba_lanes(_row(jnp.tile(moba_k_norm, 2)))
    gqb = _mla_lanes(_row(mla_q_norm), 1) * (MLA_QK_DIM ** -0.5 * LOG2_E)
    gkb = _mla_lanes(_row(mla_k_norm), 1)
    half_a, half_b = MOBA_HEAD_DIM // 2, MLA_ROPE_DIM // 2
    cosa, sina = _rope_tables(s, half_a, (0, half_a), (LANES // 2, LANES // 2 + half_a))
    cosb, sinb = _rope_tables(s, half_b, (LANES // 2,), (0,))

    tok = lambda width: pl.BlockSpec((TOKEN_TILE, width), lambda i: (i, 0))
    whole = lambda arr: pl.BlockSpec(arr.shape, lambda i: (0,) * arr.ndim)
    seq_tab = pl.BlockSpec((TOKEN_TILE, LANES), lambda i: (i % tiles_per_seq, 0))
    vt_spec = pl.BlockSpec((1, MOBA_WIDTH, TOKEN_TILE), lambda i: (i, 0, 0))
    k_tiles_spec = lambda tiles: pl.BlockSpec((tiles, TOKEN_TILE, LANES), lambda i: (0, i, 0))
    params = pltpu.CompilerParams(dimension_semantics=("arbitrary",), vmem_limit_bytes=VMEM_LIMIT)

    consts1 = [_row(g_mix), w_in_p, w_va_t, gqa, gka]
    consts2 = [_row(mla_q_lat_norm), w_uq_p, _row(mla_kv_lat_norm), w_uk_p, w_uv_t, gqb, gkb]
    qa, ka, va, kmean, qb, kb, vb, ga, gb = pl.pallas_call(
        _inproj_kernel,
        grid=(n_tiles,),
        in_specs=([tok(d)] + [whole(a) for a in consts1] + [seq_tab] * 2
                  + [whole(a) for a in consts2] + [seq_tab] * 2),
        out_specs=[tok(MOBA_WIDTH), k_tiles_spec(MOBA_WIDTH // LANES), vt_spec,
                   pl.BlockSpec((1, 1, MOBA_WIDTH), lambda i: (i, 0, 0)),
                   tok(MLA_HEADS * LANES), k_tiles_spec(MLA_HEADS), vt_spec,
                   tok(d), tok(d)],
        out_shape=[jax.ShapeDtypeStruct((n, MOBA_WIDTH), BF16),
                   jax.ShapeDtypeStruct((MOBA_WIDTH // LANES, n, LANES), BF16),
                   jax.ShapeDtypeStruct((n_tiles, MOBA_WIDTH, TOKEN_TILE), BF16),
                   jax.ShapeDtypeStruct((n_tiles, 1, MOBA_WIDTH), F32),
                   jax.ShapeDtypeStruct((n, MLA_HEADS * LANES), BF16),
                   jax.ShapeDtypeStruct((MLA_HEADS, n, LANES), BF16),
                   jax.ShapeDtypeStruct((n_tiles, MLA_WIDTH, TOKEN_TILE), BF16)]
        + [jax.ShapeDtypeStruct((n, d), BF16)] * 2,
        compiler_params=params,
        name="in_projection",
    )(xf, *consts1, cosa, sina, *consts2, cosb, sinb)

    kmean = kmean.reshape(b, tiles_per_seq, MOBA_WIDTH)
    kmean = jnp.pad(kmean, ((0, 0), (0, GATE_ROWS - tiles_per_seq), (0, 0))).astype(BF16)

    r3 = lambda a: a.reshape(b, s, a.shape[-1])
    r4 = lambda a: a.reshape(b, tiles_per_seq, a.shape[1], TOKEN_TILE)
    ya = _attention(r3(qa), ka, r4(va), kmean, _logits_bounded(gqa, gka, MOBA_HEAD_DIM),
                    moba=True).reshape(n, MOBA_WIDTH)
    yb = _attention(r3(qb), kb, r4(vb), None, _logits_bounded(gqb, gkb, MLA_QK_DIM),
                    moba=False).reshape(n, MLA_WIDTH)

    consts3 = [w_branch_a.astype(BF16), w_branch_b.astype(BF16), w_out.astype(BF16), _row(g_ffn),
               w_router.T.astype(BF16), b_router.reshape(N_EXPERTS, 1).astype(F32)]
    x1, h_ffn, route_t, meta, cnt = pl.pallas_call(
        _merge_kernel,
        grid=(n_tiles,),
        in_specs=[tok(d), tok(MOBA_WIDTH), tok(MLA_WIDTH), tok(d), tok(d)]
        + [whole(a) for a in consts3],
        out_specs=[tok(d), tok(d),
                   pl.BlockSpec((1, 2 * TOP_K, TOKEN_TILE), lambda i: (i, 0, 0)),
                   pl.BlockSpec((1, N_EXPERTS, LANES), lambda i: (i, 0, 0)),
                   pl.BlockSpec((N_EXPERTS, LANES), lambda i: (0, 0))],
        out_shape=[jax.ShapeDtypeStruct((n, d), F32),
                   jax.ShapeDtypeStruct((n, d), BF16),
                   jax.ShapeDtypeStruct((n_tiles, 2 * TOP_K, TOKEN_TILE), F32),
                   jax.ShapeDtypeStruct((n_tiles, N_EXPERTS, LANES), F32),
                   jax.ShapeDtypeStruct((N_EXPERTS, LANES), F32)],
        scratch_shapes=[pltpu.VMEM((N_EXPERTS, LANES), F32)],
        compiler_params=params,
        name="merge_router",
    )(xf, ya, yb, ga, gb, *consts3)
    route = jnp.pad(route_t.transpose(0, 2, 1).reshape(n, 2 * TOP_K),
                    ((0, 0), (0, LANES - 2 * TOP_K)))

    rb = EXPERT_ROWS
    n_blocks = -(-(n * TOP_K + N_EXPERTS * (RUN_CHUNK - 1)) // rb) + N_EXPERTS
    n_rows = n_blocks * rb
    counts = cnt[:, 0].astype(jnp.int32)
    pcounts = ((counts + (RUN_CHUNK - 1) + rb - 1) // rb) * rb
    pends = jnp.cumsum(pcounts).astype(jnp.int32)
    pstarts = (pends - pcounts).astype(jnp.int32)
    nact = (pends[-1] // rb).astype(jnp.int32).reshape(1)
    blk = jnp.minimum(jnp.arange(n_blocks, dtype=jnp.int32), nact[0] - 1)
    blk_e = jnp.sum((pends[None, :] <= (blk * rb)[:, None]).astype(jnp.int32), axis=1)
    blk_e = jnp.minimum(blk_e, N_EXPERTS - 1)
    meta_i = jnp.pad(meta[:, :, :4].transpose(0, 2, 1), ((0, 0), (0, 0), (0, LANES - N_EXPERTS)))
    meta_i = meta_i.astype(jnp.int32).reshape(n_tiles * 4 * LANES)
    meta_spec = lambda shift: pl.BlockSpec(
        (4 * LANES,), lambda i, *_: (jnp.clip(i + shift, 0, n_tiles - 1),), memory_space=pltpu.SMEM)

    xr = pl.pallas_call(
        _dispatch_kernel,
        grid_spec=pltpu.PrefetchScalarGridSpec(
            num_scalar_prefetch=3,
            grid=(n_tiles,),
            in_specs=[meta_spec(0), meta_spec(-1),
                      pl.BlockSpec((1, 2 * TOP_K, TOKEN_TILE), lambda i, *_: (i, 0, 0)),
                      pl.BlockSpec((TOKEN_TILE, d), lambda i, *_: (i, 0))],
            out_specs=pl.BlockSpec(memory_space=pl.ANY),
            scratch_shapes=[pltpu.VMEM((2 * RUN_SLOTS * ROW_SLABS, LANES), F32),
                            pltpu.VMEM((EXPERT_ROWS * ROW_SLABS, LANES), F32),
                            pltpu.SemaphoreType.DMA((2,)), pltpu.SemaphoreType.DMA(())]),
        out_shape=jax.ShapeDtypeStruct((n_rows * ROW_SLABS, LANES), F32),
        compiler_params=params,
        name="dispatch_rows",
    )(pstarts, pends, nact, meta_i, meta_i, route_t, h_ffn)

    act_blk = lambda i, be, na: jnp.minimum(i, na[0] - 1)
    yr = pl.pallas_call(
        _expert_kernel,
        grid_spec=pltpu.PrefetchScalarGridSpec(
            num_scalar_prefetch=2,
            grid=(n_blocks,),
            in_specs=[
                pl.BlockSpec((rb * ROW_SLABS, LANES), lambda i, be, na: (act_blk(i, be, na), 0)),
                pl.BlockSpec(memory_space=pl.ANY),
                pl.BlockSpec((1, 1, 2 * D_EXPERT), lambda i, be, na: (be[i], 0, 0)),
                pl.BlockSpec(memory_space=pl.ANY),
                pl.BlockSpec((1, 1, d), lambda i, be, na: (be[i], 0, 0)),
            ],
            out_specs=pl.BlockSpec((rb * ROW_SLABS, LANES), lambda i, be, na: (i, 0)),
            scratch_shapes=[pltpu.VMEM((2, d, 2 * D_EXPERT), F32), pltpu.VMEM((2, D_EXPERT, d), F32),
                            pltpu.VMEM((d, 2 * D_EXPERT), BF16), pltpu.VMEM((D_EXPERT, d), BF16),
                            pltpu.SemaphoreType.DMA((2, 2))]),
        out_shape=jax.ShapeDtypeStruct((n_rows * ROW_SLABS, LANES), F32),
        compiler_params=params,
        name="experts",
    )(blk_e, nact, xr, w_gate_up.astype(F32),
      b_gate_up.reshape(N_EXPERTS, 1, -1).astype(F32), w_down.astype(F32),
      b_down.reshape(N_EXPERTS, 1, -1).astype(F32))

    ftok = lambda width: pl.BlockSpec((TOKEN_TILE, width), lambda i, ps: (i, 0))
    fwhole = lambda arr: pl.BlockSpec(arr.shape, lambda i, ps: (0,) * arr.ndim)
    consts4 = [_row(g_ple), w_ple_gate.astype(BF16), w_ple_proj.astype(BF16)]
    out = pl.pallas_call(
        _final_kernel,
        grid_spec=pltpu.PrefetchScalarGridSpec(
            num_scalar_prefetch=1,
            grid=(n_tiles,),
            in_specs=[meta_spec(0), meta_spec(1), ftok(d), ftok(LANES), ftok(PLE_DIM)]
            + [fwhole(a) for a in consts4] + [pl.BlockSpec(memory_space=pl.ANY)],
            out_specs=ftok(d),
            scratch_shapes=[pltpu.VMEM((2 * RUN_SLOTS * ROW_SLABS, LANES), F32),
                            pltpu.SemaphoreType.DMA((2,))]),
        out_shape=jax.ShapeDtypeStruct((n, d), F32),
        compiler_params=params,
        name="combine_ple",
    )(pstarts, meta_i, meta_i, x1, route, p_i.reshape(n, PLE_DIM), *consts4, yr)
    return out.reshape(b, s, d)


def kernel(x, p, g_mix, w_in, moba_q_norm, moba_k_norm, mla_q_lat_norm, w_uq, mla_kv_lat_norm, w_ukv, mla_q_norm, mla_k_norm, w_branch_a, w_branch_b, w_out, g_ffn, w_router, b_router, w_gate_up, b_gate_up, w_down, b_down, g_ple, w_ple_gate, w_ple_proj):
    for i in range(p.shape[0]):
        x = _layer(x, p[i], g_mix[i], w_in[i], moba_q_norm[i], moba_k_norm[i], mla_q_lat_norm[i],
                   w_uq[i], mla_kv_lat_norm[i], w_ukv[i], mla_q_norm[i], mla_k_norm[i],
                   w_branch_a[i], w_branch_b[i], w_out[i], g_ffn[i], w_router[i], b_router[i],
                   w_gate_up[i], b_gate_up[i], w_down[i], b_down[i], g_ple[i], w_ple_gate[i],
                   w_ple_proj[i])
    return x
```

```python
import functools

import numpy as np
import jax
import jax.numpy as jnp
from jax import lax
from jax.experimental import pallas as pl
from jax.experimental.pallas import tpu as pltpu

F32 = jnp.float32
BF16 = jnp.bfloat16

D_MODEL = 1024
PLE_DIM = 256
EPS = 1e-6
ROPE_THETA = 10000.0
MOBA_HEADS = 8
MOBA_HEAD_DIM = 64
MOBA_BLOCK = 256
MOBA_TOPK = 3
MOBA_WIDTH = MOBA_HEADS * MOBA_HEAD_DIM
MLA_HEADS = 8
MLA_Q_LORA = 256
MLA_KV_LORA = 128
MLA_NOPE_DIM = 64
MLA_ROPE_DIM = 32
MLA_V_DIM = 64
MLA_QK_DIM = MLA_NOPE_DIM + MLA_ROPE_DIM
MLA_WIDTH = MLA_HEADS * MLA_V_DIM
N_EXPERTS = 32
TOP_K = 4
D_EXPERT = 1024
SWIGLU_LIMIT = 7.0
SWIGLU_ALPHA = 1.702

LANES = 128
SUBLANES = 8
ROW_SLABS = D_MODEL // LANES
VMEM_LIMIT = 56 * 1024 * 1024

TOKEN_TILE = 256
ATTN_TILE = 256
ATTN_GROUPS = 4
EXPERT_ROWS = 512
RUN_CHUNK = 8
RUN_SLOTS = -(-(TOKEN_TILE * TOP_K + N_EXPERTS * (RUN_CHUNK - 1)) // 256) * 256

LOG2_E = 1.4426950408889634
LOGIT_LIMIT = 60.0
NEG = -1e30
MASK_BIAS = -1e9

C_QA, C_KA = 0, 512
C_CQ, C_CKV, C_KPE = 1024, 1280, 1408
C_GA, C_GB = 1536, 2560
D_IN_PACKED = 3584


def _rms(x, gain):
    return x * lax.rsqrt(jnp.mean(x * x, axis=-1, keepdims=True) + EPS) * gain


def _rope(t, cos, sin):
    return t * cos + pltpu.roll(t, LANES // 2, 1) * sin


def _moba_even_head(lane):
    return (lane & (MOBA_HEAD_DIM // 2)) == 0


def _inproj_kernel(x_ref, gmix_ref, win_ref, wvat_ref, gqa_ref, gka_ref, cosa_ref, sina_ref,
                   gql_ref, wuq_ref, gkvl_ref, wuk_ref, wuvt_ref, gqb_ref, gkb_ref,
                   cosb_ref, sinb_ref,
                   qa_ref, ka_ref, vat_ref, kmean_ref, qb_ref, kb_ref, vbt_ref, ga_ref, gb_ref):
    hn = _rms(x_ref[...], gmix_ref[...]).astype(BF16)

    def proj(c0, width):
        return jnp.dot(hn, win_ref[:, c0:c0 + width], preferred_element_type=F32)

    first = _moba_even_head(lax.broadcasted_iota(jnp.int32, (TOKEN_TILE, LANES), 1))
    cosa, sina = cosa_ref[...], sina_ref[...]

    def moba_norm_rope(t, gain):
        sq = t * t
        ss0 = jnp.sum(jnp.where(first, sq, 0.0), axis=-1, keepdims=True)
        ss1 = jnp.sum(jnp.where(first, 0.0, sq), axis=-1, keepdims=True)
        ms = jnp.where(first, ss0, ss1) * (1.0 / MOBA_HEAD_DIM)
        t = t * lax.rsqrt(ms + EPS) * gain
        return _rope(t, cosa, sina)

    qa = proj(C_QA, MOBA_WIDTH)
    ka = proj(C_KA, MOBA_WIDTH)
    for c in range(MOBA_WIDTH // LANES):
        sl = slice(c * LANES, (c + 1) * LANES)
        qa_ref[:, sl] = moba_norm_rope(qa[:, sl], gqa_ref[...]).astype(BF16)
        kc = moba_norm_rope(ka[:, sl], gka_ref[...])
        ka_ref[c] = kc.astype(BF16)
        kmean_ref[0, :, sl] = jnp.mean(kc, axis=0, keepdims=True)
    vat_ref[0] = lax.dot_general(wvat_ref[...], hn, _NT, preferred_element_type=F32).astype(BF16)

    cosb, sinb = cosb_ref[...], sinb_ref[...]

    def mla_norm_rope(t, gain):
        ms = jnp.sum(t * t, axis=-1, keepdims=True) * (1.0 / MLA_QK_DIM)
        t = t * lax.rsqrt(ms + EPS) * gain
        return _rope(t, cosb, sinb)

    cq = _rms(proj(C_CQ, MLA_Q_LORA), gql_ref[...]).astype(BF16)
    qb = jnp.dot(cq, wuq_ref[...], preferred_element_type=F32)
    ckv = _rms(proj(C_CKV, MLA_KV_LORA), gkvl_ref[...]).astype(BF16)
    kn = jnp.dot(ckv, wuk_ref[...], preferred_element_type=F32)
    kpe = proj(C_KPE, LANES)
    for h in range(MLA_HEADS):
        sl = slice(h * LANES, (h + 1) * LANES)
        qb_ref[:, sl] = mla_norm_rope(qb[:, sl], gqb_ref[...]).astype(BF16)
        kb_ref[h] = mla_norm_rope(kn[:, sl] + kpe, gkb_ref[...]).astype(BF16)
    vbt_ref[0] = lax.dot_general(wuvt_ref[...], ckv, _NT, preferred_element_type=F32).astype(BF16)

    ga_ref[...] = jax.nn.sigmoid(proj(C_GA, D_MODEL)).astype(BF16)
    gb_ref[...] = jax.nn.sigmoid(proj(C_GB, D_MODEL)).astype(BF16)


_NT = (((1,), (1,)), ((), ()))


GATE_ROWS = 16


def _attn_kernel(*refs, moba):
    if moba:
        bounded_ref, q_ref, k_ref, vt_ref, kmean_ref, o_ref = refs
    else:
        bounded_ref, q_ref, k_ref, vt_ref, o_ref = refs
    t = ATTN_TILE
    hd = MOBA_HEAD_DIM
    n_heads = 2 * ATTN_GROUPS
    qi = pl.program_id(2)
    key_i = lax.broadcasted_iota(jnp.int32, (t, t), 0)
    qry_i = lax.broadcasted_iota(jnp.int32, (t, t), 1)

    blk = lax.broadcasted_iota(jnp.int32, (GATE_ROWS, t), 0)
    heads, biases = [], []
    for hh in range(n_heads):
        if moba:
            lane = lax.broadcasted_iota(jnp.int32, (t, LANES), 1)
            even = _moba_even_head(lane)
            head_lanes = even if hh % 2 == 0 else jnp.logical_not(even)
            k_tile = hh // 2
            kcols = slice(k_tile * LANES, (k_tile + 1) * LANES)
            q = jnp.where(head_lanes, q_ref[0, :, kcols], jnp.zeros((), BF16))
            gate = lax.dot_general(kmean_ref[0, :, kcols], q, _NT, preferred_element_type=F32)
            g = jnp.where(blk < qi, gate, -jnp.inf)
            keep = jnp.zeros((GATE_ROWS, t), F32)
            for _ in range(MOBA_TOPK):
                gmax = jnp.max(g, axis=0, keepdims=True)
                pick = jnp.min(jnp.where(g == gmax, blk, GATE_ROWS), axis=0, keepdims=True)
                hit = blk == jnp.where(gmax > -jnp.inf, pick, GATE_ROWS)
                keep = jnp.where(hit, 1.0, keep)
                g = jnp.where(hit, -jnp.inf, g)
            biases.append(jnp.where(keep > 0.0, 0.0, MASK_BIAS))
        else:
            k_tile = hh
            q = q_ref[0, :, hh * LANES:(hh + 1) * LANES]
        heads.append((q, k_tile))

    def softmax_attend(shift_free):
        def update(s, vt_blk, state):
            m_prev, l_prev, acc = state
            if shift_free:
                p = jnp.exp2(s)
                l_new = l_prev + jnp.sum(p, axis=0, keepdims=True)
                acc = acc + jnp.dot(vt_blk, p.astype(BF16), preferred_element_type=F32)
                return m_prev, l_new, acc
            m_new = jnp.maximum(m_prev, jnp.max(s, axis=0, keepdims=True))
            alpha = jnp.exp2(m_prev - m_new)
            p = jnp.exp2(s - m_new)
            l_new = alpha * l_prev + jnp.sum(p, axis=0, keepdims=True)
            acc = alpha * acc + jnp.dot(vt_blk, p.astype(BF16), preferred_element_type=F32)
            return m_new, l_new, acc

        def past_block(j, states):
            start = pl.multiple_of(j * t, t)
            scores = [lax.dot_general(k_ref[k_tile, pl.ds(start, t), :], q, _NT,
                                      preferred_element_type=F32) for q, k_tile in heads]
            out = []
            for hh, s in enumerate(scores):
                if moba:
                    s = jnp.sum(jnp.where(blk == j, biases[hh], 0.0), axis=0, keepdims=True) + s
                out.append(update(s, vt_ref[0, j, hh * hd:(hh + 1) * hd, :], states[hh]))
            return tuple(out)

        init = (jnp.full((1, t), NEG, F32), jnp.zeros((1, t), F32), jnp.zeros((hd, t), F32))
        states = lax.fori_loop(0, qi, past_block, (init,) * n_heads)

        diag_start = pl.multiple_of(qi * t, t)
        scores = [lax.dot_general(k_ref[k_tile, pl.ds(diag_start, t), :], q, _NT,
                                  preferred_element_type=F32) for q, k_tile in heads]
        outs = []
        for hh, s in enumerate(scores):
            s = jnp.where(key_i <= qry_i, s, NEG)
            _, l_fin, acc = update(s, vt_ref[0, qi, hh * hd:(hh + 1) * hd, :], states[hh])
            outs.append(acc / l_fin)
        o_ref[0] = jnp.concatenate(outs, axis=0).T.astype(BF16)

    bounded = bounded_ref[0] == 1

    @pl.when(bounded)
    def _():
        softmax_attend(True)

    @pl.when(jnp.logical_not(bounded))
    def _():
        softmax_attend(False)


def _logits_bounded(q_gain, k_gain, dims):
    bound = 1.02 * dims * jnp.max(jnp.abs(q_gain)) * jnp.max(jnp.abs(k_gain))
    return jnp.where(bound <= LOGIT_LIMIT, 1, 0).astype(jnp.int32).reshape(1)


def _attention(q, k, vt, kmean, bounded, *, moba):
    b, s, _ = q.shape
    v_cols = ATTN_GROUPS * LANES
    steps = vt.shape[2] // v_cols
    nblk = s // ATTN_TILE
    qk_cols = v_cols if moba else 2 * v_cols
    k_tiles = qk_cols // LANES
    in_specs = [
        pl.BlockSpec((1, ATTN_TILE, qk_cols), lambda bi, gi, qi, fl: (bi, qi, gi)),
        pl.BlockSpec((k_tiles, s, LANES), lambda bi, gi, qi, fl: (gi, bi, 0)),
        pl.BlockSpec((1, nblk, v_cols, ATTN_TILE), lambda bi, gi, qi, fl: (bi, 0, gi, 0)),
    ]
    args = [q, k, vt]
    if moba:
        in_specs.append(pl.BlockSpec((1, GATE_ROWS, v_cols), lambda bi, gi, qi, fl: (bi, 0, gi)))
        args.append(kmean)
    return pl.pallas_call(
        functools.partial(_attn_kernel, moba=moba),
        grid_spec=pltpu.PrefetchScalarGridSpec(
            num_scalar_prefetch=1,
            grid=(b, steps, nblk),
            in_specs=in_specs,
            out_specs=pl.BlockSpec((1, ATTN_TILE, v_cols), lambda bi, gi, qi, fl: (bi, qi, gi))),
        out_shape=jax.ShapeDtypeStruct((b, s, steps * v_cols), BF16),
        compiler_params=pltpu.CompilerParams(
            dimension_semantics=("arbitrary", "arbitrary", "arbitrary"),
            vmem_limit_bytes=VMEM_LIMIT),
        name="moba_attention" if moba else "mla_attention",
    )(bounded, *args)


def _merge_kernel(x_ref, ya_ref, yb_ref, ga_ref, gb_ref, wa_ref, wb_ref, wo_ref, gffn_ref,
                  wr_ref, br_ref, x1_ref, h_ref, route_ref, meta_ref, cnt_ref, run_scr):
    t = TOKEN_TILE

    @pl.when(pl.program_id(0) == 0)
    def _():
        run_scr[...] = jnp.zeros_like(run_scr)

    merged = (ga_ref[...].astype(F32) * jnp.dot(ya_ref[...], wa_ref[...], preferred_element_type=F32)
              + gb_ref[...].astype(F32) * jnp.dot(yb_ref[...], wb_ref[...], preferred_element_type=F32))
    x1 = x_ref[...] + jnp.dot(merged.astype(BF16), wo_ref[...], preferred_element_type=F32)
    x1_ref[...] = x1
    h = _rms(x1, gffn_ref[...]).astype(BF16)
    h_ref[...] = h

    logits = lax.dot_general(wr_ref[...], h, _NT, preferred_element_type=F32) + br_ref[...]
    e_i = lax.broadcasted_iota(jnp.int32, (N_EXPERTS, t), 0)
    lg = logits
    hits = []
    top = None
    for r in range(TOP_K):
        gmax = jnp.max(lg, axis=0, keepdims=True)
        pick = jnp.min(jnp.where(lg == gmax, e_i, N_EXPERTS), axis=0, keepdims=True)
        hit = e_i == pick
        if r == 0:
            top = gmax
        hits.append(hit)
        lg = jnp.where(hit, -jnp.inf, lg)
    sel = jnp.where(lg == -jnp.inf, 1.0, 0.0)
    wgt = sel * jnp.exp(logits - top)
    wgt = wgt / jnp.sum(wgt, axis=0, keepdims=True)

    r_i = lax.broadcasted_iota(jnp.int32, (t, t), 0)
    c_i = lax.broadcasted_iota(jnp.int32, (t, t), 1)
    earlier = jnp.where(r_i < c_i, 1.0, 0.0).astype(BF16)
    rank_in_tile = jnp.dot(sel.astype(BF16), earlier, preferred_element_type=F32)
    tcnt = jnp.sum(sel, axis=1, keepdims=True)
    tpad = jnp.floor((tcnt + (RUN_CHUNK - 1)) * (1.0 / RUN_CHUNK)) * RUN_CHUNK
    e_r = lax.broadcasted_iota(jnp.int32, (N_EXPERTS, N_EXPERTS), 0)
    e_c = lax.broadcasted_iota(jnp.int32, (N_EXPERTS, N_EXPERTS), 1)
    before = jnp.where(e_c < e_r, 1.0, 0.0).astype(BF16)
    tbase = jnp.dot(before, jnp.broadcast_to(tpad, (N_EXPERTS, LANES)).astype(BF16),
                    preferred_element_type=F32)
    n_chunks = jnp.sum(tpad, axis=0, keepdims=True) * (1.0 / RUN_CHUNK)
    run = run_scr[...]
    run_new = run + tcnt
    run_scr[...] = run_new
    cnt_ref[...] = run_new
    lane = lax.broadcasted_iota(jnp.int32, (N_EXPERTS, LANES), 1)
    meta_ref[0] = jnp.where(lane == 0, tcnt, jnp.where(lane == 1, run, jnp.where(
        lane == 2, tbase, jnp.where(lane == 3, n_chunks, 0.0))))

    slot_of = rank_in_tile + tbase[:, 0:1]
    row = lax.broadcasted_iota(jnp.int32, (2 * TOP_K, t), 0)
    route = jnp.zeros((2 * TOP_K, t), F32)
    for r in range(TOP_K):
        w_r = jnp.sum(jnp.where(hits[r], wgt, 0.0), axis=0, keepdims=True)
        slot_r = jnp.sum(jnp.where(hits[r], slot_of, 0.0), axis=0, keepdims=True)
        route = jnp.where(row == r, w_r, route)
        route = jnp.where(row == TOP_K + r, slot_r, route)
    route_ref[0] = route


BIG_COPY = 4


def _for_each_run_copy(meta_ref, pstart_ref, fn):
    big_rows = BIG_COPY * RUN_CHUNK

    def per_expert(e, carry):
        n_chunks = lax.shift_right_logical(meta_ref[e] + (RUN_CHUNK - 1), RUN_CHUNK.bit_length() - 1)
        n_big = lax.shift_right_logical(n_chunks, BIG_COPY.bit_length() - 1)
        slot0 = meta_ref[2 * LANES + e]
        row0 = pstart_ref[e] + meta_ref[LANES + e]

        def big(b, cc):
            fn(slot0 + b * big_rows, row0 + b * big_rows, big_rows)
            return cc

        def small(c, cc):
            fn(slot0 + c * RUN_CHUNK, row0 + c * RUN_CHUNK, RUN_CHUNK)
            return cc

        lax.fori_loop(0, n_big, big, 0)
        lax.fori_loop(n_big * BIG_COPY, n_chunks, small, 0)
        return carry

    lax.fori_loop(0, N_EXPERTS, per_expert, 0)


WAIT_BATCH = 16


def _wait_run_chunks(meta_ref, make_copy):
    n_chunks = meta_ref[3 * LANES]
    n_batches = lax.shift_right_logical(n_chunks, WAIT_BATCH.bit_length() - 1)
    lax.fori_loop(0, n_batches, lambda b, c: (make_copy(WAIT_BATCH * RUN_CHUNK).wait(), c)[1], 0)
    lax.fori_loop(0, n_chunks - n_batches * WAIT_BATCH,
                  lambda b, c: (make_copy(RUN_CHUNK).wait(), c)[1], 0)


def _slab_rows(first_row, n_rows):
    return pl.ds(pl.multiple_of(first_row * ROW_SLABS, ROW_SLABS), n_rows * ROW_SLABS)


ZERO_ROWS = 256


def _dispatch_kernel(pstart_ref, pend_ref, count_ref, nact_ref, meta_ref, meta_prev_ref, route_ref,
                     h_ref, xr_hbm, stage, zero_scr, sems, zsem):
    t = TOKEN_TILE
    i = pl.program_id(0)
    half = lax.rem(i, 2)

    def run_copy(buf_half, slot, row, n_rows=RUN_CHUNK):
        src = stage.at[_slab_rows(buf_half * RUN_SLOTS + slot, n_rows), :]
        return pltpu.make_async_copy(src, xr_hbm.at[_slab_rows(row, n_rows), :], sems.at[buf_half])

    @pl.when(i == 0)
    def _():
        zero_scr[...] = jnp.zeros_like(zero_scr)
        shift = ZERO_ROWS.bit_length() - 1
        n_units = xr_hbm.shape[0] // (ZERO_ROWS * ROW_SLABS)
        first_unused = nact_ref[0] * (EXPERT_ROWS // ZERO_ROWS)

        def zero_copy(first_row):
            return pltpu.make_async_copy(zero_scr, xr_hbm.at[_slab_rows(first_row, ZERO_ROWS), :], zsem)

        def pad_units(e):
            pad = pend_ref[e] - pstart_ref[e] - count_ref[e]
            return lax.shift_right_logical(pad + (ZERO_ROWS - 1), shift)

        def zero_pad(e, c):
            lax.fori_loop(1, pad_units(e) + 1,
                          lambda b, cc: (zero_copy(pend_ref[e] - b * ZERO_ROWS).start(), cc)[1], 0)
            return c

        def wait_pad(e, c):
            lax.fori_loop(0, pad_units(e), lambda b, cc: (zero_copy(0).wait(), cc)[1], 0)
            return c

        def zero_tail(u, c):
            zero_copy(u * ZERO_ROWS).start()
            return c

        lax.fori_loop(0, N_EXPERTS, zero_pad, 0)
        lax.fori_loop(first_unused, n_units, zero_tail, 0)
        lax.fori_loop(0, N_EXPERTS, wait_pad, 0)
        lax.fori_loop(0, n_units - first_unused, lambda r, c: (zero_copy(0).wait(), c)[1], 0)

    slot_rows = route_ref[0, TOP_K:2 * TOP_K, :]
    s_i = lax.broadcasted_iota(jnp.int32, (RUN_SLOTS, t), 0).astype(F32)
    pick = jnp.zeros((RUN_SLOTS, t), F32)
    for k in range(TOP_K):
        pick = jnp.where(s_i == slot_rows[k:k + 1, :], 1.0, pick)
    rows = jnp.dot(pick.astype(BF16), h_ref[...], preferred_element_type=F32)
    base = half * (RUN_SLOTS * ROW_SLABS)
    for c in range(ROW_SLABS):
        stage[pl.ds(base + c, RUN_SLOTS, stride=ROW_SLABS), :] = rows[:, c * LANES:(c + 1) * LANES]

    @pl.when(i > 0)
    def _():
        _wait_run_chunks(meta_prev_ref, lambda n_rows: run_copy(1 - half, 0, 0, n_rows))

    _for_each_run_copy(meta_ref, pstart_ref, lambda s, r, n: run_copy(half, s, r, n).start())

    @pl.when(i == pl.num_programs(0) - 1)
    def _():
        _wait_run_chunks(meta_ref, lambda n_rows: run_copy(half, 0, 0, n_rows))


def _expert_kernel(blk_e_ref, nact_ref, xr_ref, wgu_hbm, bgu_ref, wdn_hbm, bdn_ref, yr_ref,
                   wgu_f32, wdn_f32, wgu_bf, wdn_bf, sems):
    r = EXPERT_ROWS
    i = pl.program_id(0)
    e = blk_e_ref[i]
    active = i < nact_ref[0]
    new_expert = jnp.logical_or(i == 0, e != blk_e_ref[jnp.maximum(i - 1, 0)])

    def weight_copies(expert):
        half = lax.rem(expert, 2)
        return (pltpu.make_async_copy(wgu_hbm.at[expert], wgu_f32.at[half], sems.at[0, half]),
                pltpu.make_async_copy(wdn_hbm.at[expert], wdn_f32.at[half], sems.at[1, half]))

    @pl.when(i == 0)
    def _():
        for cp in weight_copies(e):
            cp.start()

    @pl.when(jnp.logical_and(active, new_expert))
    def _():
        for cp in weight_copies(e):
            cp.wait()

        @pl.when(e + 1 < N_EXPERTS)
        def _():
            for cp in weight_copies(e + 1):
                cp.start()

        half = lax.rem(e, 2)

        def cast_rows(c, carry):
            rows = pl.ds(pl.multiple_of(c * LANES, LANES), LANES)
            wgu_bf[rows, :] = wgu_f32[half, rows, :].astype(BF16)
            wdn_bf[rows, :] = wdn_f32[half, rows, :].astype(BF16)
            return carry

        lax.fori_loop(0, D_MODEL // LANES, cast_rows, 0)

    @pl.when(active)
    def _():
        x = jnp.concatenate(
            [xr_ref[pl.ds(c, r, stride=ROW_SLABS), :] for c in range(ROW_SLABS)], axis=1)
        gu = jnp.dot(x.astype(BF16), wgu_bf[...], preferred_element_type=F32) + bgu_ref[0]
        g = jnp.minimum(gu[:, :D_EXPERT], SWIGLU_LIMIT)
        u = jnp.clip(gu[:, D_EXPERT:], -SWIGLU_LIMIT, SWIGLU_LIMIT)
        act = (u + 1.0) * (g * jax.nn.sigmoid(SWIGLU_ALPHA * g))
        y = jnp.dot(act.astype(BF16), wdn_bf[...], preferred_element_type=F32) + bdn_ref[0]
        for c in range(ROW_SLABS):
            yr_ref[pl.ds(c, r, stride=ROW_SLABS), :] = y[:, c * LANES:(c + 1) * LANES]

    @pl.when(jnp.logical_not(active))
    def _():
        yr_ref[...] = jnp.zeros_like(yr_ref)


def _final_kernel(pstart_ref, meta_ref, meta_next_ref, x1_ref, route_ref, p_ref, gple_ref,
                  wpg_ref, wpp_ref, yr_hbm, o_ref, gstage, sems):
    t = TOKEN_TILE
    i = pl.program_id(0)
    half = lax.rem(i, 2)

    def run_copy(buf_half, slot, row, n_rows=RUN_CHUNK):
        dst = gstage.at[_slab_rows(buf_half * RUN_SLOTS + slot, n_rows), :]
        return pltpu.make_async_copy(yr_hbm.at[_slab_rows(row, n_rows), :], dst, sems.at[buf_half])

    @pl.when(i == 0)
    def _():
        gstage[...] = jnp.zeros_like(gstage)
        _for_each_run_copy(meta_ref, pstart_ref, lambda s, r, n: run_copy(0, s, r, n).start())

    @pl.when(i + 1 < pl.num_programs(0))
    def _():
        _for_each_run_copy(meta_next_ref, pstart_ref,
                           lambda s, r, n: run_copy(1 - half, s, r, n).start())

    _wait_run_chunks(meta_ref, lambda n_rows: run_copy(half, 0, 0, n_rows))

    base = half * (RUN_SLOTS * ROW_SLABS)
    rows = jnp.concatenate(
        [gstage[pl.ds(base + c, RUN_SLOTS, stride=ROW_SLABS), :] for c in range(ROW_SLABS)], axis=1)
    route = route_ref[...]
    s_i = lax.broadcasted_iota(jnp.int32, (t, RUN_SLOTS), 1).astype(F32)
    wmat = jnp.zeros((t, RUN_SLOTS), F32)
    for k in range(TOP_K):
        wmat = jnp.where(s_i == route[:, TOP_K + k:TOP_K + k + 1], route[:, k:k + 1], wmat)
    rows_hi = rows.astype(BF16)
    rows_lo = (rows - rows_hi.astype(F32)).astype(BF16)
    w_hi = wmat.astype(BF16)
    w_lo = (wmat - w_hi.astype(F32)).astype(BF16)
    y = (jnp.dot(w_hi, rows_hi, preferred_element_type=F32)
         + jnp.dot(w_lo, rows_hi, preferred_element_type=F32)
         + jnp.dot(w_hi, rows_lo, preferred_element_type=F32))
    x2 = x1_ref[...] + y
    hp = _rms(x2, gple_ref[...]).astype(BF16)
    gate = jax.nn.sigmoid(jnp.dot(hp, wpg_ref[...], preferred_element_type=F32))
    emb = jnp.dot(p_ref[...].astype(BF16), wpp_ref[...], preferred_element_type=F32)
    o_ref[...] = x2 + gate * emb


def _rope_tables(s, half, x1_starts, x2_starts):
    inv_freq = ROPE_THETA ** (-(np.arange(half, dtype=np.float64) / half))
    ang = np.arange(s, dtype=np.float64)[:, None] * inv_freq[None, :]
    cos, sin = np.cos(ang), np.sin(ang)
    cos_t, sin_t = np.ones((s, LANES), np.float32), np.zeros((s, LANES), np.float32)
    for st in x1_starts:
        cos_t[:, st:st + half] = cos
        sin_t[:, st:st + half] = -sin
    for st in x2_starts:
        cos_t[:, st:st + half] = cos
        sin_t[:, st:st + half] = sin
    return jnp.asarray(cos_t), jnp.asarray(sin_t)


_MOBA_LANE_COLS = tuple(list(range(0, 32)) + list(range(64, 96)) + list(range(32, 64))
                        + list(range(96, 128)))
_MLA_LANE_DIMS = tuple(list(range(80, 96)) + list(range(0, 48)) + list(range(64, 80))
                       + list(range(48, 64)) + [MLA_QK_DIM] * 32)


def _moba_lanes(w):
    k, width = w.shape
    cols = jnp.asarray(_MOBA_LANE_COLS, jnp.int32)
    return w.reshape(k, width // LANES, LANES)[:, :, cols].reshape(k, width)


def _mla_lanes(w, heads):
    k = w.shape[0]
    w = jnp.pad(w.reshape(k, heads, MLA_QK_DIM), ((0, 0), (0, 0), (0, 1)))
    return w[:, :, jnp.asarray(_MLA_LANE_DIMS, jnp.int32)].reshape(k, heads * LANES)


def _row(v):
    return v.reshape(1, -1).astype(F32)


def _layer(x, p_i, g_mix, w_in, moba_q_norm, moba_k_norm, mla_q_lat_norm, w_uq, mla_kv_lat_norm,
           w_ukv, mla_q_norm, mla_k_norm, w_branch_a, w_branch_b, w_out, g_ffn, w_router, b_router,
           w_gate_up, b_gate_up, w_down, b_down, g_ple, w_ple_gate, w_ple_proj):
    b, s, d = x.shape
    n = b * s
    assert d == D_MODEL and s % ATTN_TILE == 0
    assert s // MOBA_BLOCK <= GATE_ROWS and TOKEN_TILE == ATTN_TILE == MOBA_BLOCK
    assert D_EXPERT == D_MODEL and RUN_CHUNK == SUBLANES
    n_tiles = n // TOKEN_TILE
    tiles_per_seq = s // TOKEN_TILE
    xf = x.reshape(n, d)

    off = [0]
    for wdt in (MOBA_WIDTH, MOBA_WIDTH, MOBA_WIDTH, MLA_Q_LORA, MLA_KV_LORA, MLA_ROPE_DIM, D_MODEL, D_MODEL):
        off.append(off[-1] + wdt)
    seg = [w_in[:, off[i]:off[i + 1]] for i in range(8)]
    kpe_cols = _mla_lanes(jnp.pad(seg[5], ((0, 0), (MLA_NOPE_DIM, 0))), 1)
    w_in_p = jnp.concatenate([_moba_lanes(seg[0]), _moba_lanes(seg[1])] + seg[3:5] + [kpe_cols]
                             + seg[6:], axis=1).astype(BF16)
    assert w_in_p.shape[1] == D_IN_PACKED
    w_va_t = seg[2].T.astype(BF16)
    w_uq_p = _mla_lanes(w_uq, MLA_HEADS).astype(BF16)
    w_ukv_h = w_ukv.reshape(MLA_KV_LORA, MLA_HEADS, MLA_NOPE_DIM + MLA_V_DIM)
    w_uk_p = _mla_lanes(jnp.pad(w_ukv_h[:, :, :MLA_NOPE_DIM], ((0, 0), (0, 0), (0, MLA_ROPE_DIM)))
                        .reshape(MLA_KV_LORA, -1), MLA_HEADS).astype(BF16)
    w_uv_t = w_ukv_h[:, :, MLA_NOPE_DIM:].reshape(MLA_KV_LORA, MLA_WIDTH).T.astype(BF16)
    gqa = _moba_lanes(_row(jnp.tile(moba_q_norm, 2))) * (MOBA_HEAD_DIM ** -0.5 * LOG2_E)
    gka = _moba_lanes(_row(jnp.tile(moba_k_norm, 2)))
    gqb = _mla_lanes(_row(mla_q_norm), 1) * (MLA_QK_DIM ** -0.5 * LOG2_E)
    gkb = _mla_lanes(_row(mla_k_norm), 1)
    half_a, half_b = MOBA_HEAD_DIM // 2, MLA_ROPE_DIM // 2
    cosa, sina = _rope_tables(s, half_a, (0, half_a), (LANES // 2, LANES // 2 + half_a))
    cosb, sinb = _rope_tables(s, half_b, (LANES // 2,), (0,))

    tok = lambda width: pl.BlockSpec((TOKEN_TILE, width), lambda i: (i, 0))
    whole = lambda arr: pl.BlockSpec(arr.shape, lambda i: (0,) * arr.ndim)
    seq_tab = pl.BlockSpec((TOKEN_TILE, LANES), lambda i: (i % tiles_per_seq, 0))
    vt_spec = pl.BlockSpec((1, MOBA_WIDTH, TOKEN_TILE), lambda i: (i, 0, 0))
    k_tiles_spec = lambda tiles: pl.BlockSpec((tiles, TOKEN_TILE, LANES), lambda i: (0, i, 0))
    params = pltpu.CompilerParams(dimension_semantics=("arbitrary",), vmem_limit_bytes=VMEM_LIMIT)

    consts1 = [_row(g_mix), w_in_p, w_va_t, gqa, gka]
    consts2 = [_row(mla_q_lat_norm), w_uq_p, _row(mla_kv_lat_norm), w_uk_p, w_uv_t, gqb, gkb]
    qa, ka, va, kmean, qb, kb, vb, ga, gb = pl.pallas_call(
        _inproj_kernel,
        grid=(n_tiles,),
        in_specs=([tok(d)] + [whole(a) for a in consts1] + [seq_tab] * 2
                  + [whole(a) for a in consts2] + [seq_tab] * 2),
        out_specs=[tok(MOBA_WIDTH), k_tiles_spec(MOBA_WIDTH // LANES), vt_spec,
                   pl.BlockSpec((1, 1, MOBA_WIDTH), lambda i: (i, 0, 0)),
                   tok(MLA_HEADS * LANES), k_tiles_spec(MLA_HEADS), vt_spec,
                   tok(d), tok(d)],
        out_shape=[jax.ShapeDtypeStruct((n, MOBA_WIDTH), BF16),
                   jax.ShapeDtypeStruct((MOBA_WIDTH // LANES, n, LANES), BF16),
                   jax.ShapeDtypeStruct((n_tiles, MOBA_WIDTH, TOKEN_TILE), BF16),
                   jax.ShapeDtypeStruct((n_tiles, 1, MOBA_WIDTH), F32),
                   jax.ShapeDtypeStruct((n, MLA_HEADS * LANES), BF16),
                   jax.ShapeDtypeStruct((MLA_HEADS, n, LANES), BF16),
                   jax.ShapeDtypeStruct((n_tiles, MLA_WIDTH, TOKEN_TILE), BF16)]
        + [jax.ShapeDtypeStruct((n, d), BF16)] * 2,
        compiler_params=params,
        name="in_projection",
    )(xf, *consts1, cosa, sina, *consts2, cosb, sinb)

    kmean = kmean.reshape(b, tiles_per_seq, MOBA_WIDTH)
    kmean = jnp.pad(kmean, ((0, 0), (0, GATE_ROWS - tiles_per_seq), (0, 0))).astype(BF16)

    r3 = lambda a: a.reshape(b, s, a.shape[-1])
    r4 = lambda a: a.reshape(b, tiles_per_seq, a.shape[1], TOKEN_TILE)
    ya = _attention(r3(qa), ka, r4(va), kmean, _logits_bounded(gqa, gka, MOBA_HEAD_DIM),
                    moba=True).reshape(n, MOBA_WIDTH)
    yb = _attention(r3(qb), kb, r4(vb), None, _logits_bounded(gqb, gkb, MLA_QK_DIM),
                    moba=False).reshape(n, MLA_WIDTH)

    consts3 = [w_branch_a.astype(BF16), w_branch_b.astype(BF16), w_out.astype(BF16), _row(g_ffn),
               w_router.T.astype(BF16), b_router.reshape(N_EXPERTS, 1).astype(F32)]
    x1, h_ffn, route_t, meta, cnt = pl.pallas_call(
        _merge_kernel,
        grid=(n_tiles,),
        in_specs=[tok(d), tok(MOBA_WIDTH), tok(MLA_WIDTH), tok(d), tok(d)]
        + [whole(a) for a in consts3],
        out_specs=[tok(d), tok(d),
                   pl.BlockSpec((1, 2 * TOP_K, TOKEN_TILE), lambda i: (i, 0, 0)),
                   pl.BlockSpec((1, N_EXPERTS, LANES), lambda i: (i, 0, 0)),
                   pl.BlockSpec((N_EXPERTS, LANES), lambda i: (0, 0))],
        out_shape=[jax.ShapeDtypeStruct((n, d), F32),
                   jax.ShapeDtypeStruct((n, d), BF16),
                   jax.ShapeDtypeStruct((n_tiles, 2 * TOP_K, TOKEN_TILE), F32),
                   jax.ShapeDtypeStruct((n_tiles, N_EXPERTS, LANES), F32),
                   jax.ShapeDtypeStruct((N_EXPERTS, LANES), F32)],
        scratch_shapes=[pltpu.VMEM((N_EXPERTS, LANES), F32)],
        compiler_params=params,
        name="merge_router",
    )(xf, ya, yb, ga, gb, *consts3)
    route = jnp.pad(route_t.transpose(0, 2, 1).reshape(n, 2 * TOP_K),
                    ((0, 0), (0, LANES - 2 * TOP_K)))

    rb = EXPERT_ROWS
    n_blocks = -(-(n * TOP_K + N_EXPERTS * (RUN_CHUNK - 1)) // rb) + N_EXPERTS
    n_rows = n_blocks * rb
    counts = cnt[:, 0].astype(jnp.int32)
    pcounts = ((counts + (RUN_CHUNK - 1) + rb - 1) // rb) * rb
    pends = jnp.cumsum(pcounts).astype(jnp.int32)
    pstarts = (pends - pcounts).astype(jnp.int32)
    nact = (pends[-1] // rb).astype(jnp.int32).reshape(1)
    blk = jnp.minimum(jnp.arange(n_blocks, dtype=jnp.int32), nact[0] - 1)
    blk_e = jnp.sum((pends[None, :] <= (blk * rb)[:, None]).astype(jnp.int32), axis=1)
    blk_e = jnp.minimum(blk_e, N_EXPERTS - 1)
    meta_i = jnp.pad(meta[:, :, :4].transpose(0, 2, 1), ((0, 0), (0, 0), (0, LANES - N_EXPERTS)))
    meta_i = meta_i.astype(jnp.int32).reshape(n_tiles * 4 * LANES)
    meta_spec = lambda shift: pl.BlockSpec(
        (4 * LANES,), lambda i, *_: (jnp.clip(i + shift, 0, n_tiles - 1),), memory_space=pltpu.SMEM)

    xr = pl.pallas_call(
        _dispatch_kernel,
        grid_spec=pltpu.PrefetchScalarGridSpec(
            num_scalar_prefetch=4,
            grid=(n_tiles,),
            in_specs=[meta_spec(0), meta_spec(-1),
                      pl.BlockSpec((1, 2 * TOP_K, TOKEN_TILE), lambda i, *_: (i, 0, 0)),
                      pl.BlockSpec((TOKEN_TILE, d), lambda i, *_: (i, 0))],
            out_specs=pl.BlockSpec(memory_space=pl.ANY),
            scratch_shapes=[pltpu.VMEM((2 * RUN_SLOTS * ROW_SLABS, LANES), F32),
                            pltpu.VMEM((ZERO_ROWS * ROW_SLABS, LANES), F32),
                            pltpu.SemaphoreType.DMA((2,)), pltpu.SemaphoreType.DMA(())]),
        out_shape=jax.ShapeDtypeStruct((n_rows * ROW_SLABS, LANES), F32),
        compiler_params=params,
        name="dispatch_rows",
    )(pstarts, pends, counts, nact, meta_i, meta_i, route_t, h_ffn)

    act_blk = lambda i, be, na: jnp.minimum(i, na[0] - 1)
    yr = pl.pallas_call(
        _expert_kernel,
        grid_spec=pltpu.PrefetchScalarGridSpec(
            num_scalar_prefetch=2,
            grid=(n_blocks,),
            in_specs=[
                pl.BlockSpec((rb * ROW_SLABS, LANES), lambda i, be, na: (act_blk(i, be, na), 0)),
                pl.BlockSpec(memory_space=pl.ANY),
                pl.BlockSpec((1, 1, 2 * D_EXPERT), lambda i, be, na: (be[i], 0, 0)),
                pl.BlockSpec(memory_space=pl.ANY),
                pl.BlockSpec((1, 1, d), lambda i, be, na: (be[i], 0, 0)),
            ],
            out_specs=pl.BlockSpec((rb * ROW_SLABS, LANES), lambda i, be, na: (i, 0)),
            scratch_shapes=[pltpu.VMEM((2, d, 2 * D_EXPERT), F32), pltpu.VMEM((2, D_EXPERT, d), F32),
                            pltpu.VMEM((d, 2 * D_EXPERT), BF16), pltpu.VMEM((D_EXPERT, d), BF16),
                            pltpu.SemaphoreType.DMA((2, 2))]),
        out_shape=jax.ShapeDtypeStruct((n_rows * ROW_SLABS, LANES), F32),
        compiler_params=params,
        name="experts",
    )(blk_e, nact, xr, w_gate_up.astype(F32),
      b_gate_up.reshape(N_EXPERTS, 1, -1).astype(F32), w_down.astype(F32),
      b_down.reshape(N_EXPERTS, 1, -1).astype(F32))

    ftok = lambda width: pl.BlockSpec((TOKEN_TILE, width), lambda i, ps: (i, 0))
    fwhole = lambda arr: pl.BlockSpec(arr.shape, lambda i, ps: (0,) * arr.ndim)
    consts4 = [_row(g_ple), w_ple_gate.astype(BF16), w_ple_proj.astype(BF16)]
    out = pl.pallas_call(
        _final_kernel,
        grid_spec=pltpu.PrefetchScalarGridSpec(
            num_scalar_prefetch=1,
            grid=(n_tiles,),
            in_specs=[meta_spec(0), meta_spec(1), ftok(d), ftok(LANES), ftok(PLE_DIM)]
            + [fwhole(a) for a in consts4] + [pl.BlockSpec(memory_space=pl.ANY)],
            out_specs=ftok(d),
            scratch_shapes=[pltpu.VMEM((2 * RUN_SLOTS * ROW_SLABS, LANES), F32),
                            pltpu.SemaphoreType.DMA((2,))]),
        out_shape=jax.ShapeDtypeStruct((n, d), F32),
        compiler_params=params,
        name="combine_ple",
    )(pstarts, meta_i, meta_i, x1, route, p_i.reshape(n, PLE_DIM), *consts4, yr)
    return out.reshape(b, s, d)


def kernel(x, p, g_mix, w_in, moba_q_norm, moba_k_norm, mla_q_lat_norm, w_uq, mla_kv_lat_norm, w_ukv, mla_q_norm, mla_k_norm, w_branch_a, w_branch_b, w_out, g_ffn, w_router, b_router, w_gate_up, b_gate_up, w_down, b_down, g_ple, w_ple_gate, w_ple_proj):
    for i in range(p.shape[0]):
        x = _layer(x, p[i], g_mix[i], w_in[i], moba_q_norm[i], moba_k_norm[i], mla_q_lat_norm[i],
                   w_uq[i], mla_kv_lat_norm[i], w_ukv[i], mla_q_norm[i], mla_k_norm[i],
                   w_branch_a[i], w_branch_b[i], w_out[i], g_ffn[i], w_router[i], b_router[i],
                   w_gate_up[i], b_gate_up[i], w_down[i], b_down[i], g_ple[i], w_ple_gate[i],
                   w_ple_proj[i])
    return x
```

```python
import functools

import numpy as np
import jax
import jax.numpy as jnp
from jax import lax
from jax.experimental import pallas as pl
from jax.experimental.pallas import tpu as pltpu

F32 = jnp.float32
BF16 = jnp.bfloat16

D_MODEL = 1024
PLE_DIM = 256
EPS = 1e-6
ROPE_THETA = 10000.0
MOBA_HEADS = 8
MOBA_HEAD_DIM = 64
MOBA_BLOCK = 256
MOBA_TOPK = 3
MOBA_WIDTH = MOBA_HEADS * MOBA_HEAD_DIM
MLA_HEADS = 8
MLA_Q_LORA = 256
MLA_KV_LORA = 128
MLA_NOPE_DIM = 64
MLA_ROPE_DIM = 32
MLA_V_DIM = 64
MLA_QK_DIM = MLA_NOPE_DIM + MLA_ROPE_DIM
MLA_WIDTH = MLA_HEADS * MLA_V_DIM
N_EXPERTS = 32
TOP_K = 4
D_EXPERT = 1024
SWIGLU_LIMIT = 7.0
SWIGLU_ALPHA = 1.702

LANES = 128
SUBLANES = 8
ROW_SLABS = D_MODEL // LANES
VMEM_LIMIT = 56 * 1024 * 1024

TOKEN_TILE = 256
ATTN_TILE = 256
ATTN_GROUPS = 4
EXPERT_ROWS = 512
RUN_CHUNK = 8
RUN_SLOTS = -(-(TOKEN_TILE * TOP_K + N_EXPERTS * (RUN_CHUNK - 1)) // 256) * 256

LOG2_E = 1.4426950408889634
LOGIT_LIMIT = 60.0
NEG = -1e30
MASK_BIAS = -1e9

C_QA, C_KA = 0, 512
C_CQ, C_CKV, C_KPE = 1024, 1280, 1408
C_GA, C_GB = 1536, 2560
D_IN_PACKED = 3584


def _rms(x, gain):
    return x * lax.rsqrt(jnp.mean(x * x, axis=-1, keepdims=True) + EPS) * gain


def _rope(t, cos, sin):
    return t * cos + pltpu.roll(t, LANES // 2, 1) * sin


def _moba_even_head(lane):
    return (lane & (MOBA_HEAD_DIM // 2)) == 0


def _inproj_kernel(x_ref, gmix_ref, win_ref, wvat_ref, gqa_ref, gka_ref, cosa_ref, sina_ref,
                   gql_ref, wuq_ref, gkvl_ref, wuk_ref, wuvt_ref, gqb_ref, gkb_ref,
                   cosb_ref, sinb_ref,
                   qa_ref, ka_ref, vat_ref, kmean_ref, qb_ref, kb_ref, vbt_ref, ga_ref, gb_ref):
    hn = _rms(x_ref[...], gmix_ref[...]).astype(BF16)

    def proj(c0, width):
        return jnp.dot(hn, win_ref[:, c0:c0 + width], preferred_element_type=F32)

    first = _moba_even_head(lax.broadcasted_iota(jnp.int32, (TOKEN_TILE, LANES), 1))
    cosa, sina = cosa_ref[...], sina_ref[...]

    def moba_norm_rope(t, gain):
        sq = t * t
        ss0 = jnp.sum(jnp.where(first, sq, 0.0), axis=-1, keepdims=True)
        ss1 = jnp.sum(jnp.where(first, 0.0, sq), axis=-1, keepdims=True)
        ms = jnp.where(first, ss0, ss1) * (1.0 / MOBA_HEAD_DIM)
        t = t * lax.rsqrt(ms + EPS) * gain
        return _rope(t, cosa, sina)

    qa = proj(C_QA, MOBA_WIDTH)
    ka = proj(C_KA, MOBA_WIDTH)
    for c in range(MOBA_WIDTH // LANES):
        sl = slice(c * LANES, (c + 1) * LANES)
        qa_ref[:, sl] = moba_norm_rope(qa[:, sl], gqa_ref[...]).astype(BF16)
        kc = moba_norm_rope(ka[:, sl], gka_ref[...])
        ka_ref[c] = kc.astype(BF16)
        kmean_ref[0, :, sl] = jnp.mean(kc, axis=0, keepdims=True)
    vat_ref[0] = lax.dot_general(wvat_ref[...], hn, _NT, preferred_element_type=F32).astype(BF16)

    cosb, sinb = cosb_ref[...], sinb_ref[...]

    def mla_norm_rope(t, gain):
        ms = jnp.sum(t * t, axis=-1, keepdims=True) * (1.0 / MLA_QK_DIM)
        t = t * lax.rsqrt(ms + EPS) * gain
        return _rope(t, cosb, sinb)

    cq = _rms(proj(C_CQ, MLA_Q_LORA), gql_ref[...]).astype(BF16)
    qb = jnp.dot(cq, wuq_ref[...], preferred_element_type=F32)
    ckv = _rms(proj(C_CKV, MLA_KV_LORA), gkvl_ref[...]).astype(BF16)
    kn = jnp.dot(ckv, wuk_ref[...], preferred_element_type=F32)
    kpe = proj(C_KPE, LANES)
    for h in range(MLA_HEADS):
        sl = slice(h * LANES, (h + 1) * LANES)
        qb_ref[:, sl] = mla_norm_rope(qb[:, sl], gqb_ref[...]).astype(BF16)
        kb_ref[h] = mla_norm_rope(kn[:, sl] + kpe, gkb_ref[...]).astype(BF16)
    vbt_ref[0] = lax.dot_general(wuvt_ref[...], ckv, _NT, preferred_element_type=F32).astype(BF16)

    ga_ref[...] = jax.nn.sigmoid(proj(C_GA, D_MODEL)).astype(BF16)
    gb_ref[...] = jax.nn.sigmoid(proj(C_GB, D_MODEL)).astype(BF16)


_NT = (((1,), (1,)), ((), ()))


GATE_ROWS = 16


def _attn_kernel(*refs, moba):
    if moba:
        bounded_ref, q_ref, k_ref, vt_ref, kmean_ref, o_ref = refs
    else:
        bounded_ref, q_ref, k_ref, vt_ref, o_ref = refs
    t = ATTN_TILE
    hd = MOBA_HEAD_DIM
    n_heads = 2 * ATTN_GROUPS
    qi = pl.program_id(2)
    key_i = lax.broadcasted_iota(jnp.int32, (t, t), 0)
    qry_i = lax.broadcasted_iota(jnp.int32, (t, t), 1)

    blk = lax.broadcasted_iota(jnp.int32, (GATE_ROWS, t), 0)
    heads, biases = [], []
    for hh in range(n_heads):
        if moba:
            lane = lax.broadcasted_iota(jnp.int32, (t, LANES), 1)
            even = _moba_even_head(lane)
            head_lanes = even if hh % 2 == 0 else jnp.logical_not(even)
            k_tile = hh // 2
            kcols = slice(k_tile * LANES, (k_tile + 1) * LANES)
            q = jnp.where(head_lanes, q_ref[0, :, kcols], jnp.zeros((), BF16))
            gate = lax.dot_general(kmean_ref[0, :, kcols], q, _NT, preferred_element_type=F32)
            g = jnp.where(blk < qi, gate, -jnp.inf)
            keep = jnp.zeros((GATE_ROWS, t), F32)
            for _ in range(MOBA_TOPK):
                gmax = jnp.max(g, axis=0, keepdims=True)
                pick = jnp.min(jnp.where(g == gmax, blk, GATE_ROWS), axis=0, keepdims=True)
                hit = blk == jnp.where(gmax > -jnp.inf, pick, GATE_ROWS)
                keep = jnp.where(hit, 1.0, keep)
                g = jnp.where(hit, -jnp.inf, g)
            biases.append(jnp.where(keep > 0.0, 0.0, MASK_BIAS))
        else:
            k_tile = hh
            q = q_ref[0, :, hh * LANES:(hh + 1) * LANES]
        heads.append((q, k_tile))

    def softmax_attend(shift_free):
        def update(s, vt_blk, state):
            m_prev, l_prev, acc = state
            if shift_free:
                p = jnp.exp2(s)
                l_new = l_prev + jnp.sum(p, axis=0, keepdims=True)
                acc = acc + jnp.dot(vt_blk, p.astype(BF16), preferred_element_type=F32)
                return m_prev, l_new, acc
            m_new = jnp.maximum(m_prev, jnp.max(s, axis=0, keepdims=True))
            alpha = jnp.exp2(m_prev - m_new)
            p = jnp.exp2(s - m_new)
            l_new = alpha * l_prev + jnp.sum(p, axis=0, keepdims=True)
            acc = alpha * acc + jnp.dot(vt_blk, p.astype(BF16), preferred_element_type=F32)
            return m_new, l_new, acc

        def past_block(j, states):
            start = pl.multiple_of(j * t, t)
            scores = [lax.dot_general(k_ref[k_tile, pl.ds(start, t), :], q, _NT,
                                      preferred_element_type=F32) for q, k_tile in heads]
            out = []
            for hh, s in enumerate(scores):
                if moba:
                    s = jnp.sum(jnp.where(blk == j, biases[hh], 0.0), axis=0, keepdims=True) + s
                out.append(update(s, vt_ref[0, j, hh * hd:(hh + 1) * hd, :], states[hh]))
            return tuple(out)

        init = (jnp.full((1, t), NEG, F32), jnp.zeros((1, t), F32), jnp.zeros((hd, t), F32))
        states = lax.fori_loop(0, qi, past_block, (init,) * n_heads)

        diag_start = pl.multiple_of(qi * t, t)
        scores = [lax.dot_general(k_ref[k_tile, pl.ds(diag_start, t), :], q, _NT,
                                  preferred_element_type=F32) for q, k_tile in heads]
        outs = []
        for hh, s in enumerate(scores):
            s = jnp.where(key_i <= qry_i, s, NEG)
            _, l_fin, acc = update(s, vt_ref[0, qi, hh * hd:(hh + 1) * hd, :], states[hh])
            outs.append(acc / l_fin)
        o_ref[0] = jnp.concatenate(outs, axis=0).T.astype(BF16)

    bounded = bounded_ref[0] == 1

    @pl.when(bounded)
    def _():
        softmax_attend(True)

    @pl.when(jnp.logical_not(bounded))
    def _():
        softmax_attend(False)


def _logits_bounded(q_gain, k_gain, dims):
    bound = 1.02 * dims * jnp.max(jnp.abs(q_gain)) * jnp.max(jnp.abs(k_gain))
    return jnp.where(bound <= LOGIT_LIMIT, 1, 0).astype(jnp.int32).reshape(1)


def _attention(q, k, vt, kmean, bounded, *, moba):
    b, s, _ = q.shape
    v_cols = ATTN_GROUPS * LANES
    steps = vt.shape[2] // v_cols
    nblk = s // ATTN_TILE
    qk_cols = v_cols if moba else 2 * v_cols
    k_tiles = qk_cols // LANES
    in_specs = [
        pl.BlockSpec((1, ATTN_TILE, qk_cols), lambda bi, gi, qi, fl: (bi, qi, gi)),
        pl.BlockSpec((k_tiles, s, LANES), lambda bi, gi, qi, fl: (gi, bi, 0)),
        pl.BlockSpec((1, nblk, v_cols, ATTN_TILE), lambda bi, gi, qi, fl: (bi, 0, gi, 0)),
    ]
    args = [q, k, vt]
    if moba:
        in_specs.append(pl.BlockSpec((1, GATE_ROWS, v_cols), lambda bi, gi, qi, fl: (bi, 0, gi)))
        args.append(kmean)
    return pl.pallas_call(
        functools.partial(_attn_kernel, moba=moba),
        grid_spec=pltpu.PrefetchScalarGridSpec(
            num_scalar_prefetch=1,
            grid=(b, steps, nblk),
            in_specs=in_specs,
            out_specs=pl.BlockSpec((1, ATTN_TILE, v_cols), lambda bi, gi, qi, fl: (bi, qi, gi))),
        out_shape=jax.ShapeDtypeStruct((b, s, steps * v_cols), BF16),
        compiler_params=pltpu.CompilerParams(
            dimension_semantics=("arbitrary", "arbitrary", "arbitrary"),
            vmem_limit_bytes=VMEM_LIMIT),
        name="moba_attention" if moba else "mla_attention",
    )(bounded, *args)


def _merge_kernel(x_ref, ya_ref, yb_ref, ga_ref, gb_ref, wa_ref, wb_ref, wo_ref, gffn_ref,
                  wr_ref, br_ref, x1_ref, h_ref, route_ref, meta_ref, cnt_ref, run_scr):
    t = TOKEN_TILE

    @pl.when(pl.program_id(0) == 0)
    def _():
        run_scr[...] = jnp.zeros_like(run_scr)

    merged = (ga_ref[...].astype(F32) * jnp.dot(ya_ref[...], wa_ref[...], preferred_element_type=F32)
              + gb_ref[...].astype(F32) * jnp.dot(yb_ref[...], wb_ref[...], preferred_element_type=F32))
    x1 = x_ref[...] + jnp.dot(merged.astype(BF16), wo_ref[...], preferred_element_type=F32)
    x1_ref[...] = x1
    h = _rms(x1, gffn_ref[...]).astype(BF16)
    h_ref[...] = h

    logits = lax.dot_general(wr_ref[...], h, _NT, preferred_element_type=F32) + br_ref[...]
    e_i = lax.broadcasted_iota(jnp.int32, (N_EXPERTS, t), 0)
    lg = logits
    hits = []
    top = None
    for r in range(TOP_K):
        gmax = jnp.max(lg, axis=0, keepdims=True)
        pick = jnp.min(jnp.where(lg == gmax, e_i, N_EXPERTS), axis=0, keepdims=True)
        hit = e_i == pick
        if r == 0:
            top = gmax
        hits.append(hit)
        lg = jnp.where(hit, -jnp.inf, lg)
    sel = jnp.where(lg == -jnp.inf, 1.0, 0.0)
    wgt = sel * jnp.exp(logits - top)
    wgt = wgt / jnp.sum(wgt, axis=0, keepdims=True)

    r_i = lax.broadcasted_iota(jnp.int32, (t, t), 0)
    c_i = lax.broadcasted_iota(jnp.int32, (t, t), 1)
    earlier = jnp.where(r_i < c_i, 1.0, 0.0).astype(BF16)
    rank_in_tile = jnp.dot(sel.astype(BF16), earlier, preferred_element_type=F32)
    tcnt = jnp.sum(sel, axis=1, keepdims=True)
    tpad = jnp.floor((tcnt + (RUN_CHUNK - 1)) * (1.0 / RUN_CHUNK)) * RUN_CHUNK
    e_r = lax.broadcasted_iota(jnp.int32, (N_EXPERTS, N_EXPERTS), 0)
    e_c = lax.broadcasted_iota(jnp.int32, (N_EXPERTS, N_EXPERTS), 1)
    before = jnp.where(e_c < e_r, 1.0, 0.0).astype(BF16)
    tbase = jnp.dot(before, jnp.broadcast_to(tpad, (N_EXPERTS, LANES)).astype(BF16),
                    preferred_element_type=F32)
    n_chunks = jnp.sum(tpad, axis=0, keepdims=True) * (1.0 / RUN_CHUNK)
    run = run_scr[...]
    run_new = run + tcnt
    run_scr[...] = run_new
    cnt_ref[...] = run_new
    lane = lax.broadcasted_iota(jnp.int32, (N_EXPERTS, LANES), 1)
    meta_ref[0] = jnp.where(lane == 0, tcnt, jnp.where(lane == 1, run, jnp.where(
        lane == 2, tbase, jnp.where(lane == 3, n_chunks, 0.0))))

    slot_of = rank_in_tile + tbase[:, 0:1]
    row = lax.broadcasted_iota(jnp.int32, (2 * TOP_K, t), 0)
    route = jnp.zeros((2 * TOP_K, t), F32)
    for r in range(TOP_K):
        w_r = jnp.sum(jnp.where(hits[r], wgt, 0.0), axis=0, keepdims=True)
        slot_r = jnp.sum(jnp.where(hits[r], slot_of, 0.0), axis=0, keepdims=True)
        route = jnp.where(row == r, w_r, route)
        route = jnp.where(row == TOP_K + r, slot_r, route)
    route_ref[0] = route


BIG_COPY = 4


def _for_each_run_copy(meta_ref, pstart_ref, fn):
    big_rows = BIG_COPY * RUN_CHUNK

    def per_expert(e, carry):
        n_chunks = lax.shift_right_logical(meta_ref[e] + (RUN_CHUNK - 1), RUN_CHUNK.bit_length() - 1)
        n_big = lax.shift_right_logical(n_chunks, BIG_COPY.bit_length() - 1)
        slot0 = meta_ref[2 * LANES + e]
        row0 = pstart_ref[e] + meta_ref[LANES + e]

        def big(b, cc):
            fn(slot0 + b * big_rows, row0 + b * big_rows, big_rows)
            return cc

        def small(c, cc):
            fn(slot0 + c * RUN_CHUNK, row0 + c * RUN_CHUNK, RUN_CHUNK)
            return cc

        lax.fori_loop(0, n_big, big, 0)
        lax.fori_loop(n_big * BIG_COPY, n_chunks, small, 0)
        return carry

    lax.fori_loop(0, N_EXPERTS, per_expert, 0)


WAIT_BATCH = 16


def _wait_run_chunks(meta_ref, make_copy):
    n_chunks = meta_ref[3 * LANES]
    n_batches = lax.shift_right_logical(n_chunks, WAIT_BATCH.bit_length() - 1)
    lax.fori_loop(0, n_batches, lambda b, c: (make_copy(WAIT_BATCH * RUN_CHUNK).wait(), c)[1], 0)
    lax.fori_loop(0, n_chunks - n_batches * WAIT_BATCH,
                  lambda b, c: (make_copy(RUN_CHUNK).wait(), c)[1], 0)


def _slab_rows(first_row, n_rows):
    return pl.ds(pl.multiple_of(first_row * ROW_SLABS, ROW_SLABS), n_rows * ROW_SLABS)


ZERO_ROWS = 256


def _dispatch_kernel(pstart_ref, pend_ref, count_ref, nact_ref, meta_ref, meta_prev_ref, route_ref,
                     h_ref, xr_hbm, stage, zero_scr, sems, zsem):
    t = TOKEN_TILE
    i = pl.program_id(0)
    half = lax.rem(i, 2)

    def run_copy(buf_half, slot, row, n_rows=RUN_CHUNK):
        src = stage.at[_slab_rows(buf_half * RUN_SLOTS + slot, n_rows), :]
        return pltpu.make_async_copy(src, xr_hbm.at[_slab_rows(row, n_rows), :], sems.at[buf_half])

    @pl.when(i == 0)
    def _():
        zero_scr[...] = jnp.zeros_like(zero_scr)
        shift = ZERO_ROWS.bit_length() - 1
        n_units = xr_hbm.shape[0] // (ZERO_ROWS * ROW_SLABS)
        first_unused = nact_ref[0] * (EXPERT_ROWS // ZERO_ROWS)

        def zero_copy(first_row):
            return pltpu.make_async_copy(zero_scr, xr_hbm.at[_slab_rows(first_row, ZERO_ROWS), :], zsem)

        def pad_units(e):
            pad = pend_ref[e] - pstart_ref[e] - count_ref[e]
            return lax.shift_right_logical(pad + (ZERO_ROWS - 1), shift)

        def zero_pad(e, c):
            lax.fori_loop(1, pad_units(e) + 1,
                          lambda b, cc: (zero_copy(pend_ref[e] - b * ZERO_ROWS).start(), cc)[1], 0)
            return c

        def wait_pad(e, c):
            lax.fori_loop(0, pad_units(e), lambda b, cc: (zero_copy(0).wait(), cc)[1], 0)
            return c

        def zero_tail(u, c):
            zero_copy(u * ZERO_ROWS).start()
            return c

        lax.fori_loop(0, N_EXPERTS, zero_pad, 0)
        lax.fori_loop(first_unused, n_units, zero_tail, 0)
        lax.fori_loop(0, N_EXPERTS, wait_pad, 0)
        lax.fori_loop(0, n_units - first_unused, lambda r, c: (zero_copy(0).wait(), c)[1], 0)

    slot_rows = route_ref[0, TOP_K:2 * TOP_K, :]
    s_i = lax.broadcasted_iota(jnp.int32, (RUN_SLOTS, t), 0).astype(F32)
    pick = jnp.zeros((RUN_SLOTS, t), F32)
    for k in range(TOP_K):
        pick = jnp.where(s_i == slot_rows[k:k + 1, :], 1.0, pick)
    rows = jnp.dot(pick.astype(BF16), h_ref[...], preferred_element_type=F32)
    base = half * (RUN_SLOTS * ROW_SLABS)
    for c in range(ROW_SLABS):
        stage[pl.ds(base + c, RUN_SLOTS, stride=ROW_SLABS), :] = rows[:, c * LANES:(c + 1) * LANES]

    @pl.when(i > 0)
    def _():
        _wait_run_chunks(meta_prev_ref, lambda n_rows: run_copy(1 - half, 0, 0, n_rows))

    _for_each_run_copy(meta_ref, pstart_ref, lambda s, r, n: run_copy(half, s, r, n).start())

    @pl.when(i == pl.num_programs(0) - 1)
    def _():
        _wait_run_chunks(meta_ref, lambda n_rows: run_copy(half, 0, 0, n_rows))


def _expert_kernel(blk_e_ref, nact_ref, xr_ref, wgu_hbm, bgu_ref, wdn_hbm, bdn_ref, yr_ref,
                   wgu_f32, wdn_f32, wgu_bf, wdn_bf, sems):
    r = EXPERT_ROWS
    i = pl.program_id(0)
    e = blk_e_ref[i]
    active = i < nact_ref[0]
    new_expert = jnp.logical_or(i == 0, e != blk_e_ref[jnp.maximum(i - 1, 0)])

    def weight_copies(expert):
        half = lax.rem(expert, 2)
        return (pltpu.make_async_copy(wgu_hbm.at[expert], wgu_f32.at[half], sems.at[0, half]),
                pltpu.make_async_copy(wdn_hbm.at[expert], wdn_f32.at[half], sems.at[1, half]))

    @pl.when(i == 0)
    def _():
        for cp in weight_copies(e):
            cp.start()

    @pl.when(jnp.logical_and(active, new_expert))
    def _():
        for cp in weight_copies(e):
            cp.wait()

        @pl.when(e + 1 < N_EXPERTS)
        def _():
            for cp in weight_copies(e + 1):
                cp.start()

        half = lax.rem(e, 2)

        def cast_rows(c, carry):
            rows = pl.ds(pl.multiple_of(c * LANES, LANES), LANES)
            wgu_bf[rows, :] = wgu_f32[half, rows, :].astype(BF16)
            wdn_bf[rows, :] = wdn_f32[half, rows, :].astype(BF16)
            return carry

        lax.fori_loop(0, D_MODEL // LANES, cast_rows, 0)

    @pl.when(active)
    def _():
        x = jnp.concatenate(
            [xr_ref[pl.ds(c, r, stride=ROW_SLABS), :] for c in range(ROW_SLABS)], axis=1)
        gu = jnp.dot(x.astype(BF16), wgu_bf[...], preferred_element_type=F32) + bgu_ref[0]
        g = jnp.minimum(gu[:, :D_EXPERT], SWIGLU_LIMIT)
        u = jnp.clip(gu[:, D_EXPERT:], -SWIGLU_LIMIT, SWIGLU_LIMIT)
        act = (u + 1.0) * (g * jax.nn.sigmoid(SWIGLU_ALPHA * g))
        y = jnp.dot(act.astype(BF16), wdn_bf[...], preferred_element_type=F32) + bdn_ref[0]
        for c in range(ROW_SLABS):
            yr_ref[pl.ds(c, r, stride=ROW_SLABS), :] = y[:, c * LANES:(c + 1) * LANES]

    @pl.when(jnp.logical_not(active))
    def _():
        yr_ref[...] = jnp.zeros_like(yr_ref)


def _final_kernel(pstart_ref, meta_ref, meta_next_ref, x1_ref, route_ref, p_ref, gple_ref,
                  wpg_ref, wpp_ref, yr_hbm, o_ref, gstage, sems):
    t = TOKEN_TILE
    i = pl.program_id(0)
    half = lax.rem(i, 2)

    def run_copy(buf_half, slot, row, n_rows=RUN_CHUNK):
        dst = gstage.at[_slab_rows(buf_half * RUN_SLOTS + slot, n_rows), :]
        return pltpu.make_async_copy(yr_hbm.at[_slab_rows(row, n_rows), :], dst, sems.at[buf_half])

    @pl.when(i == 0)
    def _():
        gstage[...] = jnp.zeros_like(gstage)
        _for_each_run_copy(meta_ref, pstart_ref, lambda s, r, n: run_copy(0, s, r, n).start())

    @pl.when(i + 1 < pl.num_programs(0))
    def _():
        _for_each_run_copy(meta_next_ref, pstart_ref,
                           lambda s, r, n: run_copy(1 - half, s, r, n).start())

    _wait_run_chunks(meta_ref, lambda n_rows: run_copy(half, 0, 0, n_rows))

    base = half * (RUN_SLOTS * ROW_SLABS)
    rows = jnp.concatenate(
        [gstage[pl.ds(base + c, RUN_SLOTS, stride=ROW_SLABS), :] for c in range(ROW_SLABS)], axis=1)
    route = route_ref[...]
    s_i = lax.broadcasted_iota(jnp.int32, (t, RUN_SLOTS), 1).astype(F32)
    wmat = jnp.zeros((t, RUN_SLOTS), F32)
    for k in range(TOP_K):
        wmat = jnp.where(s_i == route[:, TOP_K + k:TOP_K + k + 1], route[:, k:k + 1], wmat)
    rows_bf = rows.astype(BF16)
    w_hi = wmat.astype(BF16)
    w_lo = (wmat - w_hi.astype(F32)).astype(BF16)
    y = (jnp.dot(w_hi, rows_bf, preferred_element_type=F32)
         + jnp.dot(w_lo, rows_bf, preferred_element_type=F32))
    x2 = x1_ref[...] + y
    hp = _rms(x2, gple_ref[...]).astype(BF16)
    gate = jax.nn.sigmoid(jnp.dot(hp, wpg_ref[...], preferred_element_type=F32))
    emb = jnp.dot(p_ref[...].astype(BF16), wpp_ref[...], preferred_element_type=F32)
    o_ref[...] = x2 + gate * emb


def _rope_tables(s, half, x1_starts, x2_starts):
    inv_freq = ROPE_THETA ** (-(np.arange(half, dtype=np.float64) / half))
    ang = np.arange(s, dtype=np.float64)[:, None] * inv_freq[None, :]
    cos, sin = np.cos(ang), np.sin(ang)
    cos_t, sin_t = np.ones((s, LANES), np.float32), np.zeros((s, LANES), np.float32)
    for st in x1_starts:
        cos_t[:, st:st + half] = cos
        sin_t[:, st:st + half] = -sin
    for st in x2_starts:
        cos_t[:, st:st + half] = cos
        sin_t[:, st:st + half] = sin
    return jnp.asarray(cos_t), jnp.asarray(sin_t)


_MOBA_LANE_COLS = tuple(list(range(0, 32)) + list(range(64, 96)) + list(range(32, 64))
                        + list(range(96, 128)))
_MLA_LANE_DIMS = tuple(list(range(80, 96)) + list(range(0, 48)) + list(range(64, 80))
                       + list(range(48, 64)) + [MLA_QK_DIM] * 32)


def _moba_lanes(w):
    k, width = w.shape
    cols = jnp.asarray(_MOBA_LANE_COLS, jnp.int32)
    return w.reshape(k, width // LANES, LANES)[:, :, cols].reshape(k, width)


def _mla_lanes(w, heads):
    k = w.shape[0]
    w = jnp.pad(w.reshape(k, heads, MLA_QK_DIM), ((0, 0), (0, 0), (0, 1)))
    return w[:, :, jnp.asarray(_MLA_LANE_DIMS, jnp.int32)].reshape(k, heads * LANES)


def _row(v):
    return v.reshape(1, -1).astype(F32)


def _layer(x, p_i, g_mix, w_in, moba_q_norm, moba_k_norm, mla_q_lat_norm, w_uq, mla_kv_lat_norm,
           w_ukv, mla_q_norm, mla_k_norm, w_branch_a, w_branch_b, w_out, g_ffn, w_router, b_router,
           w_gate_up, b_gate_up, w_down, b_down, g_ple, w_ple_gate, w_ple_proj):
    b, s, d = x.shape
    n = b * s
    assert d == D_MODEL and s % ATTN_TILE == 0
    assert s // MOBA_BLOCK <= GATE_ROWS and TOKEN_TILE == ATTN_TILE == MOBA_BLOCK
    assert D_EXPERT == D_MODEL and RUN_CHUNK == SUBLANES
    n_tiles = n // TOKEN_TILE
    tiles_per_seq = s // TOKEN_TILE
    xf = x.reshape(n, d)

    off = [0]
    for wdt in (MOBA_WIDTH, MOBA_WIDTH, MOBA_WIDTH, MLA_Q_LORA, MLA_KV_LORA, MLA_ROPE_DIM, D_MODEL, D_MODEL):
        off.append(off[-1] + wdt)
    seg = [w_in[:, off[i]:off[i + 1]] for i in range(8)]
    kpe_cols = _mla_lanes(jnp.pad(seg[5], ((0, 0), (MLA_NOPE_DIM, 0))), 1)
    w_in_p = jnp.concatenate([_moba_lanes(seg[0]), _moba_lanes(seg[1])] + seg[3:5] + [kpe_cols]
                             + seg[6:], axis=1).astype(BF16)
    assert w_in_p.shape[1] == D_IN_PACKED
    w_va_t = seg[2].T.astype(BF16)
    w_uq_p = _mla_lanes(w_uq, MLA_HEADS).astype(BF16)
    w_ukv_h = w_ukv.reshape(MLA_KV_LORA, MLA_HEADS, MLA_NOPE_DIM + MLA_V_DIM)
    w_uk_p = _mla_lanes(jnp.pad(w_ukv_h[:, :, :MLA_NOPE_DIM], ((0, 0), (0, 0), (0, MLA_ROPE_DIM)))
                        .reshape(MLA_KV_LORA, -1), MLA_HEADS).astype(BF16)
    w_uv_t = w_ukv_h[:, :, MLA_NOPE_DIM:].reshape(MLA_KV_LORA, MLA_WIDTH).T.astype(BF16)
    gqa = _moba_lanes(_row(jnp.tile(moba_q_norm, 2))) * (MOBA_HEAD_DIM ** -0.5 * LOG2_E)
    gka = _moba_lanes(_row(jnp.tile(moba_k_norm, 2)))
    gqb = _mla_lanes(_row(mla_q_norm), 1) * (MLA_QK_DIM ** -0.5 * LOG2_E)
    gkb = _mla_lanes(_row(mla_k_norm), 1)
    half_a, half_b = MOBA_HEAD_DIM // 2, MLA_ROPE_DIM // 2
    cosa, sina = _rope_tables(s, half_a, (0, half_a), (LANES // 2, LANES // 2 + half_a))
    cosb, sinb = _rope_tables(s, half_b, (LANES // 2,), (0,))

    tok = lambda width: pl.BlockSpec((TOKEN_TILE, width), lambda i: (i, 0))
    whole = lambda arr: pl.BlockSpec(arr.shape, lambda i: (0,) * arr.ndim)
    seq_tab = pl.BlockSpec((TOKEN_TILE, LANES), lambda i: (i % tiles_per_seq, 0))
    vt_spec = pl.BlockSpec((1, MOBA_WIDTH, TOKEN_TILE), lambda i: (i, 0, 0))
    k_tiles_spec = lambda tiles: pl.BlockSpec((tiles, TOKEN_TILE, LANES), lambda i: (0, i, 0))
    params = pltpu.CompilerParams(dimension_semantics=("arbitrary",), vmem_limit_bytes=VMEM_LIMIT)

    consts1 = [_row(g_mix), w_in_p, w_va_t, gqa, gka]
    consts2 = [_row(mla_q_lat_norm), w_uq_p, _row(mla_kv_lat_norm), w_uk_p, w_uv_t, gqb, gkb]
    qa, ka, va, kmean, qb, kb, vb, ga, gb = pl.pallas_call(
        _inproj_kernel,
        grid=(n_tiles,),
        in_specs=([tok(d)] + [whole(a) for a in consts1] + [seq_tab] * 2
                  + [whole(a) for a in consts2] + [seq_tab] * 2),
        out_specs=[tok(MOBA_WIDTH), k_tiles_spec(MOBA_WIDTH // LANES), vt_spec,
                   pl.BlockSpec((1, 1, MOBA_WIDTH), lambda i: (i, 0, 0)),
                   tok(MLA_HEADS * LANES), k_tiles_spec(MLA_HEADS), vt_spec,
                   tok(d), tok(d)],
        out_shape=[jax.ShapeDtypeStruct((n, MOBA_WIDTH), BF16),
                   jax.ShapeDtypeStruct((MOBA_WIDTH // LANES, n, LANES), BF16),
                   jax.ShapeDtypeStruct((n_tiles, MOBA_WIDTH, TOKEN_TILE), BF16),
                   jax.ShapeDtypeStruct((n_tiles, 1, MOBA_WIDTH), F32),
                   jax.ShapeDtypeStruct((n, MLA_HEADS * LANES), BF16),
                   jax.ShapeDtypeStruct((MLA_HEADS, n, LANES), BF16),
                   jax.ShapeDtypeStruct((n_tiles, MLA_WIDTH, TOKEN_TILE), BF16)]
        + [jax.ShapeDtypeStruct((n, d), BF16)] * 2,
        compiler_params=params,
        name="in_projection",
    )(xf, *consts1, cosa, sina, *consts2, cosb, sinb)

    kmean = kmean.reshape(b, tiles_per_seq, MOBA_WIDTH)
    kmean = jnp.pad(kmean, ((0, 0), (0, GATE_ROWS - tiles_per_seq), (0, 0))).astype(BF16)

    r3 = lambda a: a.reshape(b, s, a.shape[-1])
    r4 = lambda a: a.reshape(b, tiles_per_seq, a.shape[1], TOKEN_TILE)
    ya = _attention(r3(qa), ka, r4(va), kmean, _logits_bounded(gqa, gka, MOBA_HEAD_DIM),
                    moba=True).reshape(n, MOBA_WIDTH)
    yb = _attention(r3(qb), kb, r4(vb), None, _logits_bounded(gqb, gkb, MLA_QK_DIM),
                    moba=False).reshape(n, MLA_WIDTH)

    consts3 = [w_branch_a.astype(BF16), w_branch_b.astype(BF16), w_out.astype(BF16), _row(g_ffn),
               w_router.T.astype(BF16), b_router.reshape(N_EXPERTS, 1).astype(F32)]
    x1, h_ffn, route_t, meta, cnt = pl.pallas_call(
        _merge_kernel,
        grid=(n_tiles,),
        in_specs=[tok(d), tok(MOBA_WIDTH), tok(MLA_WIDTH), tok(d), tok(d)]
        + [whole(a) for a in consts3],
        out_specs=[tok(d), tok(d),
                   pl.BlockSpec((1, 2 * TOP_K, TOKEN_TILE), lambda i: (i, 0, 0)),
                   pl.BlockSpec((1, N_EXPERTS, LANES), lambda i: (i, 0, 0)),
                   pl.BlockSpec((N_EXPERTS, LANES), lambda i: (0, 0))],
        out_shape=[jax.ShapeDtypeStruct((n, d), F32),
                   jax.ShapeDtypeStruct((n, d), BF16),
                   jax.ShapeDtypeStruct((n_tiles, 2 * TOP_K, TOKEN_TILE), F32),
                   jax.ShapeDtypeStruct((n_tiles, N_EXPERTS, LANES), F32),
                   jax.ShapeDtypeStruct((N_EXPERTS, LANES), F32)],
        scratch_shapes=[pltpu.VMEM((N_EXPERTS, LANES), F32)],
        compiler_params=params,
        name="merge_router",
    )(xf, ya, yb, ga, gb, *consts3)
    route = jnp.pad(route_t.transpose(0, 2, 1).reshape(n, 2 * TOP_K),
                    ((0, 0), (0, LANES - 2 * TOP_K)))

    rb = EXPERT_ROWS
    n_blocks = -(-(n * TOP_K + N_EXPERTS * (RUN_CHUNK - 1)) // rb) + N_EXPERTS
    n_rows = n_blocks * rb
    counts = cnt[:, 0].astype(jnp.int32)
    pcounts = ((counts + (RUN_CHUNK - 1) + rb - 1) // rb) * rb
    pends = jnp.cumsum(pcounts).astype(jnp.int32)
    pstarts = (pends - pcounts).astype(jnp.int32)
    nact = (pends[-1] // rb).astype(jnp.int32).reshape(1)
    blk = jnp.minimum(jnp.arange(n_blocks, dtype=jnp.int32), nact[0] - 1)
    blk_e = jnp.sum((pends[None, :] <= (blk * rb)[:, None]).astype(jnp.int32), axis=1)
    blk_e = jnp.minimum(blk_e, N_EXPERTS - 1)
    meta_i = jnp.pad(meta[:, :, :4].transpose(0, 2, 1), ((0, 0), (0, 0), (0, LANES - N_EXPERTS)))
    meta_i = meta_i.astype(jnp.int32).reshape(n_tiles * 4 * LANES)
    meta_spec = lambda shift: pl.BlockSpec(
        (4 * LANES,), lambda i, *_: (jnp.clip(i + shift, 0, n_tiles - 1),), memory_space=pltpu.SMEM)

    xr = pl.pallas_call(
        _dispatch_kernel,
        grid_spec=pltpu.PrefetchScalarGridSpec(
            num_scalar_prefetch=4,
            grid=(n_tiles,),
            in_specs=[meta_spec(0), meta_spec(-1),
                      pl.BlockSpec((1, 2 * TOP_K, TOKEN_TILE), lambda i, *_: (i, 0, 0)),
                      pl.BlockSpec((TOKEN_TILE, d), lambda i, *_: (i, 0))],
            out_specs=pl.BlockSpec(memory_space=pl.ANY),
            scratch_shapes=[pltpu.VMEM((2 * RUN_SLOTS * ROW_SLABS, LANES), F32),
                            pltpu.VMEM((ZERO_ROWS * ROW_SLABS, LANES), F32),
                            pltpu.SemaphoreType.DMA((2,)), pltpu.SemaphoreType.DMA(())]),
        out_shape=jax.ShapeDtypeStruct((n_rows * ROW_SLABS, LANES), F32),
        compiler_params=params,
        name="dispatch_rows",
    )(pstarts, pends, counts, nact, meta_i, meta_i, route_t, h_ffn)

    act_blk = lambda i, be, na: jnp.minimum(i, na[0] - 1)
    yr = pl.pallas_call(
        _expert_kernel,
        grid_spec=pltpu.PrefetchScalarGridSpec(
            num_scalar_prefetch=2,
            grid=(n_blocks,),
            in_specs=[
                pl.BlockSpec((rb * ROW_SLABS, LANES), lambda i, be, na: (act_blk(i, be, na), 0)),
                pl.BlockSpec(memory_space=pl.ANY),
                pl.BlockSpec((1, 1, 2 * D_EXPERT), lambda i, be, na: (be[i], 0, 0)),
                pl.BlockSpec(memory_space=pl.ANY),
                pl.BlockSpec((1, 1, d), lambda i, be, na: (be[i], 0, 0)),
            ],
            out_specs=pl.BlockSpec((rb * ROW_SLABS, LANES), lambda i, be, na: (i, 0)),
            scratch_shapes=[pltpu.VMEM((2, d, 2 * D_EXPERT), F32), pltpu.VMEM((2, D_EXPERT, d), F32),
                            pltpu.VMEM((d, 2 * D_EXPERT), BF16), pltpu.VMEM((D_EXPERT, d), BF16),
                            pltpu.SemaphoreType.DMA((2, 2))]),
        out_shape=jax.ShapeDtypeStruct((n_rows * ROW_SLABS, LANES), F32),
        compiler_params=params,
        name="experts",
    )(blk_e, nact, xr, w_gate_up.astype(F32),
      b_gate_up.reshape(N_EXPERTS, 1, -1).astype(F32), w_down.astype(F32),
      b_down.reshape(N_EXPERTS, 1, -1).astype(F32))

    ftok = lambda width: pl.BlockSpec((TOKEN_TILE, width), lambda i, ps: (i, 0))
    fwhole = lambda arr: pl.BlockSpec(arr.shape, lambda i, ps: (0,) * arr.ndim)
    consts4 = [_row(g_ple), w_ple_gate.astype(BF16), w_ple_proj.astype(BF16)]
    out = pl.pallas_call(
        _final_kernel,
        grid_spec=pltpu.PrefetchScalarGridSpec(
            num_scalar_prefetch=1,
            grid=(n_tiles,),
            in_specs=[meta_spec(0), meta_spec(1), ftok(d), ftok(LANES), ftok(PLE_DIM)]
            + [fwhole(a) for a in consts4] + [pl.BlockSpec(memory_space=pl.ANY)],
            out_specs=ftok(d),
            scratch_shapes=[pltpu.VMEM((2 * RUN_SLOTS * ROW_SLABS, LANES), F32),
                            pltpu.SemaphoreType.DMA((2,))]),
        out_shape=jax.ShapeDtypeStruct((n, d), F32),
        compiler_params=params,
        name="combine_ple",
    )(pstarts, meta_i, meta_i, x1, route, p_i.reshape(n, PLE_DIM), *consts4, yr)
    return out.reshape(b, s, d)


def kernel(x, p, g_mix, w_in, moba_q_norm, moba_k_norm, mla_q_lat_norm, w_uq, mla_kv_lat_norm, w_ukv, mla_q_norm, mla_k_norm, w_branch_a, w_branch_b, w_out, g_ffn, w_router, b_router, w_gate_up, b_gate_up, w_down, b_down, g_ple, w_ple_gate, w_ple_proj):
    for i in range(p.shape[0]):
        x = _layer(x, p[i], g_mix[i], w_in[i], moba_q_norm[i], moba_k_norm[i], mla_q_lat_norm[i],
                   w_uq[i], mla_kv_lat_norm[i], w_ukv[i], mla_q_norm[i], mla_k_norm[i],
                   w_branch_a[i], w_branch_b[i], w_out[i], g_ffn[i], w_router[i], b_router[i],
                   w_gate_up[i], b_gate_up[i], w_down[i], b_down[i], g_ple[i], w_ple_gate[i],
                   w_ple_proj[i])
    return x
```

```python
import functools

import numpy as np
import jax
import jax.numpy as jnp
from jax import lax
from jax.experimental import pallas as pl
from jax.experimental.pallas import tpu as pltpu

F32 = jnp.float32
BF16 = jnp.bfloat16

D_MODEL = 1024
PLE_DIM = 256
EPS = 1e-6
ROPE_THETA = 10000.0
MOBA_HEADS = 8
MOBA_HEAD_DIM = 64
MOBA_BLOCK = 256
MOBA_TOPK = 3
MOBA_WIDTH = MOBA_HEADS * MOBA_HEAD_DIM
MLA_HEADS = 8
MLA_Q_LORA = 256
MLA_KV_LORA = 128
MLA_NOPE_DIM = 64
MLA_ROPE_DIM = 32
MLA_V_DIM = 64
MLA_QK_DIM = MLA_NOPE_DIM + MLA_ROPE_DIM
MLA_WIDTH = MLA_HEADS * MLA_V_DIM
N_EXPERTS = 32
TOP_K = 4
D_EXPERT = 1024
SWIGLU_LIMIT = 7.0
SWIGLU_ALPHA = 1.702

LANES = 128
SUBLANES = 8
ROW_SLABS = D_MODEL // LANES
VMEM_LIMIT = 56 * 1024 * 1024

TOKEN_TILE = 256
ATTN_TILE = 256
ATTN_GROUPS = 4
EXPERT_ROWS = 512
RUN_CHUNK = 8
RUN_SLOTS = -(-(TOKEN_TILE * TOP_K + N_EXPERTS * (RUN_CHUNK - 1)) // 256) * 256

LOG2_E = 1.4426950408889634
LOGIT_LIMIT = 60.0
NEG = -1e30
MASK_BIAS = -1e9

C_QA, C_KA = 0, 512
C_CQ, C_CKV, C_KPE = 1024, 1280, 1408
C_GA, C_GB = 1536, 2560
D_IN_PACKED = 3584


def _rms(x, gain):
    return x * lax.rsqrt(jnp.mean(x * x, axis=-1, keepdims=True) + EPS) * gain


def _rope(t, cos, sin):
    return t * cos + pltpu.roll(t, LANES // 2, 1) * sin


def _moba_even_head(lane):
    return (lane & (MOBA_HEAD_DIM // 2)) == 0


def _inproj_kernel(x_ref, gmix_ref, win_ref, wvat_ref, gqa_ref, gka_ref, cosa_ref, sina_ref,
                   gql_ref, wuq_ref, gkvl_ref, wuk_ref, wuvt_ref, gqb_ref, gkb_ref,
                   cosb_ref, sinb_ref,
                   qa_ref, ka_ref, vat_ref, kmean_ref, qb_ref, kb_ref, vbt_ref, ga_ref, gb_ref):
    hn = _rms(x_ref[...], gmix_ref[...]).astype(BF16)

    def proj(c0, width):
        return jnp.dot(hn, win_ref[:, c0:c0 + width], preferred_element_type=F32)

    first = _moba_even_head(lax.broadcasted_iota(jnp.int32, (TOKEN_TILE, LANES), 1))
    cosa, sina = cosa_ref[...], sina_ref[...]

    def moba_norm_rope(t, gain):
        sq = t * t
        ss0 = jnp.sum(jnp.where(first, sq, 0.0), axis=-1, keepdims=True)
        ss1 = jnp.sum(jnp.where(first, 0.0, sq), axis=-1, keepdims=True)
        ms = jnp.where(first, ss0, ss1) * (1.0 / MOBA_HEAD_DIM)
        t = t * lax.rsqrt(ms + EPS) * gain
        return _rope(t, cosa, sina)

    qa = proj(C_QA, MOBA_WIDTH)
    ka = proj(C_KA, MOBA_WIDTH)
    for c in range(MOBA_WIDTH // LANES):
        sl = slice(c * LANES, (c + 1) * LANES)
        qa_ref[:, sl] = moba_norm_rope(qa[:, sl], gqa_ref[...]).astype(BF16)
        kc = moba_norm_rope(ka[:, sl], gka_ref[...])
        ka_ref[c] = kc.astype(BF16)
        kmean_ref[0, :, sl] = jnp.mean(kc, axis=0, keepdims=True)
    vat_ref[0] = lax.dot_general(wvat_ref[...], hn, _NT, preferred_element_type=F32).astype(BF16)

    cosb, sinb = cosb_ref[...], sinb_ref[...]

    def mla_norm_rope(t, gain):
        ms = jnp.sum(t * t, axis=-1, keepdims=True) * (1.0 / MLA_QK_DIM)
        t = t * lax.rsqrt(ms + EPS) * gain
        return _rope(t, cosb, sinb)

    cq = _rms(proj(C_CQ, MLA_Q_LORA), gql_ref[...]).astype(BF16)
    qb = jnp.dot(cq, wuq_ref[...], preferred_element_type=F32)
    ckv = _rms(proj(C_CKV, MLA_KV_LORA), gkvl_ref[...]).astype(BF16)
    kn = jnp.dot(ckv, wuk_ref[...], preferred_element_type=F32)
    kpe = proj(C_KPE, LANES)
    for h in range(MLA_HEADS):
        sl = slice(h * LANES, (h + 1) * LANES)
        qb_ref[:, sl] = mla_norm_rope(qb[:, sl], gqb_ref[...]).astype(BF16)
        kb_ref[h] = mla_norm_rope(kn[:, sl] + kpe, gkb_ref[...]).astype(BF16)
    vbt_ref[0] = lax.dot_general(wuvt_ref[...], ckv, _NT, preferred_element_type=F32).astype(BF16)

    ga_ref[...] = jax.nn.sigmoid(proj(C_GA, D_MODEL)).astype(BF16)
    gb_ref[...] = jax.nn.sigmoid(proj(C_GB, D_MODEL)).astype(BF16)


_NT = (((1,), (1,)), ((), ()))


GATE_ROWS = 16


def _attn_kernel(*refs, moba):
    if moba:
        bounded_ref, q_ref, k_ref, vt_ref, kmean_ref, o_ref = refs
    else:
        bounded_ref, q_ref, k_ref, vt_ref, o_ref = refs
    t = ATTN_TILE
    hd = MOBA_HEAD_DIM
    n_heads = 2 * ATTN_GROUPS
    qi = pl.program_id(2)
    key_i = lax.broadcasted_iota(jnp.int32, (t, t), 0)
    qry_i = lax.broadcasted_iota(jnp.int32, (t, t), 1)

    blk = lax.broadcasted_iota(jnp.int32, (GATE_ROWS, t), 0)
    heads, biases = [], []
    for hh in range(n_heads):
        if moba:
            lane = lax.broadcasted_iota(jnp.int32, (t, LANES), 1)
            even = _moba_even_head(lane)
            head_lanes = even if hh % 2 == 0 else jnp.logical_not(even)
            k_tile = hh // 2
            kcols = slice(k_tile * LANES, (k_tile + 1) * LANES)
            q = jnp.where(head_lanes, q_ref[0, :, kcols], jnp.zeros((), BF16))
            gate = lax.dot_general(kmean_ref[0, :, kcols], q, _NT, preferred_element_type=F32)
            g = jnp.where(blk < qi, gate, -jnp.inf)
            keep = jnp.zeros((GATE_ROWS, t), F32)
            for _ in range(MOBA_TOPK):
                gmax = jnp.max(g, axis=0, keepdims=True)
                pick = jnp.min(jnp.where(g == gmax, blk, GATE_ROWS), axis=0, keepdims=True)
                hit = blk == jnp.where(gmax > -jnp.inf, pick, GATE_ROWS)
                keep = jnp.where(hit, 1.0, keep)
                g = jnp.where(hit, -jnp.inf, g)
            biases.append(jnp.where(keep > 0.0, 0.0, MASK_BIAS))
        else:
            k_tile = hh
            q = q_ref[0, :, hh * LANES:(hh + 1) * LANES]
        heads.append((q, k_tile))

    def softmax_attend(shift_free):
        def update(s, vt_blk, state):
            m_prev, l_prev, acc = state
            if shift_free:
                p = jnp.exp2(s)
                l_new = l_prev + jnp.sum(p, axis=0, keepdims=True)
                acc = acc + jnp.dot(vt_blk, p.astype(BF16), preferred_element_type=F32)
                return m_prev, l_new, acc
            m_new = jnp.maximum(m_prev, jnp.max(s, axis=0, keepdims=True))
            alpha = jnp.exp2(m_prev - m_new)
            p = jnp.exp2(s - m_new)
            l_new = alpha * l_prev + jnp.sum(p, axis=0, keepdims=True)
            acc = alpha * acc + jnp.dot(vt_blk, p.astype(BF16), preferred_element_type=F32)
            return m_new, l_new, acc

        def past_block(j, states):
            start = pl.multiple_of(j * t, t)
            scores = [lax.dot_general(k_ref[k_tile, pl.ds(start, t), :], q, _NT,
                                      preferred_element_type=F32) for q, k_tile in heads]
            out = []
            for hh, s in enumerate(scores):
                if moba:
                    s = jnp.sum(jnp.where(blk == j, biases[hh], 0.0), axis=0, keepdims=True) + s
                out.append(update(s, vt_ref[0, j, hh * hd:(hh + 1) * hd, :], states[hh]))
            return tuple(out)

        init = (jnp.full((1, t), NEG, F32), jnp.zeros((1, t), F32), jnp.zeros((hd, t), F32))
        states = lax.fori_loop(0, qi, past_block, (init,) * n_heads)

        diag_start = pl.multiple_of(qi * t, t)
        scores = [lax.dot_general(k_ref[k_tile, pl.ds(diag_start, t), :], q, _NT,
                                  preferred_element_type=F32) for q, k_tile in heads]
        outs = []
        for hh, s in enumerate(scores):
            s = jnp.where(key_i <= qry_i, s, NEG)
            _, l_fin, acc = update(s, vt_ref[0, qi, hh * hd:(hh + 1) * hd, :], states[hh])
            outs.append(acc / l_fin)
        o_ref[0] = jnp.concatenate(outs, axis=0).T.astype(BF16)

    bounded = bounded_ref[0] == 1

    @pl.when(bounded)
    def _():
        softmax_attend(True)

    @pl.when(jnp.logical_not(bounded))
    def _():
        softmax_attend(False)


def _logits_bounded(q_gain, k_gain, dims):
    bound = 1.02 * dims * jnp.max(jnp.abs(q_gain)) * jnp.max(jnp.abs(k_gain))
    return jnp.where(bound <= LOGIT_LIMIT, 1, 0).astype(jnp.int32).reshape(1)


def _attention(q, k, vt, kmean, bounded, *, moba):
    b, s, _ = q.shape
    v_cols = ATTN_GROUPS * LANES
    steps = vt.shape[2] // v_cols
    nblk = s // ATTN_TILE
    qk_cols = v_cols if moba else 2 * v_cols
    k_tiles = qk_cols // LANES
    in_specs = [
        pl.BlockSpec((1, ATTN_TILE, qk_cols), lambda bi, gi, qi, fl: (bi, qi, gi)),
        pl.BlockSpec((k_tiles, s, LANES), lambda bi, gi, qi, fl: (gi, bi, 0)),
        pl.BlockSpec((1, nblk, v_cols, ATTN_TILE), lambda bi, gi, qi, fl: (bi, 0, gi, 0)),
    ]
    args = [q, k, vt]
    if moba:
        in_specs.append(pl.BlockSpec((1, GATE_ROWS, v_cols), lambda bi, gi, qi, fl: (bi, 0, gi)))
        args.append(kmean)
    return pl.pallas_call(
        functools.partial(_attn_kernel, moba=moba),
        grid_spec=pltpu.PrefetchScalarGridSpec(
            num_scalar_prefetch=1,
            grid=(b, steps, nblk),
            in_specs=in_specs,
            out_specs=pl.BlockSpec((1, ATTN_TILE, v_cols), lambda bi, gi, qi, fl: (bi, qi, gi))),
        out_shape=jax.ShapeDtypeStruct((b, s, steps * v_cols), BF16),
        compiler_params=pltpu.CompilerParams(
            dimension_semantics=("arbitrary", "arbitrary", "arbitrary"),
            vmem_limit_bytes=VMEM_LIMIT),
        name="moba_attention" if moba else "mla_attention",
    )(bounded, *args)


def _merge_kernel(x_ref, ya_ref, yb_ref, ga_ref, gb_ref, wa_ref, wb_ref, wo_ref, gffn_ref,
                  wr_ref, br_ref, x1_ref, h_ref, route_ref, meta_ref, cnt_ref, run_scr):
    t = TOKEN_TILE

    @pl.when(pl.program_id(0) == 0)
    def _():
        run_scr[...] = jnp.zeros_like(run_scr)

    merged = (ga_ref[...].astype(F32) * jnp.dot(ya_ref[...], wa_ref[...], preferred_element_type=F32)
              + gb_ref[...].astype(F32) * jnp.dot(yb_ref[...], wb_ref[...], preferred_element_type=F32))
    x1 = x_ref[...] + jnp.dot(merged.astype(BF16), wo_ref[...], preferred_element_type=F32)
    x1_ref[...] = x1
    h = _rms(x1, gffn_ref[...]).astype(BF16)
    h_ref[...] = h

    logits = lax.dot_general(wr_ref[...], h, _NT, preferred_element_type=F32) + br_ref[...]
    e_i = lax.broadcasted_iota(jnp.int32, (N_EXPERTS, t), 0)
    lg = logits
    hits = []
    top = None
    for r in range(TOP_K):
        gmax = jnp.max(lg, axis=0, keepdims=True)
        pick = jnp.min(jnp.where(lg == gmax, e_i, N_EXPERTS), axis=0, keepdims=True)
        hit = e_i == pick
        if r == 0:
            top = gmax
        hits.append(hit)
        lg = jnp.where(hit, -jnp.inf, lg)
    sel = jnp.where(lg == -jnp.inf, 1.0, 0.0)
    wgt = sel * jnp.exp(logits - top)
    wgt = wgt / jnp.sum(wgt, axis=0, keepdims=True)

    r_i = lax.broadcasted_iota(jnp.int32, (t, t), 0)
    c_i = lax.broadcasted_iota(jnp.int32, (t, t), 1)
    earlier = jnp.where(r_i < c_i, 1.0, 0.0).astype(BF16)
    rank_in_tile = jnp.dot(sel.astype(BF16), earlier, preferred_element_type=F32)
    tcnt = jnp.sum(sel, axis=1, keepdims=True)
    tpad = jnp.floor((tcnt + (RUN_CHUNK - 1)) * (1.0 / RUN_CHUNK)) * RUN_CHUNK
    e_r = lax.broadcasted_iota(jnp.int32, (N_EXPERTS, N_EXPERTS), 0)
    e_c = lax.broadcasted_iota(jnp.int32, (N_EXPERTS, N_EXPERTS), 1)
    before = jnp.where(e_c < e_r, 1.0, 0.0).astype(BF16)
    tbase = jnp.dot(before, jnp.broadcast_to(tpad, (N_EXPERTS, LANES)).astype(BF16),
                    preferred_element_type=F32)
    n_chunks = jnp.sum(tpad, axis=0, keepdims=True) * (1.0 / RUN_CHUNK)
    run = run_scr[...]
    run_new = run + tcnt
    run_scr[...] = run_new
    cnt_ref[...] = run_new
    lane = lax.broadcasted_iota(jnp.int32, (N_EXPERTS, LANES), 1)
    meta_ref[0] = jnp.where(lane == 0, tcnt, jnp.where(lane == 1, run, jnp.where(
        lane == 2, tbase, jnp.where(lane == 3, n_chunks, 0.0))))

    slot_of = rank_in_tile + tbase[:, 0:1]
    row = lax.broadcasted_iota(jnp.int32, (2 * TOP_K, t), 0)
    route = jnp.zeros((2 * TOP_K, t), F32)
    for r in range(TOP_K):
        w_r = jnp.sum(jnp.where(hits[r], wgt, 0.0), axis=0, keepdims=True)
        slot_r = jnp.sum(jnp.where(hits[r], slot_of, 0.0), axis=0, keepdims=True)
        route = jnp.where(row == r, w_r, route)
        route = jnp.where(row == TOP_K + r, slot_r, route)
    route_ref[0] = route


BIG_COPY = 4


def _for_each_run_copy(meta_ref, pstart_ref, fn):
    big_rows = BIG_COPY * RUN_CHUNK

    def per_expert(e, carry):
        n_chunks = lax.shift_right_logical(meta_ref[e] + (RUN_CHUNK - 1), RUN_CHUNK.bit_length() - 1)
        n_big = lax.shift_right_logical(n_chunks, BIG_COPY.bit_length() - 1)
        slot0 = meta_ref[2 * LANES + e]
        row0 = pstart_ref[e] + meta_ref[LANES + e]

        def big(b, cc):
            fn(slot0 + b * big_rows, row0 + b * big_rows, big_rows)
            return cc

        def small(c, cc):
            fn(slot0 + c * RUN_CHUNK, row0 + c * RUN_CHUNK, RUN_CHUNK)
            return cc

        lax.fori_loop(0, n_big, big, 0)
        lax.fori_loop(n_big * BIG_COPY, n_chunks, small, 0)
        return carry

    lax.fori_loop(0, N_EXPERTS, per_expert, 0)


WAIT_BATCH = 16


def _wait_run_chunks(meta_ref, make_copy):
    n_chunks = meta_ref[3 * LANES]
    n_batches = lax.shift_right_logical(n_chunks, WAIT_BATCH.bit_length() - 1)
    lax.fori_loop(0, n_batches, lambda b, c: (make_copy(WAIT_BATCH * RUN_CHUNK).wait(), c)[1], 0)
    lax.fori_loop(0, n_chunks - n_batches * WAIT_BATCH,
                  lambda b, c: (make_copy(RUN_CHUNK).wait(), c)[1], 0)


def _slab_rows(first_row, n_rows):
    return pl.ds(pl.multiple_of(first_row * ROW_SLABS, ROW_SLABS), n_rows * ROW_SLABS)


ZERO_ROWS = 256


def _dispatch_kernel(pstart_ref, pend_ref, count_ref, nact_ref, meta_ref, meta_prev_ref, route_ref,
                     h_ref, xr_hbm, stage, zero_scr, sems, zsem):
    t = TOKEN_TILE
    i = pl.program_id(0)
    half = lax.rem(i, 2)

    def run_copy(buf_half, slot, row, n_rows=RUN_CHUNK):
        src = stage.at[_slab_rows(buf_half * RUN_SLOTS + slot, n_rows), :]
        return pltpu.make_async_copy(src, xr_hbm.at[_slab_rows(row, n_rows), :], sems.at[buf_half])

    @pl.when(i == 0)
    def _():
        zero_scr[...] = jnp.zeros_like(zero_scr)
        shift = ZERO_ROWS.bit_length() - 1
        n_units = xr_hbm.shape[0] // (ZERO_ROWS * ROW_SLABS)
        first_unused = nact_ref[0] * (EXPERT_ROWS // ZERO_ROWS)

        def zero_copy(first_row):
            return pltpu.make_async_copy(zero_scr, xr_hbm.at[_slab_rows(first_row, ZERO_ROWS), :], zsem)

        def pad_units(e):
            pad = pend_ref[e] - pstart_ref[e] - count_ref[e]
            return lax.shift_right_logical(pad + (ZERO_ROWS - 1), shift)

        def zero_pad(e, c):
            lax.fori_loop(1, pad_units(e) + 1,
                          lambda b, cc: (zero_copy(pend_ref[e] - b * ZERO_ROWS).start(), cc)[1], 0)
            return c

        def wait_pad(e, c):
            lax.fori_loop(0, pad_units(e), lambda b, cc: (zero_copy(0).wait(), cc)[1], 0)
            return c

        def zero_tail(u, c):
            zero_copy(u * ZERO_ROWS).start()
            return c

        lax.fori_loop(0, N_EXPERTS, zero_pad, 0)
        lax.fori_loop(first_unused, n_units, zero_tail, 0)
        lax.fori_loop(0, N_EXPERTS, wait_pad, 0)
        lax.fori_loop(0, n_units - first_unused, lambda r, c: (zero_copy(0).wait(), c)[1], 0)

    slot_rows = route_ref[0, TOP_K:2 * TOP_K, :]
    s_i = lax.broadcasted_iota(jnp.int32, (RUN_SLOTS, t), 0).astype(F32)
    pick = jnp.zeros((RUN_SLOTS, t), F32)
    for k in range(TOP_K):
        pick = jnp.where(s_i == slot_rows[k:k + 1, :], 1.0, pick)
    rows = jnp.dot(pick.astype(BF16), h_ref[...], preferred_element_type=F32)
    base = half * (RUN_SLOTS * ROW_SLABS)
    for c in range(ROW_SLABS):
        stage[pl.ds(base + c, RUN_SLOTS, stride=ROW_SLABS), :] = rows[:, c * LANES:(c + 1) * LANES]

    @pl.when(i > 0)
    def _():
        _wait_run_chunks(meta_prev_ref, lambda n_rows: run_copy(1 - half, 0, 0, n_rows))

    _for_each_run_copy(meta_ref, pstart_ref, lambda s, r, n: run_copy(half, s, r, n).start())

    @pl.when(i == pl.num_programs(0) - 1)
    def _():
        _wait_run_chunks(meta_ref, lambda n_rows: run_copy(half, 0, 0, n_rows))


def _expert_kernel(blk_e_ref, nact_ref, rows_ref, xr_ref, wgu_hbm, bgu_ref, wdn_hbm, bdn_ref, yr_ref,
                   wgu_f32, wdn_f32, wgu_bf, wdn_bf, sems):
    r = EXPERT_ROWS
    i = pl.program_id(0)
    e = blk_e_ref[i]
    active = i < nact_ref[0]
    new_expert = jnp.logical_or(i == 0, e != blk_e_ref[jnp.maximum(i - 1, 0)])

    def weight_copies(expert):
        half = lax.rem(expert, 2)
        return (pltpu.make_async_copy(wgu_hbm.at[expert], wgu_f32.at[half], sems.at[0, half]),
                pltpu.make_async_copy(wdn_hbm.at[expert], wdn_f32.at[half], sems.at[1, half]))

    @pl.when(i == 0)
    def _():
        for cp in weight_copies(e):
            cp.start()

    @pl.when(jnp.logical_and(active, new_expert))
    def _():
        for cp in weight_copies(e):
            cp.wait()

        @pl.when(e + 1 < N_EXPERTS)
        def _():
            for cp in weight_copies(e + 1):
                cp.start()

        half = lax.rem(e, 2)

        def cast_rows(c, carry):
            rows = pl.ds(pl.multiple_of(c * LANES, LANES), LANES)
            wgu_bf[rows, :] = wgu_f32[half, rows, :].astype(BF16)
            wdn_bf[rows, :] = wdn_f32[half, rows, :].astype(BF16)
            return carry

        lax.fori_loop(0, D_MODEL // LANES, cast_rows, 0)

    def mlp_rows(n_rows):
        x = jnp.concatenate(
            [xr_ref[pl.ds(c, n_rows, stride=ROW_SLABS), :] for c in range(ROW_SLABS)], axis=1)
        gu = jnp.dot(x.astype(BF16), wgu_bf[...], preferred_element_type=F32) + bgu_ref[0]
        g = jnp.minimum(gu[:, :D_EXPERT], SWIGLU_LIMIT)
        u = jnp.clip(gu[:, D_EXPERT:], -SWIGLU_LIMIT, SWIGLU_LIMIT)
        act = (u + 1.0) * (g * jax.nn.sigmoid(SWIGLU_ALPHA * g))
        y = jnp.dot(act.astype(BF16), wdn_bf[...], preferred_element_type=F32) + bdn_ref[0]
        for c in range(ROW_SLABS):
            yr_ref[pl.ds(c, n_rows, stride=ROW_SLABS), :] = y[:, c * LANES:(c + 1) * LANES]

    more_than_half = rows_ref[i] > r // 2

    @pl.when(jnp.logical_and(active, more_than_half))
    def _():
        mlp_rows(r)

    @pl.when(jnp.logical_and(active, jnp.logical_not(more_than_half)))
    def _():
        mlp_rows(r // 2)
        yr_ref[pl.ds((r // 2) * ROW_SLABS, (r // 2) * ROW_SLABS), :] = jnp.zeros(
            ((r // 2) * ROW_SLABS, LANES), F32)

    @pl.when(jnp.logical_not(active))
    def _():
        yr_ref[...] = jnp.zeros_like(yr_ref)


def _final_kernel(pstart_ref, meta_ref, meta_next_ref, x1_ref, route_ref, p_ref, gple_ref,
                  wpg_ref, wpp_ref, yr_hbm, o_ref, gstage, sems):
    t = TOKEN_TILE
    i = pl.program_id(0)
    half = lax.rem(i, 2)

    def run_copy(buf_half, slot, row, n_rows=RUN_CHUNK):
        dst = gstage.at[_slab_rows(buf_half * RUN_SLOTS + slot, n_rows), :]
        return pltpu.make_async_copy(yr_hbm.at[_slab_rows(row, n_rows), :], dst, sems.at[buf_half])

    @pl.when(i == 0)
    def _():
        gstage[...] = jnp.zeros_like(gstage)
        _for_each_run_copy(meta_ref, pstart_ref, lambda s, r, n: run_copy(0, s, r, n).start())

    @pl.when(i + 1 < pl.num_programs(0))
    def _():
        _for_each_run_copy(meta_next_ref, pstart_ref,
                           lambda s, r, n: run_copy(1 - half, s, r, n).start())

    _wait_run_chunks(meta_ref, lambda n_rows: run_copy(half, 0, 0, n_rows))

    base = half * (RUN_SLOTS * ROW_SLABS)
    rows = jnp.concatenate(
        [gstage[pl.ds(base + c, RUN_SLOTS, stride=ROW_SLABS), :] for c in range(ROW_SLABS)], axis=1)
    route = route_ref[...]
    s_i = lax.broadcasted_iota(jnp.int32, (t, RUN_SLOTS), 1).astype(F32)
    wmat = jnp.zeros((t, RUN_SLOTS), F32)
    for k in range(TOP_K):
        wmat = jnp.where(s_i == route[:, TOP_K + k:TOP_K + k + 1], route[:, k:k + 1], wmat)
    rows_bf = rows.astype(BF16)
    w_hi = wmat.astype(BF16)
    w_lo = (wmat - w_hi.astype(F32)).astype(BF16)
    y = (jnp.dot(w_hi, rows_bf, preferred_element_type=F32)
         + jnp.dot(w_lo, rows_bf, preferred_element_type=F32))
    x2 = x1_ref[...] + y
    hp = _rms(x2, gple_ref[...]).astype(BF16)
    gate = jax.nn.sigmoid(jnp.dot(hp, wpg_ref[...], preferred_element_type=F32))
    emb = jnp.dot(p_ref[...].astype(BF16), wpp_ref[...], preferred_element_type=F32)
    o_ref[...] = x2 + gate * emb


def _rope_tables(s, half, x1_starts, x2_starts):
    inv_freq = ROPE_THETA ** (-(np.arange(half, dtype=np.float64) / half))
    ang = np.arange(s, dtype=np.float64)[:, None] * inv_freq[None, :]
    cos, sin = np.cos(ang), np.sin(ang)
    cos_t, sin_t = np.ones((s, LANES), np.float32), np.zeros((s, LANES), np.float32)
    for st in x1_starts:
        cos_t[:, st:st + half] = cos
        sin_t[:, st:st + half] = -sin
    for st in x2_starts:
        cos_t[:, st:st + half] = cos
        sin_t[:, st:st + half] = sin
    return jnp.asarray(cos_t), jnp.asarray(sin_t)


_MOBA_LANE_COLS = tuple(list(range(0, 32)) + list(range(64, 96)) + list(range(32, 64))
                        + list(range(96, 128)))
_MLA_LANE_DIMS = tuple(list(range(80, 96)) + list(range(0, 48)) + list(range(64, 80))
                       + list(range(48, 64)) + [MLA_QK_DIM] * 32)


def _moba_lanes(w):
    k, width = w.shape
    cols = jnp.asarray(_MOBA_LANE_COLS, jnp.int32)
    return w.reshape(k, width // LANES, LANES)[:, :, cols].reshape(k, width)


def _mla_lanes(w, heads):
    k = w.shape[0]
    w = jnp.pad(w.reshape(k, heads, MLA_QK_DIM), ((0, 0), (0, 0), (0, 1)))
    return w[:, :, jnp.asarray(_MLA_LANE_DIMS, jnp.int32)].reshape(k, heads * LANES)


def _row(v):
    return v.reshape(1, -1).astype(F32)


def _layer(x, p_i, g_mix, w_in, moba_q_norm, moba_k_norm, mla_q_lat_norm, w_uq, mla_kv_lat_norm,
           w_ukv, mla_q_norm, mla_k_norm, w_branch_a, w_branch_b, w_out, g_ffn, w_router, b_router,
           w_gate_up, b_gate_up, w_down, b_down, g_ple, w_ple_gate, w_ple_proj):
    b, s, d = x.shape
    n = b * s
    assert d == D_MODEL and s % ATTN_TILE == 0
    assert s // MOBA_BLOCK <= GATE_ROWS and TOKEN_TILE == ATTN_TILE == MOBA_BLOCK
    assert D_EXPERT == D_MODEL and RUN_CHUNK == SUBLANES
    n_tiles = n // TOKEN_TILE
    tiles_per_seq = s // TOKEN_TILE
    xf = x.reshape(n, d)

    off = [0]
    for wdt in (MOBA_WIDTH, MOBA_WIDTH, MOBA_WIDTH, MLA_Q_LORA, MLA_KV_LORA, MLA_ROPE_DIM, D_MODEL, D_MODEL):
        off.append(off[-1] + wdt)
    seg = [w_in[:, off[i]:off[i + 1]] for i in range(8)]
    kpe_cols = _mla_lanes(jnp.pad(seg[5], ((0, 0), (MLA_NOPE_DIM, 0))), 1)
    w_in_p = jnp.concatenate([_moba_lanes(seg[0]), _moba_lanes(seg[1])] + seg[3:5] + [kpe_cols]
                             + seg[6:], axis=1).astype(BF16)
    assert w_in_p.shape[1] == D_IN_PACKED
    w_va_t = seg[2].T.astype(BF16)
    w_uq_p = _mla_lanes(w_uq, MLA_HEADS).astype(BF16)
    w_ukv_h = w_ukv.reshape(MLA_KV_LORA, MLA_HEADS, MLA_NOPE_DIM + MLA_V_DIM)
    w_uk_p = _mla_lanes(jnp.pad(w_ukv_h[:, :, :MLA_NOPE_DIM], ((0, 0), (0, 0), (0, MLA_ROPE_DIM)))
                        .reshape(MLA_KV_LORA, -1), MLA_HEADS).astype(BF16)
    w_uv_t = w_ukv_h[:, :, MLA_NOPE_DIM:].reshape(MLA_KV_LORA, MLA_WIDTH).T.astype(BF16)
    gqa = _moba_lanes(_row(jnp.tile(moba_q_norm, 2))) * (MOBA_HEAD_DIM ** -0.5 * LOG2_E)
    gka = _moba_lanes(_row(jnp.tile(moba_k_norm, 2)))
    gqb = _mla_lanes(_row(mla_q_norm), 1) * (MLA_QK_DIM ** -0.5 * LOG2_E)
    gkb = _mla_lanes(_row(mla_k_norm), 1)
    half_a, half_b = MOBA_HEAD_DIM // 2, MLA_ROPE_DIM // 2
    cosa, sina = _rope_tables(s, half_a, (0, half_a), (LANES // 2, LANES // 2 + half_a))
    cosb, sinb = _rope_tables(s, half_b, (LANES // 2,), (0,))

    tok = lambda width: pl.BlockSpec((TOKEN_TILE, width), lambda i: (i, 0))
    whole = lambda arr: pl.BlockSpec(arr.shape, lambda i: (0,) * arr.ndim)
    seq_tab = pl.BlockSpec((TOKEN_TILE, LANES), lambda i: (i % tiles_per_seq, 0))
    vt_spec = pl.BlockSpec((1, MOBA_WIDTH, TOKEN_TILE), lambda i: (i, 0, 0))
    k_tiles_spec = lambda tiles: pl.BlockSpec((tiles, TOKEN_TILE, LANES), lambda i: (0, i, 0))
    params = pltpu.CompilerParams(dimension_semantics=("arbitrary",), vmem_limit_bytes=VMEM_LIMIT)

    consts1 = [_row(g_mix), w_in_p, w_va_t, gqa, gka]
    consts2 = [_row(mla_q_lat_norm), w_uq_p, _row(mla_kv_lat_norm), w_uk_p, w_uv_t, gqb, gkb]
    qa, ka, va, kmean, qb, kb, vb, ga, gb = pl.pallas_call(
        _inproj_kernel,
        grid=(n_tiles,),
        in_specs=([tok(d)] + [whole(a) for a in consts1] + [seq_tab] * 2
                  + [whole(a) for a in consts2] + [seq_tab] * 2),
        out_specs=[tok(MOBA_WIDTH), k_tiles_spec(MOBA_WIDTH // LANES), vt_spec,
                   pl.BlockSpec((1, 1, MOBA_WIDTH), lambda i: (i, 0, 0)),
                   tok(MLA_HEADS * LANES), k_tiles_spec(MLA_HEADS), vt_spec,
                   tok(d), tok(d)],
        out_shape=[jax.ShapeDtypeStruct((n, MOBA_WIDTH), BF16),
                   jax.ShapeDtypeStruct((MOBA_WIDTH // LANES, n, LANES), BF16),
                   jax.ShapeDtypeStruct((n_tiles, MOBA_WIDTH, TOKEN_TILE), BF16),
                   jax.ShapeDtypeStruct((n_tiles, 1, MOBA_WIDTH), F32),
                   jax.ShapeDtypeStruct((n, MLA_HEADS * LANES), BF16),
                   jax.ShapeDtypeStruct((MLA_HEADS, n, LANES), BF16),
                   jax.ShapeDtypeStruct((n_tiles, MLA_WIDTH, TOKEN_TILE), BF16)]
        + [jax.ShapeDtypeStruct((n, d), BF16)] * 2,
        compiler_params=params,
        name="in_projection",
    )(xf, *consts1, cosa, sina, *consts2, cosb, sinb)

    kmean = kmean.reshape(b, tiles_per_seq, MOBA_WIDTH)
    kmean = jnp.pad(kmean, ((0, 0), (0, GATE_ROWS - tiles_per_seq), (0, 0))).astype(BF16)

    r3 = lambda a: a.reshape(b, s, a.shape[-1])
    r4 = lambda a: a.reshape(b, tiles_per_seq, a.shape[1], TOKEN_TILE)
    ya = _attention(r3(qa), ka, r4(va), kmean, _logits_bounded(gqa, gka, MOBA_HEAD_DIM),
                    moba=True).reshape(n, MOBA_WIDTH)
    yb = _attention(r3(qb), kb, r4(vb), None, _logits_bounded(gqb, gkb, MLA_QK_DIM),
                    moba=False).reshape(n, MLA_WIDTH)

    consts3 = [w_branch_a.astype(BF16), w_branch_b.astype(BF16), w_out.astype(BF16), _row(g_ffn),
               w_router.T.astype(BF16), b_router.reshape(N_EXPERTS, 1).astype(F32)]
    x1, h_ffn, route_t, meta, cnt = pl.pallas_call(
        _merge_kernel,
        grid=(n_tiles,),
        in_specs=[tok(d), tok(MOBA_WIDTH), tok(MLA_WIDTH), tok(d), tok(d)]
        + [whole(a) for a in consts3],
        out_specs=[tok(d), tok(d),
                   pl.BlockSpec((1, 2 * TOP_K, TOKEN_TILE), lambda i: (i, 0, 0)),
                   pl.BlockSpec((1, N_EXPERTS, LANES), lambda i: (i, 0, 0)),
                   pl.BlockSpec((N_EXPERTS, LANES), lambda i: (0, 0))],
        out_shape=[jax.ShapeDtypeStruct((n, d), F32),
                   jax.ShapeDtypeStruct((n, d), BF16),
                   jax.ShapeDtypeStruct((n_tiles, 2 * TOP_K, TOKEN_TILE), F32),
                   jax.ShapeDtypeStruct((n_tiles, N_EXPERTS, LANES), F32),
                   jax.ShapeDtypeStruct((N_EXPERTS, LANES), F32)],
        scratch_shapes=[pltpu.VMEM((N_EXPERTS, LANES), F32)],
        compiler_params=params,
        name="merge_router",
    )(xf, ya, yb, ga, gb, *consts3)
    route = jnp.pad(route_t.transpose(0, 2, 1).reshape(n, 2 * TOP_K),
                    ((0, 0), (0, LANES - 2 * TOP_K)))

    rb = EXPERT_ROWS
    n_blocks = -(-(n * TOP_K + N_EXPERTS * (RUN_CHUNK - 1)) // rb) + N_EXPERTS
    n_rows = n_blocks * rb
    counts = cnt[:, 0].astype(jnp.int32)
    pcounts = ((counts + (RUN_CHUNK - 1) + rb - 1) // rb) * rb
    pends = jnp.cumsum(pcounts).astype(jnp.int32)
    pstarts = (pends - pcounts).astype(jnp.int32)
    nact = (pends[-1] // rb).astype(jnp.int32).reshape(1)
    blk = jnp.minimum(jnp.arange(n_blocks, dtype=jnp.int32), nact[0] - 1)
    blk_e = jnp.sum((pends[None, :] <= (blk * rb)[:, None]).astype(jnp.int32), axis=1)
    blk_e = jnp.minimum(blk_e, N_EXPERTS - 1)
    meta_i = jnp.pad(meta[:, :, :4].transpose(0, 2, 1), ((0, 0), (0, 0), (0, LANES - N_EXPERTS)))
    meta_i = meta_i.astype(jnp.int32).reshape(n_tiles * 4 * LANES)
    meta_spec = lambda shift: pl.BlockSpec(
        (4 * LANES,), lambda i, *_: (jnp.clip(i + shift, 0, n_tiles - 1),), memory_space=pltpu.SMEM)

    xr = pl.pallas_call(
        _dispatch_kernel,
        grid_spec=pltpu.PrefetchScalarGridSpec(
            num_scalar_prefetch=4,
            grid=(n_tiles,),
            in_specs=[meta_spec(0), meta_spec(-1),
                      pl.BlockSpec((1, 2 * TOP_K, TOKEN_TILE), lambda i, *_: (i, 0, 0)),
                      pl.BlockSpec((TOKEN_TILE, d), lambda i, *_: (i, 0))],
            out_specs=pl.BlockSpec(memory_space=pl.ANY),
            scratch_shapes=[pltpu.VMEM((2 * RUN_SLOTS * ROW_SLABS, LANES), F32),
                            pltpu.VMEM((ZERO_ROWS * ROW_SLABS, LANES), F32),
                            pltpu.SemaphoreType.DMA((2,)), pltpu.SemaphoreType.DMA(())]),
        out_shape=jax.ShapeDtypeStruct((n_rows * ROW_SLABS, LANES), F32),
        compiler_params=params,
        name="dispatch_rows",
    )(pstarts, pends, counts, nact, meta_i, meta_i, route_t, h_ffn)

    of_blk = lambda v: jnp.sum(jnp.where(
        jnp.arange(N_EXPERTS, dtype=jnp.int32)[None, :] == blk_e[:, None], v[None, :], 0), axis=1)
    blk_rows = jnp.clip(of_blk(counts) - (blk * rb - of_blk(pstarts)), 0, rb).astype(jnp.int32)
    yr = pl.pallas_call(
        _expert_kernel,
        grid_spec=pltpu.PrefetchScalarGridSpec(
            num_scalar_prefetch=3,
            grid=(n_blocks,),
            in_specs=[
                pl.BlockSpec((rb * ROW_SLABS, LANES),
                             lambda i, be, na, br: (jnp.minimum(i, na[0] - 1), 0)),
                pl.BlockSpec(memory_space=pl.ANY),
                pl.BlockSpec((1, 1, 2 * D_EXPERT), lambda i, be, na, br: (be[i], 0, 0)),
                pl.BlockSpec(memory_space=pl.ANY),
                pl.BlockSpec((1, 1, d), lambda i, be, na, br: (be[i], 0, 0)),
            ],
            out_specs=pl.BlockSpec((rb * ROW_SLABS, LANES), lambda i, be, na, br: (i, 0)),
            scratch_shapes=[pltpu.VMEM((2, d, 2 * D_EXPERT), F32), pltpu.VMEM((2, D_EXPERT, d), F32),
                            pltpu.VMEM((d, 2 * D_EXPERT), BF16), pltpu.VMEM((D_EXPERT, d), BF16),
                            pltpu.SemaphoreType.DMA((2, 2))]),
        out_shape=jax.ShapeDtypeStruct((n_rows * ROW_SLABS, LANES), F32),
        compiler_params=params,
        name="experts",
    )(blk_e, nact, blk_rows, xr, w_gate_up.astype(F32),
      b_gate_up.reshape(N_EXPERTS, 1, -1).astype(F32), w_down.astype(F32),
      b_down.reshape(N_EXPERTS, 1, -1).astype(F32))

    ftok = lambda width: pl.BlockSpec((TOKEN_TILE, width), lambda i, ps: (i, 0))
    fwhole = lambda arr: pl.BlockSpec(arr.shape, lambda i, ps: (0,) * arr.ndim)
    consts4 = [_row(g_ple), w_ple_gate.astype(BF16), w_ple_proj.astype(BF16)]
    out = pl.pallas_call(
        _final_kernel,
        grid_spec=pltpu.PrefetchScalarGridSpec(
            num_scalar_prefetch=1,
            grid=(n_tiles,),
            in_specs=[meta_spec(0), meta_spec(1), ftok(d), ftok(LANES), ftok(PLE_DIM)]
            + [fwhole(a) for a in consts4] + [pl.BlockSpec(memory_space=pl.ANY)],
            out_specs=ftok(d),
            scratch_shapes=[pltpu.VMEM((2 * RUN_SLOTS * ROW_SLABS, LANES), F32),
                            pltpu.SemaphoreType.DMA((2,))]),
        out_shape=jax.ShapeDtypeStruct((n, d), F32),
        compiler_params=params,
        name="combine_ple",
    )(pstarts, meta_i, meta_i, x1, route, p_i.reshape(n, PLE_DIM), *consts4, yr)
    return out.reshape(b, s, d)


def kernel(x, p, g_mix, w_in, moba_q_norm, moba_k_norm, mla_q_lat_norm, w_uq, mla_kv_lat_norm, w_ukv, mla_q_norm, mla_k_norm, w_branch_a, w_branch_b, w_out, g_ffn, w_router, b_router, w_gate_up, b_gate_up, w_down, b_down, g_ple, w_ple_gate, w_ple_proj):
    for i in range(p.shape[0]):
        x = _layer(x, p[i], g_mix[i], w_in[i], moba_q_norm[i], moba_k_norm[i], mla_q_lat_norm[i],
                   w_uq[i], mla_kv_lat_norm[i], w_ukv[i], mla_q_norm[i], mla_k_norm[i],
                   w_branch_a[i], w_branch_b[i], w_out[i], g_ffn[i], w_router[i], b_router[i],
                   w_gate_up[i], b_gate_up[i], w_down[i], b_down[i], g_ple[i], w_ple_gate[i],
                   w_ple_proj[i])
    return x
```
